```python
import jax, jax.numpy as jnp
from jax import lax
import numpy as np

D_MODEL = 4096
BATCH = 1
SEQ = 8192
DEPTH = 1

MIX_WIDTH = D_MODEL
CONV_WIDTH = MIX_WIDTH // 2
ATTN_WIDTH = MIX_WIDTH - CONV_WIDTH
HEAD_DIM = 128
N_Q_HEADS = ATTN_WIDTH // HEAD_DIM
N_KV_HEADS = max(1, N_Q_HEADS // 4)
GQA_GROUP = N_Q_HEADS // N_KV_HEADS
KV_WIDTH = N_KV_HEADS * HEAD_DIM
WINDOW = 128
BLOCK = 128
DN_ALPHA = (2.0 * DEPTH) ** 0.25
DN_BETA = (8.0 * DEPTH) ** -0.25
LN_EPS = 1e-5
NEG_INF = -1e30

OFF_CB = 0
OFF_CC = OFF_CB + CONV_WIDTH
OFF_CH = OFF_CC + CONV_WIDTH
OFF_CZ = OFF_CH + CONV_WIDTH
OFF_Q = OFF_CZ + CONV_WIDTH
OFF_K = OFF_Q + ATTN_WIDTH
OFF_V = OFF_K + KV_WIDTH
OFF_AZ = OFF_V + KV_WIDTH
PROJ_WIDTH = OFF_AZ + ATTN_WIDTH

kernel_name = "hybrid_shortconv_swa_deepnorm_encoder"


def layer_norm(x, g, b):
    xf = x.astype(jnp.float32)
    mu = jnp.mean(xf, axis=-1, keepdims=True)
    var = jnp.mean(jnp.square(xf - mu), axis=-1, keepdims=True)
    y = (xf - mu) * lax.rsqrt(var + LN_EPS) * g.astype(jnp.float32) + b.astype(jnp.float32)
    return y.astype(x.dtype)


def alibi_slopes(n_heads):
    h = jnp.arange(1, n_heads + 1, dtype=jnp.float32)
    return jnp.exp2(-8.0 * h / n_heads)


def centred_short_conv(u, w):
    up = jnp.pad(u, ((0, 0), (1, 1), (0, 0)))
    return up[:, :-2] * w[0] + up[:, 1:-1] * w[1] + up[:, 2:] * w[2]


def banded_window_attention(q, k, v, sink):
    b, s, _ = q.shape
    nb = s // BLOCK
    qb = q.reshape(b, nb, BLOCK, N_KV_HEADS, GQA_GROUP, HEAD_DIM).astype(jnp.float32)

    def band(t):
        t = t.reshape(b, nb, BLOCK, N_KV_HEADS, HEAD_DIM).astype(jnp.float32)
        tp = jnp.pad(t, ((0, 0), (1, 1), (0, 0), (0, 0), (0, 0)))
        return jnp.concatenate([tp[:, :-2], tp[:, 1:-1], tp[:, 2:]], axis=2)

    kb, vb = band(k), band(v)
    scores = jnp.einsum("bnqhgd,bnkhd->bnhgqk", qb, kb) * (HEAD_DIM ** -0.5)

    blk = jnp.arange(nb, dtype=jnp.int32)[:, None]
    q_pos = blk * BLOCK + jnp.arange(BLOCK, dtype=jnp.int32)[None, :]
    k_pos = (blk - 1) * BLOCK + jnp.arange(3 * BLOCK, dtype=jnp.int32)[None, :]
    dist = jnp.abs(q_pos[:, :, None] - k_pos[:, None, :])
    valid = (dist <= WINDOW) & (k_pos[:, None, :] >= 0) & (k_pos[:, None, :] < s)

    slopes = alibi_slopes(N_Q_HEADS).reshape(N_KV_HEADS, GQA_GROUP)
    bias = -slopes[None, None, :, :, None, None] * dist[None, :, None, None].astype(jnp.float32)
    scores = jnp.where(valid[None, :, None, None], scores + bias, NEG_INF)

    sink_l = sink.astype(jnp.float32).reshape(1, 1, N_KV_HEADS, GQA_GROUP, 1, 1)
    m = jnp.maximum(jnp.max(scores, axis=-1, keepdims=True), sink_l)
    p = jnp.exp(scores - m)
    denom = jnp.sum(p, axis=-1, keepdims=True) + jnp.exp(sink_l - m)
    out = jnp.einsum("bnhgqk,bnkhd->bnqhgd", p / denom, vb)
    return out.reshape(b, s, ATTN_WIDTH).astype(q.dtype)


def hybrid_layer(x, w_in, conv_w, sink, w_out, ln_g, ln_b):
    p = x @ w_in
    c_b = p[..., OFF_CB:OFF_CC]
    c_c = p[..., OFF_CC:OFF_CH]
    c_h = p[..., OFF_CH:OFF_CZ]
    c_z = p[..., OFF_CZ:OFF_Q]
    a_q = p[..., OFF_Q:OFF_K]
    a_k = p[..., OFF_K:OFF_V]
    a_v = p[..., OFF_V:OFF_AZ]
    a_z = p[..., OFF_AZ:PROJ_WIDTH]

    y_conv = c_b * centred_short_conv(c_c * c_h, conv_w) * jax.nn.silu(c_z)
    y_attn = banded_window_attention(a_q, a_k, a_v, sink) * jax.nn.silu(a_z)

    y = jnp.concatenate([y_conv, y_attn], axis=-1) @ w_out
    return layer_norm(DN_ALPHA * x + y, ln_g, ln_b)


def setup_inputs(seed: int = 0) -> dict:
    key = jax.random.key(seed)
    ks = jax.random.split(key, 10)
    x = jax.random.normal(ks[0], (BATCH, SEQ, D_MODEL), jnp.float32)
    emb_ln_g = 1.0 + 0.02 * jax.random.normal(ks[1], (D_MODEL,), jnp.float32)
    emb_ln_b = 0.02 * jax.random.normal(ks[2], (D_MODEL,), jnp.float32)
    col_scale = (jnp.ones((PROJ_WIDTH,), jnp.float32)
                 .at[OFF_CH:OFF_CZ].set(DN_BETA)
                 .at[OFF_V:OFF_AZ].set(DN_BETA))
    w_in = (jax.random.normal(ks[3], (DEPTH, D_MODEL, PROJ_WIDTH), jnp.float32)
            * (D_MODEL ** -0.5) * col_scale)
    conv_w = jax.random.normal(ks[4], (DEPTH, 3, CONV_WIDTH), jnp.float32) * (3.0 ** -0.5)
    sink = 0.5 * jax.random.normal(ks[5], (DEPTH, N_Q_HEADS), jnp.float32)
    w_out = (jax.random.normal(ks[6], (DEPTH, MIX_WIDTH, D_MODEL), jnp.float32)
             * (MIX_WIDTH ** -0.5) * DN_BETA)
    ln_g = 1.0 + 0.02 * jax.random.normal(ks[7], (DEPTH, D_MODEL), jnp.float32)
    ln_b = 0.02 * jax.random.normal(ks[8], (DEPTH, D_MODEL), jnp.float32)
    return {"x": x, "emb_ln_g": emb_ln_g, "emb_ln_b": emb_ln_b, "w_in": w_in,
            "conv_w": conv_w, "sink": sink, "w_out": w_out, "ln_g": ln_g, "ln_b": ln_b}


def reference(x, emb_ln_g, emb_ln_b, w_in, conv_w, sink, w_out, ln_g, ln_b):
    h = layer_norm(x, emb_ln_g, emb_ln_b)
    for l in range(DEPTH):
        h = hybrid_layer(h, w_in[l], conv_w[l], sink[l], w_out[l], ln_g[l], ln_b[l])
    return h
```

```python
import functools

import jax
import jax.numpy as jnp
from jax import lax
from jax.experimental import pallas as pl
from jax.experimental.pallas import tpu as pltpu

D_MODEL = 4096
SEQ = 8192
CONV_WIDTH = D_MODEL // 2
ATTN_WIDTH = D_MODEL - CONV_WIDTH
HEAD_DIM = 128
N_Q_HEADS = ATTN_WIDTH // HEAD_DIM
N_KV_HEADS = N_Q_HEADS // 4
GQA_GROUP = N_Q_HEADS // N_KV_HEADS
KV_WIDTH = N_KV_HEADS * HEAD_DIM
WINDOW = 128
BLOCK = 128
N_BLOCKS = SEQ // BLOCK
DN_ALPHA = 2.0 ** 0.25
LN_EPS = 1e-5
NEG_INF = -1e30
PROJ_WIDTH = 4 * CONV_WIDTH + ATTN_WIDTH + 2 * KV_WIDTH + ATTN_WIDTH

OFF_CB = 0
OFF_CC = OFF_CB + CONV_WIDTH
OFF_CH = OFF_CC + CONV_WIDTH
OFF_CZ = OFF_CH + CONV_WIDTH
OFF_Q = OFF_CZ + CONV_WIDTH
OFF_K = OFF_Q + ATTN_WIDTH
OFF_V = OFF_K + KV_WIDTH
OFF_AZ = OFF_V + KV_WIDTH

V7X_VMEM_LIMIT_BYTES = 58 * 1024 * 1024

LN_CHUNK_ROWS = 32

IN_TM, IN_TN = 1024, 512
OUT_TM, OUT_TN = 512, 512
HALO_ROWS = 16
CONV_CHUNK = 256


def _layer_norm_rows(x, g, b):
    mu = jnp.mean(x, axis=-1, keepdims=True)
    xc = x - mu
    var = jnp.mean(xc * xc, axis=-1, keepdims=True)
    return xc * lax.rsqrt(var + LN_EPS) * g + b


def _silu(z):
    return z / (1.0 + jnp.exp(-z))


def _in_proj_kernel(x_ref, g_ref, b_ref, w_ref, p_ref, h_ref):
    @pl.when(pl.program_id(1) == 0)
    def _():
        g = g_ref[...]
        b = b_ref[...]

        def body(r, carry):
            rows = pl.ds(pl.multiple_of(r * LN_CHUNK_ROWS, LN_CHUNK_ROWS), LN_CHUNK_ROWS)
            h_ref[rows, :] = _layer_norm_rows(x_ref[rows, :], g, b).astype(jnp.bfloat16)
            return carry

        lax.fori_loop(0, IN_TM // LN_CHUNK_ROWS, body, 0)

    p_ref[...] = jnp.dot(h_ref[...], w_ref[...],
                         preferred_element_type=jnp.float32).astype(jnp.bfloat16)


def _in_proj(x2d, g, b, w_bf16):
    grid = (SEQ // IN_TM, PROJ_WIDTH // IN_TN)
    return pl.pallas_call(
        _in_proj_kernel,
        name="in_proj",
        grid=grid,
        in_specs=[
            pl.BlockSpec((IN_TM, D_MODEL), lambda i, j: (i, 0)),
            pl.BlockSpec((1, D_MODEL), lambda i, j: (0, 0)),
            pl.BlockSpec((1, D_MODEL), lambda i, j: (0, 0)),
            pl.BlockSpec((D_MODEL, IN_TN), lambda i, j: (0, j)),
        ],
        out_specs=pl.BlockSpec((IN_TM, IN_TN), lambda i, j: (i, j)),
        out_shape=jax.ShapeDtypeStruct((SEQ, PROJ_WIDTH), jnp.bfloat16),
        scratch_shapes=[pltpu.VMEM((IN_TM, D_MODEL), jnp.bfloat16)],
        compiler_params=pltpu.CompilerParams(
            dimension_semantics=("arbitrary", "arbitrary"),
            vmem_limit_bytes=V7X_VMEM_LIMIT_BYTES),
    )(x2d, g, b, w_bf16)


def _mixers_kernel(sink_ref, cb_ref, cc_ref, ch_ref, cz_ref,
                   ccp_ref, chp_ref, ccn_ref, chn_ref,
                   q_ref, kp_ref, kc_ref, kn_ref, vp_ref, vc_ref, vn_ref,
                   az0_ref, az1_ref, cw_ref, y_ref):
    i = pl.program_id(0)
    f32 = jnp.float32

    row = lax.broadcasted_iota(jnp.int32, (BLOCK, CONV_CHUNK), 0)
    has_prev = i > 0
    has_next = i < N_BLOCKS - 1
    for c in range(CONV_WIDTH // CONV_CHUNK):
        cols = slice(c * CONV_CHUNK, (c + 1) * CONV_CHUNK)
        u = cc_ref[:, cols].astype(f32) * ch_ref[:, cols].astype(f32)
        u_prev = jnp.where(has_prev, ccp_ref[HALO_ROWS - 1:HALO_ROWS, cols].astype(f32)
                           * chp_ref[HALO_ROWS - 1:HALO_ROWS, cols].astype(f32), 0.0)
        u_next = jnp.where(has_next, ccn_ref[0:1, cols].astype(f32)
                           * chn_ref[0:1, cols].astype(f32), 0.0)
        um1 = jnp.where(row == 0, u_prev, pltpu.roll(u, 1, axis=0))
        up1 = jnp.where(row == BLOCK - 1, u_next, pltpu.roll(u, BLOCK - 1, axis=0))
        conv = um1 * cw_ref[0:1, cols] + u * cw_ref[1:2, cols] + up1 * cw_ref[2:3, cols]
        y = cb_ref[:, cols].astype(f32) * conv * _silu(cz_ref[:, cols].astype(f32))
        y_ref[:, cols] = y.astype(jnp.bfloat16)

    band = 3 * BLOCK
    qi = lax.broadcasted_iota(jnp.int32, (BLOCK, band), 0)
    kj = lax.broadcasted_iota(jnp.int32, (BLOCK, band), 1) - BLOCK
    dist_i = jnp.abs(qi - kj)
    k_pos = kj + i * BLOCK
    valid = (dist_i <= WINDOW) & (k_pos >= 0) & (k_pos < SEQ)
    neg_dist = -dist_i.astype(f32)
    scale = HEAD_DIM ** -0.5

    for kvh in range(N_KV_HEADS):
        kv_cols = slice(kvh * HEAD_DIM, (kvh + 1) * HEAD_DIM)
        q4 = jnp.concatenate(
            [q_ref[:, (kvh * GQA_GROUP + g) * HEAD_DIM:(kvh * GQA_GROUP + g + 1) * HEAD_DIM]
             for g in range(GQA_GROUP)], axis=0)
        kb = jnp.concatenate([kp_ref[:, kv_cols], kc_ref[:, kv_cols], kn_ref[:, kv_cols]], axis=0)
        vb = jnp.concatenate([vp_ref[:, kv_cols], vc_ref[:, kv_cols], vn_ref[:, kv_cols]], axis=0)
        s4 = lax.dot_general(q4, kb, (((1,), (1,)), ((), ())),
                             preferred_element_type=f32)
        e_parts, l_parts = [], []
        for g in range(GQA_GROUP):
            head = kvh * GQA_GROUP + g
            slope = 2.0 ** (-8.0 * (head + 1) / N_Q_HEADS)
            sink = sink_ref[head]
            s = s4[g * BLOCK:(g + 1) * BLOCK, :] * scale + neg_dist * slope
            s = jnp.where(valid, s, NEG_INF)
            m = jnp.maximum(jnp.max(s, axis=-1, keepdims=True), sink)
            e = jnp.exp(s - m)
            l_parts.append(jnp.sum(e, axis=-1, keepdims=True) + jnp.exp(sink - m))
            e_parts.append(e.astype(jnp.bfloat16))
        o4 = jnp.dot(jnp.concatenate(e_parts, axis=0), vb, preferred_element_type=f32)
        for g in range(GQA_GROUP):
            head = kvh * GQA_GROUP + g
            o = o4[g * BLOCK:(g + 1) * BLOCK, :] / l_parts[g]
            az_ref = az0_ref if head < N_Q_HEADS // 2 else az1_ref
            zc = (head % (N_Q_HEADS // 2)) * HEAD_DIM
            gate = _silu(az_ref[:, zc:zc + HEAD_DIM].astype(f32))
            y_ref[:, CONV_WIDTH + head * HEAD_DIM:CONV_WIDTH + (head + 1) * HEAD_DIM] = (
                (o * gate).astype(jnp.bfloat16))


def _mixers(p, conv_w, sink):
    halo_per_block = BLOCK // HALO_ROWS
    n_halo = SEQ // HALO_ROWS

    def cur(col_block):
        return lambda i: (i, col_block)

    def prev_blk(col_block):
        return lambda i: (jnp.maximum(i - 1, 0), col_block)

    def next_blk(col_block):
        return lambda i: (jnp.minimum(i + 1, N_BLOCKS - 1), col_block)

    def prev_halo(col_block):
        return lambda i: (jnp.maximum(i * halo_per_block - 1, 0), col_block)

    def next_halo(col_block):
        return lambda i: (jnp.minimum((i + 1) * halo_per_block, n_halo - 1), col_block)

    wide = (BLOCK, CONV_WIDTH)
    halo = (HALO_ROWS, CONV_WIDTH)
    kv = (BLOCK, KV_WIDTH)
    half = (BLOCK, ATTN_WIDTH // 2)
    in_specs = [
        pl.BlockSpec(memory_space=pltpu.SMEM),
        pl.BlockSpec(wide, cur(OFF_CB // CONV_WIDTH)),
        pl.BlockSpec(wide, cur(OFF_CC // CONV_WIDTH)),
        pl.BlockSpec(wide, cur(OFF_CH // CONV_WIDTH)),
        pl.BlockSpec(wide, cur(OFF_CZ // CONV_WIDTH)),
        pl.BlockSpec(halo, prev_halo(OFF_CC // CONV_WIDTH)),
        pl.BlockSpec(halo, prev_halo(OFF_CH // CONV_WIDTH)),
        pl.BlockSpec(halo, next_halo(OFF_CC // CONV_WIDTH)),
        pl.BlockSpec(halo, next_halo(OFF_CH // CONV_WIDTH)),
        pl.BlockSpec(wide, cur(OFF_Q // ATTN_WIDTH)),
        pl.BlockSpec(kv, prev_blk(OFF_K // KV_WIDTH)),
        pl.BlockSpec(kv, cur(OFF_K // KV_WIDTH)),
        pl.BlockSpec(kv, next_blk(OFF_K // KV_WIDTH)),
        pl.BlockSpec(kv, prev_blk(OFF_V // KV_WIDTH)),
        pl.BlockSpec(kv, cur(OFF_V // KV_WIDTH)),
        pl.BlockSpec(kv, next_blk(OFF_V // KV_WIDTH)),
        pl.BlockSpec(half, cur(OFF_AZ // (ATTN_WIDTH // 2))),
        pl.BlockSpec(half, cur(OFF_AZ // (ATTN_WIDTH // 2) + 1)),
        pl.BlockSpec((3, CONV_WIDTH), lambda i: (0, 0)),
    ]
    return pl.pallas_call(
        _mixers_kernel,
        name="mixers",
        grid=(N_BLOCKS,),
        in_specs=in_specs,
        out_specs=pl.BlockSpec((BLOCK, D_MODEL), lambda i: (i, 0)),
        out_shape=jax.ShapeDtypeStruct((SEQ, D_MODEL), jnp.bfloat16),
        compiler_params=pltpu.CompilerParams(
            dimension_semantics=("arbitrary",),
            vmem_limit_bytes=V7X_VMEM_LIMIT_BYTES),
    )(sink, *([p] * 17), conv_w)


def _out_proj_kernel(x_ref, g0_ref, b0_ref, y_ref, w_ref, g1_ref, b1_ref, o_ref):
    n = pl.program_id(1)

    @pl.when(n == 0)
    def _():
        g = g0_ref[...]
        b = b0_ref[...]

        def body(r, carry):
            rows = pl.ds(pl.multiple_of(r * LN_CHUNK_ROWS, LN_CHUNK_ROWS), LN_CHUNK_ROWS)
            o_ref[rows, :] = DN_ALPHA * _layer_norm_rows(x_ref[rows, :], g, b)
            return carry

        lax.fori_loop(0, OUT_TM // LN_CHUNK_ROWS, body, 0)

    cols = pl.ds(pl.multiple_of(n * OUT_TN, OUT_TN), OUT_TN)
    o_ref[:, cols] += jnp.dot(y_ref[...], w_ref[...], preferred_element_type=jnp.float32)

    @pl.when(n == pl.num_programs(1) - 1)
    def _():
        g = g1_ref[...]
        b = b1_ref[...]

        def body(r, carry):
            rows = pl.ds(pl.multiple_of(r * LN_CHUNK_ROWS, LN_CHUNK_ROWS), LN_CHUNK_ROWS)
            o_ref[rows, :] = _layer_norm_rows(o_ref[rows, :], g, b)
            return carry

        lax.fori_loop(0, OUT_TM // LN_CHUNK_ROWS, body, 0)


def _out_proj(x2d, g0, b0, ymix, w_bf16, g1, b1):
    grid = (SEQ // OUT_TM, D_MODEL // OUT_TN)
    vec = pl.BlockSpec((1, D_MODEL), lambda i, n: (0, 0))
    return pl.pallas_call(
        _out_proj_kernel,
        name="out_proj",
        grid=grid,
        in_specs=[
            pl.BlockSpec((OUT_TM, D_MODEL), lambda i, n: (i, 0)),
            vec, vec,
            pl.BlockSpec((OUT_TM, D_MODEL), lambda i, n: (i, 0)),
            pl.BlockSpec((D_MODEL, OUT_TN), lambda i, n: (0, n)),
            vec, vec,
        ],
        out_specs=pl.BlockSpec((OUT_TM, D_MODEL), lambda i, n: (i, 0)),
        out_shape=jax.ShapeDtypeStruct((SEQ, D_MODEL), jnp.float32),
        compiler_params=pltpu.CompilerParams(
            dimension_semantics=("arbitrary", "arbitrary"),
            vmem_limit_bytes=V7X_VMEM_LIMIT_BYTES),
    )(x2d, g0, b0, ymix, w_bf16, g1, b1)


def kernel(x, emb_ln_g, emb_ln_b, w_in, conv_w, sink, w_out, ln_g, ln_b):
    batch, seq, d_model = x.shape
    assert (batch, seq, d_model) == (1, SEQ, D_MODEL)
    assert w_in.shape == (1, D_MODEL, PROJ_WIDTH) and w_out.shape == (1, D_MODEL, D_MODEL)
    x2d = x.reshape(SEQ, D_MODEL)
    g0 = emb_ln_g.reshape(1, D_MODEL)
    b0 = emb_ln_b.reshape(1, D_MODEL)
    g1 = ln_g.reshape(1, D_MODEL)
    b1 = ln_b.reshape(1, D_MODEL)
    p = _in_proj(x2d, g0, b0, w_in[0].astype(jnp.bfloat16))
    ymix = _mixers(p, conv_w[0], sink[0])
    out = _out_proj(x2d, g0, b0, ymix, w_out[0].astype(jnp.bfloat16), g1, b1)
    return out.reshape(1, SEQ, D_MODEL)
```

```python
import jax
import jax.numpy as jnp
from jax import lax
from jax.experimental import pallas as pl
from jax.experimental.pallas import tpu as pltpu

D_MODEL = 4096
SEQ = 8192
CONV_WIDTH = D_MODEL // 2
ATTN_WIDTH = D_MODEL - CONV_WIDTH
HEAD_DIM = 128
N_Q_HEADS = ATTN_WIDTH // HEAD_DIM
N_KV_HEADS = N_Q_HEADS // 4
GQA_GROUP = N_Q_HEADS // N_KV_HEADS
KV_WIDTH = N_KV_HEADS * HEAD_DIM
WINDOW = 128
BLOCK = 128
N_BLOCKS = SEQ // BLOCK
DN_ALPHA = 2.0 ** 0.25
LN_EPS = 1e-5
NEG_INF = -1e30
PROJ_WIDTH = 4 * CONV_WIDTH + ATTN_WIDTH + 2 * KV_WIDTH + ATTN_WIDTH

OFF_CB = 0
OFF_CC = OFF_CB + CONV_WIDTH
OFF_CH = OFF_CC + CONV_WIDTH
OFF_CZ = OFF_CH + CONV_WIDTH
OFF_Q = OFF_CZ + CONV_WIDTH
OFF_K = OFF_Q + ATTN_WIDTH
OFF_V = OFF_K + KV_WIDTH
OFF_AZ = OFF_V + KV_WIDTH

V7X_VMEM_LIMIT_BYTES = 58 * 1024 * 1024
SUBLANES = 8

NORM_TM = 256
NORM_UNROLL = 8
IN_TM, IN_TN = 1024, 512
OUT_TM, OUT_TN = 512, 512
OUT_NT = D_MODEL // OUT_TN
OUT_ROW_CHUNK = 64
STAT_LANES = 128
assert OUT_NT <= STAT_LANES
HALO_ROWS = 16
CONV_CHUNK = 256


def _silu(z):
    return z / (1.0 + jnp.exp(-z))


def _entry_norm_kernel(x_ref, g_ref, b_ref, h_ref, mu_ref, rstd_ref):
    n_groups = NORM_TM // SUBLANES

    def group(r):
        return pl.ds(pl.multiple_of(r * SUBLANES, SUBLANES), SUBLANES)

    def mean_body(r, carry):
        rows = group(r)
        mu_ref[rows, :] = jnp.mean(x_ref[rows, :], axis=-1, keepdims=True)
        return carry

    def var_body(r, carry):
        rows = group(r)
        xc = x_ref[rows, :] - mu_ref[rows, :]
        var = jnp.mean(xc * xc, axis=-1, keepdims=True)
        rstd_ref[rows, :] = lax.rsqrt(var + LN_EPS)
        return carry

    def scale_body(r, carry):
        rows = group(r)
        y = (x_ref[rows, :] - mu_ref[rows, :]) * rstd_ref[rows, :] * g_ref[...] + b_ref[...]
        h_ref[rows, :] = y.astype(jnp.bfloat16)
        return carry

    lax.fori_loop(0, n_groups, mean_body, 0, unroll=NORM_UNROLL)
    lax.fori_loop(0, n_groups, var_body, 0, unroll=NORM_UNROLL)
    lax.fori_loop(0, n_groups, scale_body, 0, unroll=NORM_UNROLL)


def _entry_norm(x2d, g, b):
    row_stat = pl.BlockSpec((NORM_TM, 1), lambda i: (i, 0))
    vec = pl.BlockSpec((1, D_MODEL), lambda i: (0, 0))
    return pl.pallas_call(
        _entry_norm_kernel,
        name="entry_norm",
        grid=(SEQ // NORM_TM,),
        in_specs=[pl.BlockSpec((NORM_TM, D_MODEL), lambda i: (i, 0)), vec, vec],
        out_specs=[pl.BlockSpec((NORM_TM, D_MODEL), lambda i: (i, 0)), row_stat, row_stat],
        out_shape=[jax.ShapeDtypeStruct((SEQ, D_MODEL), jnp.bfloat16),
                   jax.ShapeDtypeStruct((SEQ, 1), jnp.float32),
                   jax.ShapeDtypeStruct((SEQ, 1), jnp.float32)],
        compiler_params=pltpu.CompilerParams(
            dimension_semantics=("arbitrary",),
            vmem_limit_bytes=V7X_VMEM_LIMIT_BYTES),
    )(x2d, g, b)


def _in_proj_kernel(h_ref, w_ref, p_ref):
    p_ref[...] = jnp.dot(h_ref[...], w_ref[...].astype(jnp.bfloat16),
                         preferred_element_type=jnp.float32).astype(jnp.bfloat16)


def _in_proj(h, w):
    grid = (SEQ // IN_TM, PROJ_WIDTH // IN_TN)
    return pl.pallas_call(
        _in_proj_kernel,
        name="in_proj",
        grid=grid,
        in_specs=[
            pl.BlockSpec((IN_TM, D_MODEL), lambda i, j: (i, 0)),
            pl.BlockSpec((D_MODEL, IN_TN), lambda i, j: (0, j)),
        ],
        out_specs=pl.BlockSpec((IN_TM, IN_TN), lambda i, j: (i, j)),
        out_shape=jax.ShapeDtypeStruct((SEQ, PROJ_WIDTH), jnp.bfloat16),
        compiler_params=pltpu.CompilerParams(
            dimension_semantics=("arbitrary", "arbitrary"),
            vmem_limit_bytes=V7X_VMEM_LIMIT_BYTES),
    )(h, w)


def _mixers_kernel(sink_ref, cb_ref, cc_ref, ch_ref, cz_ref,
                   ccp_ref, chp_ref, ccn_ref, chn_ref,
                   q_ref, kp_ref, kc_ref, kn_ref, vp_ref, vc_ref, vn_ref,
                   az0_ref, az1_ref, cw_ref, y_ref):
    i = pl.program_id(0)
    f32 = jnp.float32

    row = lax.broadcasted_iota(jnp.int32, (BLOCK, CONV_CHUNK), 0)
    has_prev = i > 0
    has_next = i < N_BLOCKS - 1
    for c in range(CONV_WIDTH // CONV_CHUNK):
        cols = slice(c * CONV_CHUNK, (c + 1) * CONV_CHUNK)
        u = cc_ref[:, cols].astype(f32) * ch_ref[:, cols].astype(f32)
        u_prev = jnp.where(has_prev, ccp_ref[HALO_ROWS - 1:HALO_ROWS, cols].astype(f32)
                           * chp_ref[HALO_ROWS - 1:HALO_ROWS, cols].astype(f32), 0.0)
        u_next = jnp.where(has_next, ccn_ref[0:1, cols].astype(f32)
                           * chn_ref[0:1, cols].astype(f32), 0.0)
        um1 = jnp.where(row == 0, u_prev, pltpu.roll(u, 1, axis=0))
        up1 = jnp.where(row == BLOCK - 1, u_next, pltpu.roll(u, BLOCK - 1, axis=0))
        conv = um1 * cw_ref[0:1, cols] + u * cw_ref[1:2, cols] + up1 * cw_ref[2:3, cols]
        y = cb_ref[:, cols].astype(f32) * conv * _silu(cz_ref[:, cols].astype(f32))
        y_ref[:, cols] = y.astype(jnp.bfloat16)

    band = 3 * BLOCK
    qi = lax.broadcasted_iota(jnp.int32, (BLOCK, band), 0)
    kj = lax.broadcasted_iota(jnp.int32, (BLOCK, band), 1) - BLOCK
    dist_i = jnp.abs(qi - kj)
    k_pos = kj + i * BLOCK
    valid = (dist_i <= WINDOW) & (k_pos >= 0) & (k_pos < SEQ)
    neg_dist = -dist_i.astype(f32)
    scale = HEAD_DIM ** -0.5

    for kvh in range(N_KV_HEADS):
        kv_cols = slice(kvh * HEAD_DIM, (kvh + 1) * HEAD_DIM)
        q4 = jnp.concatenate(
            [q_ref[:, (kvh * GQA_GROUP + g) * HEAD_DIM:(kvh * GQA_GROUP + g + 1) * HEAD_DIM]
             for g in range(GQA_GROUP)], axis=0)
        kb = jnp.concatenate([kp_ref[:, kv_cols], kc_ref[:, kv_cols], kn_ref[:, kv_cols]], axis=0)
        vb = jnp.concatenate([vp_ref[:, kv_cols], vc_ref[:, kv_cols], vn_ref[:, kv_cols]], axis=0)
        s4 = lax.dot_general(q4, kb, (((1,), (1,)), ((), ())),
                             preferred_element_type=f32)
        e_parts, l_parts = [], []
        for g in range(GQA_GROUP):
            head = kvh * GQA_GROUP + g
            slope = 2.0 ** (-8.0 * (head + 1) / N_Q_HEADS)
            sink = sink_ref[head]
            s = s4[g * BLOCK:(g + 1) * BLOCK, :] * scale + neg_dist * slope
            s = jnp.where(valid, s, NEG_INF)
            m = jnp.maximum(jnp.max(s, axis=-1, keepdims=True), sink)
            e = jnp.exp(s - m)
            l_parts.append(jnp.sum(e, axis=-1, keepdims=True) + jnp.exp(sink - m))
            e_parts.append(e.astype(jnp.bfloat16))
        o4 = jnp.dot(jnp.concatenate(e_parts, axis=0), vb, preferred_element_type=f32)
        for g in range(GQA_GROUP):
            head = kvh * GQA_GROUP + g
            o = o4[g * BLOCK:(g + 1) * BLOCK, :] / l_parts[g]
            az_ref = az0_ref if head < N_Q_HEADS // 2 else az1_ref
            zc = (head % (N_Q_HEADS // 2)) * HEAD_DIM
            gate = _silu(az_ref[:, zc:zc + HEAD_DIM].astype(f32))
            y_ref[:, CONV_WIDTH + head * HEAD_DIM:CONV_WIDTH + (head + 1) * HEAD_DIM] = (
                (o * gate).astype(jnp.bfloat16))


def _mixers(p, conv_w, sink):
    halo_per_block = BLOCK // HALO_ROWS
    n_halo = SEQ // HALO_ROWS

    def cur(col_block):
        return lambda i: (i, col_block)

    def prev_blk(col_block):
        return lambda i: (jnp.maximum(i - 1, 0), col_block)

    def next_blk(col_block):
        return lambda i: (jnp.minimum(i + 1, N_BLOCKS - 1), col_block)

    def prev_halo(col_block):
        return lambda i: (jnp.maximum(i * halo_per_block - 1, 0), col_block)

    def next_halo(col_block):
        return lambda i: (jnp.minimum((i + 1) * halo_per_block, n_halo - 1), col_block)

    wide = (BLOCK, CONV_WIDTH)
    halo = (HALO_ROWS, CONV_WIDTH)
    kv = (BLOCK, KV_WIDTH)
    half = (BLOCK, ATTN_WIDTH // 2)
    in_specs = [
        pl.BlockSpec(memory_space=pltpu.SMEM),
        pl.BlockSpec(wide, cur(OFF_CB // CONV_WIDTH)),
        pl.BlockSpec(wide, cur(OFF_CC // CONV_WIDTH)),
        pl.BlockSpec(wide, cur(OFF_CH // CONV_WIDTH)),
        pl.BlockSpec(wide, cur(OFF_CZ // CONV_WIDTH)),
        pl.BlockSpec(halo, prev_halo(OFF_CC // CONV_WIDTH)),
        pl.BlockSpec(halo, prev_halo(OFF_CH // CONV_WIDTH)),
        pl.BlockSpec(halo, next_halo(OFF_CC // CONV_WIDTH)),
        pl.BlockSpec(halo, next_halo(OFF_CH // CONV_WIDTH)),
        pl.BlockSpec(wide, cur(OFF_Q // ATTN_WIDTH)),
        pl.BlockSpec(kv, prev_blk(OFF_K // KV_WIDTH)),
        pl.BlockSpec(kv, cur(OFF_K // KV_WIDTH)),
        pl.BlockSpec(kv, next_blk(OFF_K // KV_WIDTH)),
        pl.BlockSpec(kv, prev_blk(OFF_V // KV_WIDTH)),
        pl.BlockSpec(kv, cur(OFF_V // KV_WIDTH)),
        pl.BlockSpec(kv, next_blk(OFF_V // KV_WIDTH)),
        pl.BlockSpec(half, cur(OFF_AZ // (ATTN_WIDTH // 2))),
        pl.BlockSpec(half, cur(OFF_AZ // (ATTN_WIDTH // 2) + 1)),
        pl.BlockSpec((3, CONV_WIDTH), lambda i: (0, 0)),
    ]
    return pl.pallas_call(
        _mixers_kernel,
        name="mixers",
        grid=(N_BLOCKS,),
        in_specs=in_specs,
        out_specs=pl.BlockSpec((BLOCK, D_MODEL), lambda i: (i, 0)),
        out_shape=jax.ShapeDtypeStruct((SEQ, D_MODEL), jnp.bfloat16),
        compiler_params=pltpu.CompilerParams(
            dimension_semantics=("arbitrary",),
            vmem_limit_bytes=V7X_VMEM_LIMIT_BYTES),
    )(sink, *([p] * 17), conv_w)


def _out_proj_kernel(x_ref, mu_ref, rstd_ref, g0_ref, b0_ref, y_ref, w_ref, g1_ref, b1_ref,
                     o_ref, z_ref, zmean_ref, zm2_ref):
    n = pl.program_id(1)

    @pl.when(n == 0)
    def _():
        zmean_ref[...] = jnp.zeros_like(zmean_ref)
        zm2_ref[...] = jnp.zeros_like(zm2_ref)

    y = jnp.dot(y_ref[...], w_ref[...].astype(jnp.bfloat16), preferred_element_type=jnp.float32)
    g0 = g0_ref[...]
    b0 = b0_ref[...]
    own_lane = lax.broadcasted_iota(jnp.int32, (OUT_ROW_CHUNK, STAT_LANES), 1) == n
    for c in range(OUT_TM // OUT_ROW_CHUNK):
        rows = slice(c * OUT_ROW_CHUNK, (c + 1) * OUT_ROW_CHUNK)
        h = (x_ref[rows, :] - mu_ref[rows, :]) * rstd_ref[rows, :] * g0 + b0
        z = DN_ALPHA * h + y[rows, :]
        z_ref[n, rows, :] = z
        zm = jnp.mean(z, axis=-1, keepdims=True)
        zc = z - zm
        zmean_ref[rows, :] = jnp.where(own_lane, zm, zmean_ref[rows, :])
        zm2_ref[rows, :] = jnp.where(own_lane, jnp.sum(zc * zc, axis=-1, keepdims=True),
                                     zm2_ref[rows, :])

    @pl.when(n == OUT_NT - 1)
    def _():
        used_lane = lax.broadcasted_iota(jnp.int32, (OUT_ROW_CHUNK, STAT_LANES), 1) < OUT_NT

        def body(c, carry):
            rows = pl.ds(pl.multiple_of(c * OUT_ROW_CHUNK, OUT_ROW_CHUNK), OUT_ROW_CHUNK)
            means = zmean_ref[rows, :]
            mean = jnp.sum(means, axis=-1, keepdims=True) * (1.0 / OUT_NT)
            dm = jnp.where(used_lane, means - mean, 0.0)
            m2 = jnp.sum(zm2_ref[rows, :] + OUT_TN * (dm * dm), axis=-1, keepdims=True)
            rstd = lax.rsqrt(m2 * (1.0 / D_MODEL) + LN_EPS)
            for k in range(OUT_NT):
                cols = slice(k * OUT_TN, (k + 1) * OUT_TN)
                o_ref[rows, cols] = (z_ref[k, rows, :] - mean) * rstd * g1_ref[:, cols] + b1_ref[:, cols]
            return carry

        lax.fori_loop(0, OUT_TM // OUT_ROW_CHUNK, body, 0)


def _out_proj(x2d, mu, rstd, g0, b0, ymix, w, g1, b1):
    grid = (SEQ // OUT_TM, OUT_NT)
    row_stat = pl.BlockSpec((OUT_TM, 1), lambda i, n: (i, 0))
    col_vec = pl.BlockSpec((1, OUT_TN), lambda i, n: (0, n))
    full_vec = pl.BlockSpec((1, D_MODEL), lambda i, n: (0, 0))
    return pl.pallas_call(
        _out_proj_kernel,
        name="out_proj",
        grid=grid,
        in_specs=[
            pl.BlockSpec((OUT_TM, OUT_TN), lambda i, n: (i, n)),
            row_stat, row_stat,
            col_vec, col_vec,
            pl.BlockSpec((OUT_TM, D_MODEL), lambda i, n: (i, 0)),
            pl.BlockSpec((D_MODEL, OUT_TN), lambda i, n: (0, n)),
            full_vec, full_vec,
        ],
        out_specs=pl.BlockSpec((OUT_TM, D_MODEL), lambda i, n: (i, 0)),
        out_shape=jax.ShapeDtypeStruct((SEQ, D_MODEL), jnp.float32),
        scratch_shapes=[
            pltpu.VMEM((OUT_NT, OUT_TM, OUT_TN), jnp.float32),
            pltpu.VMEM((OUT_TM, STAT_LANES), jnp.float32),
            pltpu.VMEM((OUT_TM, STAT_LANES), jnp.float32),
        ],
        compiler_params=pltpu.CompilerParams(
            dimension_semantics=("arbitrary", "arbitrary"),
            vmem_limit_bytes=V7X_VMEM_LIMIT_BYTES),
    )(x2d, mu, rstd, g0, b0, ymix, w, g1, b1)


def kernel(x, emb_ln_g, emb_ln_b, w_in, conv_w, sink, w_out, ln_g, ln_b):
    batch, seq, d_model = x.shape
    assert (batch, seq, d_model) == (1, SEQ, D_MODEL)
    assert w_in.shape == (1, D_MODEL, PROJ_WIDTH) and w_out.shape == (1, D_MODEL, D_MODEL)
    x2d = x.reshape(SEQ, D_MODEL)
    g0 = emb_ln_g.reshape(1, D_MODEL)
    b0 = emb_ln_b.reshape(1, D_MODEL)
    g1 = ln_g.reshape(1, D_MODEL)
    b1 = ln_b.reshape(1, D_MODEL)
    h, mu, rstd = _entry_norm(x2d, g0, b0)
    p = _in_proj(h, w_in[0])
    ymix = _mixers(p, conv_w[0], sink[0])
    out = _out_proj(x2d, mu, rstd, g0, b0, ymix, w_out[0], g1, b1)
    return out.reshape(1, SEQ, D_MODEL)
```

```python
import jax
import jax.numpy as jnp
from jax import lax
from jax.experimental import pallas as pl
from jax.experimental.pallas import tpu as pltpu

D_MODEL = 4096
SEQ = 8192
CONV_WIDTH = D_MODEL // 2
ATTN_WIDTH = D_MODEL - CONV_WIDTH
HEAD_DIM = 128
N_Q_HEADS = ATTN_WIDTH // HEAD_DIM
N_KV_HEADS = N_Q_HEADS // 4
GQA_GROUP = N_Q_HEADS // N_KV_HEADS
KV_WIDTH = N_KV_HEADS * HEAD_DIM
WINDOW = 128
BLOCK = 128
N_BLOCKS = SEQ // BLOCK
DN_ALPHA = 2.0 ** 0.25
LN_EPS = 1e-5
NEG_INF = -1e30
PROJ_WIDTH = 4 * CONV_WIDTH + ATTN_WIDTH + 2 * KV_WIDTH + ATTN_WIDTH

OFF_CB = 0
OFF_CC = OFF_CB + CONV_WIDTH
OFF_CH = OFF_CC + CONV_WIDTH
OFF_CZ = OFF_CH + CONV_WIDTH
OFF_Q = OFF_CZ + CONV_WIDTH
OFF_K = OFF_Q + ATTN_WIDTH
OFF_V = OFF_K + KV_WIDTH
OFF_AZ = OFF_V + KV_WIDTH

V7X_VMEM_LIMIT_BYTES = 58 * 1024 * 1024
SUBLANES = 8

NORM_TM = 256
NORM_UNROLL = 8
BF16_ROWS = 16
IN_TM, IN_TN = 1024, 512
IN_LN_STEPS = 16
IN_LN_ROWS = IN_TM // IN_LN_STEPS
WOUT_CAST_ROWS = 32
OUT_TM, OUT_TN = 512, 512
OUT_NT = D_MODEL // OUT_TN
OUT_ROW_CHUNK = 64
STAT_LANES = 128
assert OUT_NT <= STAT_LANES
HALO_ROWS = 16
CONV_CHUNK = 256


def _silu(z):
    return z / (1.0 + jnp.exp(-z))


def _entry_norm_kernel(x_ref, g_ref, b_ref, h_ref, mu_ref, rstd_ref):
    n_groups = NORM_TM // SUBLANES

    def group(r):
        return pl.ds(pl.multiple_of(r * SUBLANES, SUBLANES), SUBLANES)

    def mean_body(r, carry):
        rows = group(r)
        mu_ref[rows, :] = jnp.mean(x_ref[rows, :], axis=-1, keepdims=True)
        return carry

    def var_body(r, carry):
        rows = group(r)
        xc = x_ref[rows, :] - mu_ref[rows, :]
        var = jnp.mean(xc * xc, axis=-1, keepdims=True)
        rstd_ref[rows, :] = lax.rsqrt(var + LN_EPS)
        return carry

    def scale_body(r, carry):
        rows = group(r)
        y = (x_ref[rows, :] - mu_ref[rows, :]) * rstd_ref[rows, :] * g_ref[...] + b_ref[...]
        h_ref[rows, :] = y.astype(jnp.bfloat16)
        return carry

    lax.fori_loop(0, n_groups, mean_body, 0, unroll=NORM_UNROLL)
    lax.fori_loop(0, n_groups, var_body, 0, unroll=NORM_UNROLL)
    lax.fori_loop(0, n_groups, scale_body, 0, unroll=NORM_UNROLL)


def _entry_norm_first_tile(x2d, g, b):
    row_stat = pl.BlockSpec((NORM_TM, 1), lambda i: (i, 0))
    vec = pl.BlockSpec((1, D_MODEL), lambda i: (0, 0))
    return pl.pallas_call(
        _entry_norm_kernel,
        name="entry_norm",
        grid=(IN_TM // NORM_TM,),
        in_specs=[pl.BlockSpec((NORM_TM, D_MODEL), lambda i: (i, 0)), vec, vec],
        out_specs=[pl.BlockSpec((NORM_TM, D_MODEL), lambda i: (i, 0)), row_stat, row_stat],
        out_shape=[jax.ShapeDtypeStruct((IN_TM, D_MODEL), jnp.bfloat16),
                   jax.ShapeDtypeStruct((IN_TM, 1), jnp.float32),
                   jax.ShapeDtypeStruct((IN_TM, 1), jnp.float32)],
        compiler_params=pltpu.CompilerParams(
            dimension_semantics=("arbitrary",),
            vmem_limit_bytes=V7X_VMEM_LIMIT_BYTES),
    )(x2d, g, b)


def _in_proj_kernel(h0_ref, x_ref, g_ref, b_ref, w_ref, wo_ref,
                    p_ref, mu_ref, rstd_ref, wob_ref, h_even_ref, h_odd_ref):
    i = pl.program_id(0)
    j = pl.program_id(1)

    @pl.when((i == 0) & (j == 0))
    def _():
        h_even_ref[...] = h0_ref[...]

    def step(cur_ref, nxt_ref):
        p_ref[...] = jnp.dot(cur_ref[...], w_ref[...].astype(jnp.bfloat16),
                             preferred_element_type=jnp.float32).astype(jnp.bfloat16)
        chunk = jnp.minimum(j, IN_LN_STEPS - 1)
        base = pl.multiple_of(chunk * IN_LN_ROWS, IN_LN_ROWS)
        g = g_ref[...]
        b = b_ref[...]
        for r in range(IN_LN_ROWS // BF16_ROWS):
            rows = slice(r * BF16_ROWS, (r + 1) * BF16_ROWS)
            x = x_ref[rows, :]
            mu = jnp.mean(x, axis=-1, keepdims=True)
            xc = x - mu
            rstd = lax.rsqrt(jnp.mean(xc * xc, axis=-1, keepdims=True) + LN_EPS)
            mu_ref[rows, :] = mu
            rstd_ref[rows, :] = rstd
            nxt_ref[pl.ds(base + r * BF16_ROWS, BF16_ROWS), :] = (
                (xc * rstd * g + b).astype(jnp.bfloat16))
        wob_ref[...] = wo_ref[...].astype(jnp.bfloat16)

    @pl.when(i % 2 == 0)
    def _():
        step(h_even_ref, h_odd_ref)

    @pl.when(i % 2 == 1)
    def _():
        step(h_odd_ref, h_even_ref)


def _in_proj(h0, x2d, g, b, w_in, w_out):
    n_i, n_j = SEQ // IN_TM, PROJ_WIDTH // IN_TN
    n_chunks = SEQ // IN_LN_ROWS
    n_slabs = D_MODEL // WOUT_CAST_ROWS
    assert n_i * n_j >= n_slabs and n_j >= IN_LN_STEPS

    def next_tile_chunk(i, j):
        return (jnp.minimum((i + 1) * IN_LN_STEPS + jnp.minimum(j, IN_LN_STEPS - 1), n_chunks - 1), 0)

    def slab(i, j):
        return (jnp.minimum(i * n_j + j, n_slabs - 1), 0)

    vec = pl.BlockSpec((1, D_MODEL), lambda i, j: (0, 0))
    return pl.pallas_call(
        _in_proj_kernel,
        name="in_proj",
        grid=(n_i, n_j),
        in_specs=[
            pl.BlockSpec((IN_TM, D_MODEL), lambda i, j: (0, 0), pipeline_mode=pl.Buffered(1)),
            pl.BlockSpec((IN_LN_ROWS, D_MODEL), next_tile_chunk),
            vec, vec,
            pl.BlockSpec((D_MODEL, IN_TN), lambda i, j: (0, j)),
            pl.BlockSpec((WOUT_CAST_ROWS, D_MODEL), slab),
        ],
        out_specs=[
            pl.BlockSpec((IN_TM, IN_TN), lambda i, j: (i, j)),
            pl.BlockSpec((IN_LN_ROWS, 1), next_tile_chunk),
            pl.BlockSpec((IN_LN_ROWS, 1), next_tile_chunk),
            pl.BlockSpec((WOUT_CAST_ROWS, D_MODEL), slab),
        ],
        out_shape=[
            jax.ShapeDtypeStruct((SEQ, PROJ_WIDTH), jnp.bfloat16),
            jax.ShapeDtypeStruct((SEQ, 1), jnp.float32),
            jax.ShapeDtypeStruct((SEQ, 1), jnp.float32),
            jax.ShapeDtypeStruct((D_MODEL, D_MODEL), jnp.bfloat16),
        ],
        scratch_shapes=[pltpu.VMEM((IN_TM, D_MODEL), jnp.bfloat16),
                        pltpu.VMEM((IN_TM, D_MODEL), jnp.bfloat16)],
        compiler_params=pltpu.CompilerParams(
            dimension_semantics=("arbitrary", "arbitrary"),
            vmem_limit_bytes=V7X_VMEM_LIMIT_BYTES),
    )(h0, x2d, g, b, w_in, w_out)


def _mixers_kernel(sink_ref, cb_ref, cc_ref, ch_ref, cz_ref,
                   ccp_ref, chp_ref, ccn_ref, chn_ref,
                   q_ref, kp_ref, kc_ref, kn_ref, vp_ref, vc_ref, vn_ref,
                   az0_ref, az1_ref, cw_ref, y_ref):
    i = pl.program_id(0)
    f32 = jnp.float32

    row = lax.broadcasted_iota(jnp.int32, (BLOCK, CONV_CHUNK), 0)
    has_prev = i > 0
    has_next = i < N_BLOCKS - 1
    for c in range(CONV_WIDTH // CONV_CHUNK):
        cols = slice(c * CONV_CHUNK, (c + 1) * CONV_CHUNK)
        u = cc_ref[:, cols].astype(f32) * ch_ref[:, cols].astype(f32)
        u_prev = jnp.where(has_prev, ccp_ref[HALO_ROWS - 1:HALO_ROWS, cols].astype(f32)
                           * chp_ref[HALO_ROWS - 1:HALO_ROWS, cols].astype(f32), 0.0)
        u_next = jnp.where(has_next, ccn_ref[0:1, cols].astype(f32)
                           * chn_ref[0:1, cols].astype(f32), 0.0)
        um1 = jnp.where(row == 0, u_prev, pltpu.roll(u, 1, axis=0))
        up1 = jnp.where(row == BLOCK - 1, u_next, pltpu.roll(u, BLOCK - 1, axis=0))
        conv = um1 * cw_ref[0:1, cols] + u * cw_ref[1:2, cols] + up1 * cw_ref[2:3, cols]
        y = cb_ref[:, cols].astype(f32) * conv * _silu(cz_ref[:, cols].astype(f32))
        y_ref[:, cols] = y.astype(jnp.bfloat16)

    band = 3 * BLOCK
    qi = lax.broadcasted_iota(jnp.int32, (BLOCK, band), 0)
    kj = lax.broadcasted_iota(jnp.int32, (BLOCK, band), 1) - BLOCK
    dist_i = jnp.abs(qi - kj)
    k_pos = kj + i * BLOCK
    valid = (dist_i <= WINDOW) & (k_pos >= 0) & (k_pos < SEQ)
    neg_dist = -dist_i.astype(f32)
    scale = HEAD_DIM ** -0.5

    for kvh in range(N_KV_HEADS):
        kv_cols = slice(kvh * HEAD_DIM, (kvh + 1) * HEAD_DIM)
        q4 = jnp.concatenate(
            [q_ref[:, (kvh * GQA_GROUP + g) * HEAD_DIM:(kvh * GQA_GROUP + g + 1) * HEAD_DIM]
             for g in range(GQA_GROUP)], axis=0)
        kb = jnp.concatenate([kp_ref[:, kv_cols], kc_ref[:, kv_cols], kn_ref[:, kv_cols]], axis=0)
        vb = jnp.concatenate([vp_ref[:, kv_cols], vc_ref[:, kv_cols], vn_ref[:, kv_cols]], axis=0)
        s4 = lax.dot_general(q4, kb, (((1,), (1,)), ((), ())),
                             preferred_element_type=f32)
        e_parts, l_parts = [], []
        for g in range(GQA_GROUP):
            head = kvh * GQA_GROUP + g
            slope = 2.0 ** (-8.0 * (head + 1) / N_Q_HEADS)
            sink = sink_ref[head]
            s = s4[g * BLOCK:(g + 1) * BLOCK, :] * scale + neg_dist * slope
            s = jnp.where(valid, s, NEG_INF)
            m = jnp.maximum(jnp.max(s, axis=-1, keepdims=True), sink)
            e = jnp.exp(s - m)
            l_parts.append(jnp.sum(e, axis=-1, keepdims=True) + jnp.exp(sink - m))
            e_parts.append(e.astype(jnp.bfloat16))
        o4 = jnp.dot(jnp.concatenate(e_parts, axis=0), vb, preferred_element_type=f32)
        for g in range(GQA_GROUP):
            head = kvh * GQA_GROUP + g
            o = o4[g * BLOCK:(g + 1) * BLOCK, :] / l_parts[g]
            az_ref = az0_ref if head < N_Q_HEADS // 2 else az1_ref
            zc = (head % (N_Q_HEADS // 2)) * HEAD_DIM
            gate = _silu(az_ref[:, zc:zc + HEAD_DIM].astype(f32))
            y_ref[:, CONV_WIDTH + head * HEAD_DIM:CONV_WIDTH + (head + 1) * HEAD_DIM] = (
                (o * gate).astype(jnp.bfloat16))


def _mixers(p, conv_w, sink):
    halo_per_block = BLOCK // HALO_ROWS
    n_halo = SEQ // HALO_ROWS

    def cur(col_block):
        return lambda i: (i, col_block)

    def prev_blk(col_block):
        return lambda i: (jnp.maximum(i - 1, 0), col_block)

    def next_blk(col_block):
        return lambda i: (jnp.minimum(i + 1, N_BLOCKS - 1), col_block)

    def prev_halo(col_block):
        return lambda i: (jnp.maximum(i * halo_per_block - 1, 0), col_block)

    def next_halo(col_block):
        return lambda i: (jnp.minimum((i + 1) * halo_per_block, n_halo - 1), col_block)

    wide = (BLOCK, CONV_WIDTH)
    halo = (HALO_ROWS, CONV_WIDTH)
    kv = (BLOCK, KV_WIDTH)
    half = (BLOCK, ATTN_WIDTH // 2)
    in_specs = [
        pl.BlockSpec(memory_space=pltpu.SMEM),
        pl.BlockSpec(wide, cur(OFF_CB // CONV_WIDTH)),
        pl.BlockSpec(wide, cur(OFF_CC // CONV_WIDTH)),
        pl.BlockSpec(wide, cur(OFF_CH // CONV_WIDTH)),
        pl.BlockSpec(wide, cur(OFF_CZ // CONV_WIDTH)),
        pl.BlockSpec(halo, prev_halo(OFF_CC // CONV_WIDTH)),
        pl.BlockSpec(halo, prev_halo(OFF_CH // CONV_WIDTH)),
        pl.BlockSpec(halo, next_halo(OFF_CC // CONV_WIDTH)),
        pl.BlockSpec(halo, next_halo(OFF_CH // CONV_WIDTH)),
        pl.BlockSpec(wide, cur(OFF_Q // ATTN_WIDTH)),
        pl.BlockSpec(kv, prev_blk(OFF_K // KV_WIDTH)),
        pl.BlockSpec(kv, cur(OFF_K // KV_WIDTH)),
        pl.BlockSpec(kv, next_blk(OFF_K // KV_WIDTH)),
        pl.BlockSpec(kv, prev_blk(OFF_V // KV_WIDTH)),
        pl.BlockSpec(kv, cur(OFF_V // KV_WIDTH)),
        pl.BlockSpec(kv, next_blk(OFF_V // KV_WIDTH)),
        pl.BlockSpec(half, cur(OFF_AZ // (ATTN_WIDTH // 2))),
        pl.BlockSpec(half, cur(OFF_AZ // (ATTN_WIDTH // 2) + 1)),
        pl.BlockSpec((3, CONV_WIDTH), lambda i: (0, 0)),
    ]
    return pl.pallas_call(
        _mixers_kernel,
        name="mixers",
        grid=(N_BLOCKS,),
        in_specs=in_specs,
        out_specs=pl.BlockSpec((BLOCK, D_MODEL), lambda i: (i, 0)),
        out_shape=jax.ShapeDtypeStruct((SEQ, D_MODEL), jnp.bfloat16),
        compiler_params=pltpu.CompilerParams(
            dimension_semantics=("arbitrary",),
            vmem_limit_bytes=V7X_VMEM_LIMIT_BYTES),
    )(sink, *([p] * 17), conv_w)


def _out_proj_kernel(x_ref, mu_ref, rstd_ref, g0_ref, b0_ref, y_ref, w_ref, g1_ref, b1_ref,
                     o_ref, z_ref, zmean_ref, zm2_ref):
    n = pl.program_id(1)

    @pl.when(n == 0)
    def _():
        zmean_ref[...] = jnp.zeros_like(zmean_ref)
        zm2_ref[...] = jnp.zeros_like(zm2_ref)

    y = jnp.dot(y_ref[...], w_ref[...], preferred_element_type=jnp.float32)
    g0 = g0_ref[...]
    b0 = b0_ref[...]
    own_lane = lax.broadcasted_iota(jnp.int32, (OUT_ROW_CHUNK, STAT_LANES), 1) == n
    for c in range(OUT_TM // OUT_ROW_CHUNK):
        rows = slice(c * OUT_ROW_CHUNK, (c + 1) * OUT_ROW_CHUNK)
        h = (x_ref[rows, :] - mu_ref[rows, :]) * rstd_ref[rows, :] * g0 + b0
        z = DN_ALPHA * h + y[rows, :]
        z_ref[n, rows, :] = z
        zm = jnp.mean(z, axis=-1, keepdims=True)
        zc = z - zm
        zmean_ref[rows, :] = jnp.where(own_lane, zm, zmean_ref[rows, :])
        zm2_ref[rows, :] = jnp.where(own_lane, jnp.sum(zc * zc, axis=-1, keepdims=True),
                                     zm2_ref[rows, :])

    @pl.when(n == OUT_NT - 1)
    def _():
        used_lane = lax.broadcasted_iota(jnp.int32, (OUT_ROW_CHUNK, STAT_LANES), 1) < OUT_NT

        def body(c, carry):
            rows = pl.ds(pl.multiple_of(c * OUT_ROW_CHUNK, OUT_ROW_CHUNK), OUT_ROW_CHUNK)
            means = zmean_ref[rows, :]
            mean = jnp.sum(means, axis=-1, keepdims=True) * (1.0 / OUT_NT)
            dm = jnp.where(used_lane, means - mean, 0.0)
            m2 = jnp.sum(zm2_ref[rows, :] + OUT_TN * (dm * dm), axis=-1, keepdims=True)
            rstd = lax.rsqrt(m2 * (1.0 / D_MODEL) + LN_EPS)
            for k in range(OUT_NT):
                cols = slice(k * OUT_TN, (k + 1) * OUT_TN)
                o_ref[rows, cols] = (z_ref[k, rows, :] - mean) * rstd * g1_ref[:, cols] + b1_ref[:, cols]
            return carry

        lax.fori_loop(0, OUT_TM // OUT_ROW_CHUNK, body, 0)


def _out_proj(x2d, mu, rstd, g0, b0, ymix, w, g1, b1):
    grid = (SEQ // OUT_TM, OUT_NT)
    row_stat = pl.BlockSpec((OUT_TM, 1), lambda i, n: (i, 0))
    col_vec = pl.BlockSpec((1, OUT_TN), lambda i, n: (0, n))
    full_vec = pl.BlockSpec((1, D_MODEL), lambda i, n: (0, 0))
    return pl.pallas_call(
        _out_proj_kernel,
        name="out_proj",
        grid=grid,
        in_specs=[
            pl.BlockSpec((OUT_TM, OUT_TN), lambda i, n: (i, n)),
            row_stat, row_stat,
            col_vec, col_vec,
            pl.BlockSpec((OUT_TM, D_MODEL), lambda i, n: (i, 0)),
            pl.BlockSpec((D_MODEL, OUT_TN), lambda i, n: (0, n)),
            full_vec, full_vec,
        ],
        out_specs=pl.BlockSpec((OUT_TM, D_MODEL), lambda i, n: (i, 0)),
        out_shape=jax.ShapeDtypeStruct((SEQ, D_MODEL), jnp.float32),
        scratch_shapes=[
            pltpu.VMEM((OUT_NT, OUT_TM, OUT_TN), jnp.float32),
            pltpu.VMEM((OUT_TM, STAT_LANES), jnp.float32),
            pltpu.VMEM((OUT_TM, STAT_LANES), jnp.float32),
        ],
        compiler_params=pltpu.CompilerParams(
            dimension_semantics=("arbitrary", "arbitrary"),
            vmem_limit_bytes=V7X_VMEM_LIMIT_BYTES),
    )(x2d, mu, rstd, g0, b0, ymix, w, g1, b1)


def kernel(x, emb_ln_g, emb_ln_b, w_in, conv_w, sink, w_out, ln_g, ln_b):
    batch, seq, d_model = x.shape
    assert (batch, seq, d_model) == (1, SEQ, D_MODEL)
    assert w_in.shape == (1, D_MODEL, PROJ_WIDTH) and w_out.shape == (1, D_MODEL, D_MODEL)
    x2d = x.reshape(SEQ, D_MODEL)
    g0 = emb_ln_g.reshape(1, D_MODEL)
    b0 = emb_ln_b.reshape(1, D_MODEL)
    g1 = ln_g.reshape(1, D_MODEL)
    b1 = ln_b.reshape(1, D_MODEL)
    h0, mu0, rstd0 = _entry_norm_first_tile(x2d, g0, b0)
    p, mu_rest, rstd_rest, w_out_bf16 = _in_proj(h0, x2d, g0, b0, w_in[0], w_out[0])
    mu = jnp.concatenate([mu0, mu_rest[IN_TM:]], axis=0)
    rstd = jnp.concatenate([rstd0, rstd_rest[IN_TM:]], axis=0)
    ymix = _mixers(p, conv_w[0], sink[0])
    out = _out_proj(x2d, mu, rstd, g0, b0, ymix, w_out_bf16, g1, b1)
    return out.reshape(1, SEQ, D_MODEL)
```

```python
import jax
import jax.numpy as jnp
from jax import lax
from jax.experimental import pallas as pl
from jax.experimental.pallas import tpu as pltpu

D_MODEL = 4096
SEQ = 8192
CONV_WIDTH = D_MODEL // 2
ATTN_WIDTH = D_MODEL - CONV_WIDTH
HEAD_DIM = 128
N_Q_HEADS = ATTN_WIDTH // HEAD_DIM
N_KV_HEADS = N_Q_HEADS // 4
GQA_GROUP = N_Q_HEADS // N_KV_HEADS
KV_WIDTH = N_KV_HEADS * HEAD_DIM
WINDOW = 128
BLOCK = 128
N_BLOCKS = SEQ // BLOCK
DN_ALPHA = 2.0 ** 0.25
LN_EPS = 1e-5
NEG_INF = -1e30
PROJ_WIDTH = 4 * CONV_WIDTH + ATTN_WIDTH + 2 * KV_WIDTH + ATTN_WIDTH

OFF_CB = 0
OFF_CC = OFF_CB + CONV_WIDTH
OFF_CH = OFF_CC + CONV_WIDTH
OFF_CZ = OFF_CH + CONV_WIDTH
OFF_Q = OFF_CZ + CONV_WIDTH
OFF_K = OFF_Q + ATTN_WIDTH
OFF_V = OFF_K + KV_WIDTH
OFF_AZ = OFF_V + KV_WIDTH

V7X_VMEM_LIMIT_BYTES = 58 * 1024 * 1024
SUBLANES = 8

NORM_TM = 256
NORM_UNROLL = 8
BF16_ROWS = 16
IN_TM, IN_TN = 1024, 512
IN_LN_STEPS = 16
IN_LN_ROWS = IN_TM // IN_LN_STEPS
WOUT_CAST_ROWS = 32
OUT_TM, OUT_TN = 512, 512
OUT_NT = D_MODEL // OUT_TN
OUT_ROW_CHUNK = 64
OUT_N_PIECES = 8
OUT_K_PIECE = D_MODEL // OUT_N_PIECES
STAT_LANES = 128
assert OUT_NT <= STAT_LANES
HALO_ROWS = 16
CONV_CHUNK = 128
HALF_WIDTH = CONV_WIDTH // 2
assert HALF_WIDTH == ATTN_WIDTH // 2
assert HALF_WIDTH // CONV_CHUNK == OUT_N_PIECES == N_Q_HEADS // 2


def _silu(z):
    return z / (1.0 + jnp.exp(-z))


def _entry_norm_kernel(x_ref, g_ref, b_ref, h_ref, mu_ref, rstd_ref):
    n_groups = NORM_TM // SUBLANES

    def group(r):
        return pl.ds(pl.multiple_of(r * SUBLANES, SUBLANES), SUBLANES)

    def mean_body(r, carry):
        rows = group(r)
        mu_ref[rows, :] = jnp.mean(x_ref[rows, :], axis=-1, keepdims=True)
        return carry

    def var_body(r, carry):
        rows = group(r)
        xc = x_ref[rows, :] - mu_ref[rows, :]
        var = jnp.mean(xc * xc, axis=-1, keepdims=True)
        rstd_ref[rows, :] = lax.rsqrt(var + LN_EPS)
        return carry

    def scale_body(r, carry):
        rows = group(r)
        y = (x_ref[rows, :] - mu_ref[rows, :]) * rstd_ref[rows, :] * g_ref[...] + b_ref[...]
        h_ref[rows, :] = y.astype(jnp.bfloat16)
        return carry

    lax.fori_loop(0, n_groups, mean_body, 0, unroll=NORM_UNROLL)
    lax.fori_loop(0, n_groups, var_body, 0, unroll=NORM_UNROLL)
    lax.fori_loop(0, n_groups, scale_body, 0, unroll=NORM_UNROLL)


def _entry_norm_first_tile(x2d, g, b):
    row_stat = pl.BlockSpec((NORM_TM, 1), lambda i: (i, 0))
    vec = pl.BlockSpec((1, D_MODEL), lambda i: (0, 0))
    return pl.pallas_call(
        _entry_norm_kernel,
        name="entry_norm",
        grid=(IN_TM // NORM_TM,),
        in_specs=[pl.BlockSpec((NORM_TM, D_MODEL), lambda i: (i, 0)), vec, vec],
        out_specs=[pl.BlockSpec((NORM_TM, D_MODEL), lambda i: (i, 0)), row_stat, row_stat],
        out_shape=[jax.ShapeDtypeStruct((IN_TM, D_MODEL), jnp.bfloat16),
                   jax.ShapeDtypeStruct((IN_TM, 1), jnp.float32),
                   jax.ShapeDtypeStruct((IN_TM, 1), jnp.float32)],
        compiler_params=pltpu.CompilerParams(
            dimension_semantics=("arbitrary",),
            vmem_limit_bytes=V7X_VMEM_LIMIT_BYTES),
    )(x2d, g, b)


def _in_proj_kernel(h0_ref, x_ref, g_ref, b_ref, w_ref, wo_ref,
                    p_ref, mu_ref, rstd_ref, wob_ref, h_even_ref, h_odd_ref):
    i = pl.program_id(0)
    j = pl.program_id(1)

    @pl.when((i == 0) & (j == 0))
    def _():
        h_even_ref[...] = h0_ref[...]

    def step(cur_ref, nxt_ref):
        p_ref[...] = jnp.dot(cur_ref[...], w_ref[...].astype(jnp.bfloat16),
                             preferred_element_type=jnp.float32).astype(jnp.bfloat16)
        chunk = jnp.minimum(j, IN_LN_STEPS - 1)
        base = pl.multiple_of(chunk * IN_LN_ROWS, IN_LN_ROWS)
        g = g_ref[...]
        b = b_ref[...]
        for r in range(IN_LN_ROWS // BF16_ROWS):
            rows = slice(r * BF16_ROWS, (r + 1) * BF16_ROWS)
            x = x_ref[rows, :]
            mu = jnp.mean(x, axis=-1, keepdims=True)
            xc = x - mu
            rstd = lax.rsqrt(jnp.mean(xc * xc, axis=-1, keepdims=True) + LN_EPS)
            mu_ref[rows, :] = mu
            rstd_ref[rows, :] = rstd
            nxt_ref[pl.ds(base + r * BF16_ROWS, BF16_ROWS), :] = (
                (xc * rstd * g + b).astype(jnp.bfloat16))
        wob_ref[...] = wo_ref[...].astype(jnp.bfloat16)

    @pl.when(i % 2 == 0)
    def _():
        step(h_even_ref, h_odd_ref)

    @pl.when(i % 2 == 1)
    def _():
        step(h_odd_ref, h_even_ref)


def _in_proj(h0, x2d, g, b, w_in, w_out):
    n_i, n_j = SEQ // IN_TM, PROJ_WIDTH // IN_TN
    n_chunks = SEQ // IN_LN_ROWS
    n_slabs = D_MODEL // WOUT_CAST_ROWS
    assert n_i * n_j >= n_slabs and n_j >= IN_LN_STEPS

    def next_tile_chunk(i, j):
        return (jnp.minimum((i + 1) * IN_LN_STEPS + jnp.minimum(j, IN_LN_STEPS - 1), n_chunks - 1), 0)

    def slab(i, j):
        return (jnp.minimum(i * n_j + j, n_slabs - 1), 0)

    vec = pl.BlockSpec((1, D_MODEL), lambda i, j: (0, 0))
    return pl.pallas_call(
        _in_proj_kernel,
        name="in_proj",
        grid=(n_i, n_j),
        in_specs=[
            pl.BlockSpec((IN_TM, D_MODEL), lambda i, j: (0, 0), pipeline_mode=pl.Buffered(1)),
            pl.BlockSpec((IN_LN_ROWS, D_MODEL), next_tile_chunk),
            vec, vec,
            pl.BlockSpec((D_MODEL, IN_TN), lambda i, j: (0, j)),
            pl.BlockSpec((WOUT_CAST_ROWS, D_MODEL), slab),
        ],
        out_specs=[
            pl.BlockSpec((IN_TM, IN_TN), lambda i, j: (i, j)),
            pl.BlockSpec((IN_LN_ROWS, 1), next_tile_chunk),
            pl.BlockSpec((IN_LN_ROWS, 1), next_tile_chunk),
            pl.BlockSpec((WOUT_CAST_ROWS, D_MODEL), slab),
        ],
        out_shape=[
            jax.ShapeDtypeStruct((SEQ, PROJ_WIDTH), jnp.bfloat16),
            jax.ShapeDtypeStruct((SEQ, 1), jnp.float32),
            jax.ShapeDtypeStruct((SEQ, 1), jnp.float32),
            jax.ShapeDtypeStruct((D_MODEL, D_MODEL), jnp.bfloat16),
        ],
        scratch_shapes=[pltpu.VMEM((IN_TM, D_MODEL), jnp.bfloat16),
                        pltpu.VMEM((IN_TM, D_MODEL), jnp.bfloat16)],
        compiler_params=pltpu.CompilerParams(
            dimension_semantics=("arbitrary", "arbitrary"),
            vmem_limit_bytes=V7X_VMEM_LIMIT_BYTES),
    )(h0, x2d, g, b, w_in, w_out)


class _MixerHalfBlock:
    N_CONV_CHUNKS = HALF_WIDTH // CONV_CHUNK
    N_KV_GROUPS = N_KV_HEADS // 2

    def __init__(self, blk, half, sink_ref, cb_ref, cc_ref, ch_ref, cz_ref,
                 ccp_ref, chp_ref, ccn_ref, chn_ref,
                 q_ref, kp_ref, kc_ref, kn_ref, vp_ref, vc_ref, vn_ref,
                 az_ref, cw_ref, store):
        self.blk, self.half, self.sink_ref, self.store = blk, half, sink_ref, store
        self.cb_ref, self.cc_ref, self.ch_ref, self.cz_ref = cb_ref, cc_ref, ch_ref, cz_ref
        self.ccp_ref, self.chp_ref, self.ccn_ref, self.chn_ref = ccp_ref, chp_ref, ccn_ref, chn_ref
        self.q_ref, self.az_ref, self.cw_ref = q_ref, az_ref, cw_ref
        self.k_refs = (kp_ref, kc_ref, kn_ref)
        self.v_refs = (vp_ref, vc_ref, vn_ref)
        self._mask = None

    def conv_chunk(self, c):
        f32 = jnp.float32
        cols = slice(c * CONV_CHUNK, (c + 1) * CONV_CHUNK)
        row = lax.broadcasted_iota(jnp.int32, (BLOCK, CONV_CHUNK), 0)
        u = self.cc_ref[:, cols].astype(f32) * self.ch_ref[:, cols].astype(f32)
        u_prev = jnp.where(self.blk > 0,
                           self.ccp_ref[HALO_ROWS - 1:HALO_ROWS, cols].astype(f32)
                           * self.chp_ref[HALO_ROWS - 1:HALO_ROWS, cols].astype(f32), 0.0)
        u_next = jnp.where(self.blk < N_BLOCKS - 1,
                           self.ccn_ref[0:1, cols].astype(f32)
                           * self.chn_ref[0:1, cols].astype(f32), 0.0)
        um1 = jnp.where(row == 0, u_prev, pltpu.roll(u, 1, axis=0))
        up1 = jnp.where(row == BLOCK - 1, u_next, pltpu.roll(u, BLOCK - 1, axis=0))
        cw_ref = self.cw_ref
        conv = um1 * cw_ref[0:1, cols] + u * cw_ref[1:2, cols] + up1 * cw_ref[2:3, cols]
        y = self.cb_ref[:, cols].astype(f32) * conv * _silu(self.cz_ref[:, cols].astype(f32))
        self.store(0, c * CONV_CHUNK, CONV_CHUNK, y.astype(jnp.bfloat16))

    def _band_mask(self):
        if self._mask is None:
            band = 3 * BLOCK
            qi = lax.broadcasted_iota(jnp.int32, (BLOCK, band), 0)
            kj = lax.broadcasted_iota(jnp.int32, (BLOCK, band), 1) - BLOCK
            dist_i = jnp.abs(qi - kj)
            k_pos = kj + self.blk * BLOCK
            valid = (dist_i <= WINDOW) & (k_pos >= 0) & (k_pos < SEQ)
            self._mask = (valid, -dist_i.astype(jnp.float32))
        return self._mask

    def scores(self, kg):
        kv_cols = slice(kg * HEAD_DIM, (kg + 1) * HEAD_DIM)
        q4 = jnp.concatenate(
            [self.q_ref[:, (kg * GQA_GROUP + g) * HEAD_DIM:(kg * GQA_GROUP + g + 1) * HEAD_DIM]
             for g in range(GQA_GROUP)], axis=0)
        kb = jnp.concatenate([r[:, kv_cols] for r in self.k_refs], axis=0)
        return lax.dot_general(q4, kb, (((1,), (1,)), ((), ())),
                               preferred_element_type=jnp.float32)

    def softmax_head(self, kg, g, s4):
        valid, neg_dist = self._band_mask()
        heads_per_half = N_Q_HEADS // 2
        half_slope = jnp.where(self.half == 1, 2.0 ** (-8.0 * heads_per_half / N_Q_HEADS),
                               1.0).astype(jnp.float32)
        local = kg * GQA_GROUP + g
        slope = half_slope * (2.0 ** (-8.0 * (local + 1) / N_Q_HEADS))
        sink = self.sink_ref[self.half * heads_per_half + local]
        s = s4[g * BLOCK:(g + 1) * BLOCK, :] * (HEAD_DIM ** -0.5) + neg_dist * slope
        s = jnp.where(valid, s, NEG_INF)
        m = jnp.maximum(jnp.max(s, axis=-1, keepdims=True), sink)
        e = jnp.exp(s - m)
        return e.astype(jnp.bfloat16), jnp.sum(e, axis=-1, keepdims=True) + jnp.exp(sink - m)

    def weighted_values(self, kg, e_parts):
        kv_cols = slice(kg * HEAD_DIM, (kg + 1) * HEAD_DIM)
        vb = jnp.concatenate([r[:, kv_cols] for r in self.v_refs], axis=0)
        return jnp.dot(jnp.concatenate(e_parts, axis=0), vb,
                       preferred_element_type=jnp.float32)

    def finish(self, kg, o4, l_parts):
        for g in range(GQA_GROUP):
            local = kg * GQA_GROUP + g
            o = o4[g * BLOCK:(g + 1) * BLOCK, :] / l_parts[g]
            gate = _silu(self.az_ref[:, local * HEAD_DIM:(local + 1) * HEAD_DIM].astype(jnp.float32))
            self.store(CONV_WIDTH, local * HEAD_DIM, HEAD_DIM, (o * gate).astype(jnp.bfloat16))

    def run_all(self):
        for c in range(self.N_CONV_CHUNKS):
            self.conv_chunk(c)
        for kg in range(self.N_KV_GROUPS):
            s4 = self.scores(kg)
            heads = [self.softmax_head(kg, g, s4) for g in range(GQA_GROUP)]
            self.finish(kg, self.weighted_values(kg, [e for e, _ in heads]), [l for _, l in heads])


N_MIXER_REFS = 17


def _mixer_in_specs(blk_of, half_of):
    halo_per_block = BLOCK // HALO_ROWS
    n_halo = SEQ // HALO_ROWS

    def spec(shape, row_fn, col_base, col_width):
        def index_map(*idx):
            return (row_fn(blk_of(*idx)), col_base // col_width + half_of(*idx))
        return pl.BlockSpec(shape, index_map)

    cur = lambda b: b
    prev_blk = lambda b: jnp.maximum(b - 1, 0)
    next_blk = lambda b: jnp.minimum(b + 1, N_BLOCKS - 1)
    prev_halo = lambda b: jnp.maximum(b * halo_per_block - 1, 0)
    next_halo = lambda b: jnp.minimum((b + 1) * halo_per_block, n_halo - 1)

    wide = (BLOCK, HALF_WIDTH)
    halo = (HALO_ROWS, HALF_WIDTH)
    kv = (BLOCK, KV_WIDTH // 2)
    return [
        spec(wide, cur, OFF_CB, HALF_WIDTH),
        spec(wide, cur, OFF_CC, HALF_WIDTH),
        spec(wide, cur, OFF_CH, HALF_WIDTH),
        spec(wide, cur, OFF_CZ, HALF_WIDTH),
        spec(halo, prev_halo, OFF_CC, HALF_WIDTH),
        spec(halo, prev_halo, OFF_CH, HALF_WIDTH),
        spec(halo, next_halo, OFF_CC, HALF_WIDTH),
        spec(halo, next_halo, OFF_CH, HALF_WIDTH),
        spec(wide, cur, OFF_Q, HALF_WIDTH),
        spec(kv, prev_blk, OFF_K, KV_WIDTH // 2),
        spec(kv, cur, OFF_K, KV_WIDTH // 2),
        spec(kv, next_blk, OFF_K, KV_WIDTH // 2),
        spec(kv, prev_blk, OFF_V, KV_WIDTH // 2),
        spec(kv, cur, OFF_V, KV_WIDTH // 2),
        spec(kv, next_blk, OFF_V, KV_WIDTH // 2),
        spec(wide, cur, OFF_AZ, HALF_WIDTH),
        pl.BlockSpec((3, HALF_WIDTH), lambda *idx: (0, half_of(*idx))),
    ]


def _mixers_first_tile_kernel(sink_ref, *refs):
    mixer_refs, y_ref = refs[:N_MIXER_REFS], refs[N_MIXER_REFS]
    blk = pl.program_id(0)
    half = pl.program_id(1)

    def store(section, col, width, value):
        start = pl.multiple_of(half * HALF_WIDTH + (section + col), HEAD_DIM)
        y_ref[:, pl.ds(start, width)] = value

    _MixerHalfBlock(blk, half, sink_ref, *mixer_refs, store).run_all()


def _mixers_first_tile(p, conv_w, sink):
    in_specs = [pl.BlockSpec(memory_space=pltpu.SMEM)] + _mixer_in_specs(
        lambda b, hf: b, lambda b, hf: hf)
    return pl.pallas_call(
        _mixers_first_tile_kernel,
        name="mixers",
        grid=(OUT_TM // BLOCK, 2),
        in_specs=in_specs,
        out_specs=pl.BlockSpec((BLOCK, D_MODEL), lambda b, hf: (b, 0)),
        out_shape=jax.ShapeDtypeStruct((OUT_TM, D_MODEL), jnp.bfloat16),
        compiler_params=pltpu.CompilerParams(
            dimension_semantics=("arbitrary", "arbitrary"),
            vmem_limit_bytes=V7X_VMEM_LIMIT_BYTES),
    )(sink, *([p] * (N_MIXER_REFS - 1)), conv_w)


def _next_tile_block(i, n):
    return jnp.minimum((i + 1) * (OUT_TM // BLOCK) + n // 2, N_BLOCKS - 1)


def _out_proj_kernel(sink_ref, y0_ref, x_ref, mu_ref, rstd_ref, g0_ref, b0_ref, w_ref,
                     g1_ref, b1_ref, *refs):
    mixer_refs = refs[:N_MIXER_REFS]
    o_ref, z_ref, zmean_ref, zm2_ref, y_even_ref, y_odd_ref = refs[N_MIXER_REFS:]
    i = pl.program_id(0)
    n = pl.program_id(1)

    @pl.when((i == 0) & (n == 0))
    def _():
        y_even_ref[...] = y0_ref[...]

    @pl.when(n == 0)
    def _():
        zmean_ref[...] = jnp.zeros_like(zmean_ref)
        zm2_ref[...] = jnp.zeros_like(zm2_ref)

    def step(cur_ref, nxt_ref):
        half = n % 2
        row0 = pl.multiple_of((n // 2) * BLOCK, BLOCK)

        def store(section, col, width, value):
            start = pl.multiple_of(half * HALF_WIDTH + (section + col), HEAD_DIM)
            nxt_ref[pl.ds(row0, BLOCK), pl.ds(start, width)] = value

        mx = _MixerHalfBlock(_next_tile_block(i, n), half, sink_ref, *mixer_refs, store)

        def dot_piece(k):
            ks = slice(k * OUT_K_PIECE, (k + 1) * OUT_K_PIECE)
            return jnp.dot(cur_ref[:, ks], w_ref[ks, :], preferred_element_type=jnp.float32)

        band_scores = [mx.scores(kg) for kg in range(mx.N_KV_GROUPS)]
        heads = []
        y = None
        for k in range(OUT_N_PIECES):
            piece = dot_piece(k)
            y = piece if y is None else y + piece
            kg, g = divmod(k, GQA_GROUP)
            heads.append(mx.softmax_head(kg, g, band_scores[kg]))
            mx.conv_chunk(k)
        outs = [mx.weighted_values(kg, [e for e, _ in heads[kg * GQA_GROUP:(kg + 1) * GQA_GROUP]])
                for kg in range(mx.N_KV_GROUPS)]

        g0 = g0_ref[...]
        b0 = b0_ref[...]
        own_lane = lax.broadcasted_iota(jnp.int32, (OUT_ROW_CHUNK, STAT_LANES), 1) == n
        for c in range(OUT_TM // OUT_ROW_CHUNK):
            rows = slice(c * OUT_ROW_CHUNK, (c + 1) * OUT_ROW_CHUNK)
            h = (x_ref[rows, :] - mu_ref[rows, :]) * rstd_ref[rows, :] * g0 + b0
            z = DN_ALPHA * h + y[rows, :]
            z_ref[n, rows, :] = z
            zm = jnp.mean(z, axis=-1, keepdims=True)
            zc = z - zm
            zmean_ref[rows, :] = jnp.where(own_lane, zm, zmean_ref[rows, :])
            zm2_ref[rows, :] = jnp.where(own_lane, jnp.sum(zc * zc, axis=-1, keepdims=True),
                                         zm2_ref[rows, :])

        for kg in range(mx.N_KV_GROUPS):
            mx.finish(kg, outs[kg], [l for _, l in heads[kg * GQA_GROUP:(kg + 1) * GQA_GROUP]])

    @pl.when(i % 2 == 0)
    def _():
        step(y_even_ref, y_odd_ref)

    @pl.when(i % 2 == 1)
    def _():
        step(y_odd_ref, y_even_ref)

    @pl.when(n == OUT_NT - 1)
    def _():
        used_lane = lax.broadcasted_iota(jnp.int32, (OUT_ROW_CHUNK, STAT_LANES), 1) < OUT_NT

        def body(c, carry):
            rows = pl.ds(pl.multiple_of(c * OUT_ROW_CHUNK, OUT_ROW_CHUNK), OUT_ROW_CHUNK)
            means = zmean_ref[rows, :]
            mean = jnp.sum(means, axis=-1, keepdims=True) * (1.0 / OUT_NT)
            dm = jnp.where(used_lane, means - mean, 0.0)
            m2 = jnp.sum(zm2_ref[rows, :] + OUT_TN * (dm * dm), axis=-1, keepdims=True)
            rstd = lax.rsqrt(m2 * (1.0 / D_MODEL) + LN_EPS)
            for k in range(OUT_NT):
                cols = slice(k * OUT_TN, (k + 1) * OUT_TN)
                o_ref[rows, cols] = (z_ref[k, rows, :] - mean) * rstd * g1_ref[:, cols] + b1_ref[:, cols]
            return carry

        lax.fori_loop(0, OUT_TM // OUT_ROW_CHUNK, body, 0)


def _out_proj(x2d, mu, rstd, g0, b0, ymix0, w, g1, b1, p, conv_w, sink):
    assert OUT_NT == 2 * (OUT_TM // BLOCK)
    grid = (SEQ // OUT_TM, OUT_NT)
    row_stat = pl.BlockSpec((OUT_TM, 1), lambda i, n: (i, 0))
    col_vec = pl.BlockSpec((1, OUT_TN), lambda i, n: (0, n))
    full_vec = pl.BlockSpec((1, D_MODEL), lambda i, n: (0, 0))
    in_specs = [
        pl.BlockSpec(memory_space=pltpu.SMEM),
        pl.BlockSpec((OUT_TM, D_MODEL), lambda i, n: (0, 0),
                     pipeline_mode=pl.Buffered(1)),
        pl.BlockSpec((OUT_TM, OUT_TN), lambda i, n: (i, n)),
        row_stat, row_stat,
        col_vec, col_vec,
        pl.BlockSpec((D_MODEL, OUT_TN), lambda i, n: (0, n)),
        full_vec, full_vec,
    ] + _mixer_in_specs(_next_tile_block, lambda i, n: n % 2)
    return pl.pallas_call(
        _out_proj_kernel,
        name="out_proj",
        grid=grid,
        in_specs=in_specs,
        out_specs=pl.BlockSpec((OUT_TM, D_MODEL), lambda i, n: (i, 0)),
        out_shape=jax.ShapeDtypeStruct((SEQ, D_MODEL), jnp.float32),
        scratch_shapes=[
            pltpu.VMEM((OUT_NT, OUT_TM, OUT_TN), jnp.float32),
            pltpu.VMEM((OUT_TM, STAT_LANES), jnp.float32),
            pltpu.VMEM((OUT_TM, STAT_LANES), jnp.float32),
            pltpu.VMEM((OUT_TM, D_MODEL), jnp.bfloat16),
            pltpu.VMEM((OUT_TM, D_MODEL), jnp.bfloat16),
        ],
        compiler_params=pltpu.CompilerParams(
            dimension_semantics=("arbitrary", "arbitrary"),
            vmem_limit_bytes=V7X_VMEM_LIMIT_BYTES),
    )(sink, ymix0, x2d, mu, rstd, g0, b0, w, g1, b1, *([p] * (N_MIXER_REFS - 1)), conv_w)


def kernel(x, emb_ln_g, emb_ln_b, w_in, conv_w, sink, w_out, ln_g, ln_b):
    batch, seq, d_model = x.shape
    assert (batch, seq, d_model) == (1, SEQ, D_MODEL)
    assert w_in.shape == (1, D_MODEL, PROJ_WIDTH) and w_out.shape == (1, D_MODEL, D_MODEL)
    x2d = x.reshape(SEQ, D_MODEL)
    g0 = emb_ln_g.reshape(1, D_MODEL)
    b0 = emb_ln_b.reshape(1, D_MODEL)
    g1 = ln_g.reshape(1, D_MODEL)
    b1 = ln_b.reshape(1, D_MODEL)
    h0, mu0, rstd0 = _entry_norm_first_tile(x2d, g0, b0)
    p, mu_rest, rstd_rest, w_out_bf16 = _in_proj(h0, x2d, g0, b0, w_in[0], w_out[0])
    mu = jnp.concatenate([mu0, mu_rest[IN_TM:]], axis=0)
    rstd = jnp.concatenate([rstd0, rstd_rest[IN_TM:]], axis=0)
    ymix0 = _mixers_first_tile(p, conv_w[0], sink[0])
    out = _out_proj(x2d, mu, rstd, g0, b0, ymix0, w_out_bf16, g1, b1, p, conv_w[0], sink[0])
    return out.reshape(1, SEQ, D_MODEL)
```

```python
import jax
import jax.numpy as jnp
from jax import lax
from jax.experimental import pallas as pl
from jax.experimental.pallas import tpu as pltpu

D_MODEL = 4096
SEQ = 8192
CONV_WIDTH = D_MODEL // 2
ATTN_WIDTH = D_MODEL - CONV_WIDTH
HEAD_DIM = 128
N_Q_HEADS = ATTN_WIDTH // HEAD_DIM
N_KV_HEADS = N_Q_HEADS // 4
GQA_GROUP = N_Q_HEADS // N_KV_HEADS
KV_WIDTH = N_KV_HEADS * HEAD_DIM
WINDOW = 128
BLOCK = 128
N_BLOCKS = SEQ // BLOCK
DN_ALPHA = 2.0 ** 0.25
LN_EPS = 1e-5
NEG_INF = -1e30
LOG2_E = 1.4426950408889634
PROJ_WIDTH = 4 * CONV_WIDTH + ATTN_WIDTH + 2 * KV_WIDTH + ATTN_WIDTH

OFF_CB = 0
OFF_CC = OFF_CB + CONV_WIDTH
OFF_CH = OFF_CC + CONV_WIDTH
OFF_CZ = OFF_CH + CONV_WIDTH
OFF_Q = OFF_CZ + CONV_WIDTH
OFF_K = OFF_Q + ATTN_WIDTH
OFF_V = OFF_K + KV_WIDTH
OFF_AZ = OFF_V + KV_WIDTH

V7X_VMEM_LIMIT_BYTES = 58 * 1024 * 1024
SUBLANES = 8

NORM_TM = 256
NORM_UNROLL = 8
BF16_ROWS = 16
IN_TM, IN_TN = 1024, 512
IN_LN_STEPS = 16
IN_LN_ROWS = IN_TM // IN_LN_STEPS
WOUT_CAST_ROWS = 32
OUT_TM, OUT_TN = 512, 512
OUT_NT = D_MODEL // OUT_TN
OUT_ROW_CHUNK = 64
OUT_M_PIECE = OUT_TM // 2
STAT_LANES = 128
assert OUT_NT <= STAT_LANES
HALO_ROWS = 16
CONV_CHUNK = 128
HALF_WIDTH = CONV_WIDTH // 2
assert HALF_WIDTH == ATTN_WIDTH // 2
assert HALF_WIDTH // CONV_CHUNK == N_Q_HEADS // 2


def _silu(z):
    return z / (1.0 + jnp.exp(-z))


def _entry_norm_kernel(x_ref, g_ref, b_ref, h_ref, mu_ref, rstd_ref):
    n_groups = NORM_TM // SUBLANES

    def group(r):
        return pl.ds(pl.multiple_of(r * SUBLANES, SUBLANES), SUBLANES)

    def mean_body(r, carry):
        rows = group(r)
        mu_ref[rows, :] = jnp.mean(x_ref[rows, :], axis=-1, keepdims=True)
        return carry

    def var_body(r, carry):
        rows = group(r)
        xc = x_ref[rows, :] - mu_ref[rows, :]
        var = jnp.mean(xc * xc, axis=-1, keepdims=True)
        rstd_ref[rows, :] = lax.rsqrt(var + LN_EPS)
        return carry

    def scale_body(r, carry):
        rows = group(r)
        y = (x_ref[rows, :] - mu_ref[rows, :]) * rstd_ref[rows, :] * g_ref[...] + b_ref[...]
        h_ref[rows, :] = y.astype(jnp.bfloat16)
        return carry

    lax.fori_loop(0, n_groups, mean_body, 0, unroll=NORM_UNROLL)
    lax.fori_loop(0, n_groups, var_body, 0, unroll=NORM_UNROLL)
    lax.fori_loop(0, n_groups, scale_body, 0, unroll=NORM_UNROLL)


def _entry_norm_first_tile(x2d, g, b):
    vec = pl.BlockSpec((1, D_MODEL), lambda i: (0, 0))
    return pl.pallas_call(
        _entry_norm_kernel,
        name="entry_norm",
        grid=(IN_TM // NORM_TM,),
        in_specs=[pl.BlockSpec((NORM_TM, D_MODEL), lambda i: (i, 0)), vec, vec],
        out_specs=pl.BlockSpec((NORM_TM, D_MODEL), lambda i: (i, 0)),
        out_shape=jax.ShapeDtypeStruct((IN_TM, D_MODEL), jnp.bfloat16),
        scratch_shapes=[pltpu.VMEM((NORM_TM, 1), jnp.float32),
                        pltpu.VMEM((NORM_TM, 1), jnp.float32)],
        compiler_params=pltpu.CompilerParams(
            dimension_semantics=("arbitrary",),
            vmem_limit_bytes=V7X_VMEM_LIMIT_BYTES),
    )(x2d, g, b)


def _in_proj_kernel(h0_ref, x_ref, g_ref, b_ref, w_ref, wo_ref,
                    p_ref, mu_ref, rstd_ref, wob_ref, h_even_ref, h_odd_ref):
    i = pl.program_id(0)
    j = pl.program_id(1)

    @pl.when((i == 0) & (j == 0))
    def _():
        h_even_ref[...] = h0_ref[...]

    def step(cur_ref, nxt_ref):
        p_ref[...] = jnp.dot(cur_ref[...], w_ref[...].astype(jnp.bfloat16),
                             preferred_element_type=jnp.float32).astype(jnp.bfloat16)
        chunk = jnp.minimum(j, IN_LN_STEPS - 1)
        base = pl.multiple_of(chunk * IN_LN_ROWS, IN_LN_ROWS)
        g = g_ref[...]
        b = b_ref[...]
        for r in range(IN_LN_ROWS // BF16_ROWS):
            rows = slice(r * BF16_ROWS, (r + 1) * BF16_ROWS)
            x = x_ref[rows, :]
            mu = jnp.mean(x, axis=-1, keepdims=True)
            xc = x - mu
            rstd = lax.rsqrt(jnp.mean(xc * xc, axis=-1, keepdims=True) + LN_EPS)
            mu_ref[rows, :] = mu
            rstd_ref[rows, :] = rstd
            nxt_ref[pl.ds(base + r * BF16_ROWS, BF16_ROWS), :] = (
                (xc * rstd * g + b).astype(jnp.bfloat16))
        wob_ref[...] = wo_ref[...].astype(jnp.bfloat16)

    @pl.when(i % 2 == 0)
    def _():
        step(h_even_ref, h_odd_ref)

    @pl.when(i % 2 == 1)
    def _():
        step(h_odd_ref, h_even_ref)


def _in_proj(h0, x2d, g, b, w_in, w_out):
    n_i, n_j = SEQ // IN_TM, PROJ_WIDTH // IN_TN
    n_chunks = SEQ // IN_LN_ROWS
    n_slabs = D_MODEL // WOUT_CAST_ROWS
    assert n_i * n_j >= n_slabs and n_j >= IN_LN_STEPS

    def next_tile_chunk(i, j):
        return (lax.rem(i + 1, n_i) * IN_LN_STEPS + jnp.minimum(j, IN_LN_STEPS - 1), 0)

    def slab(i, j):
        return (jnp.minimum(i * n_j + j, n_slabs - 1), 0)

    vec = pl.BlockSpec((1, D_MODEL), lambda i, j: (0, 0))
    return pl.pallas_call(
        _in_proj_kernel,
        name="in_proj",
        grid=(n_i, n_j),
        in_specs=[
            pl.BlockSpec((IN_TM, D_MODEL), lambda i, j: (0, 0), pipeline_mode=pl.Buffered(1)),
            pl.BlockSpec((IN_LN_ROWS, D_MODEL), next_tile_chunk),
            vec, vec,
            pl.BlockSpec((D_MODEL, IN_TN), lambda i, j: (0, j)),
            pl.BlockSpec((WOUT_CAST_ROWS, D_MODEL), slab),
        ],
        out_specs=[
            pl.BlockSpec((IN_TM, IN_TN), lambda i, j: (i, j)),
            pl.BlockSpec((IN_LN_ROWS, 1), next_tile_chunk),
            pl.BlockSpec((IN_LN_ROWS, 1), next_tile_chunk),
            pl.BlockSpec((WOUT_CAST_ROWS, D_MODEL), slab),
        ],
        out_shape=[
            jax.ShapeDtypeStruct((SEQ, PROJ_WIDTH), jnp.bfloat16),
            jax.ShapeDtypeStruct((SEQ, 1), jnp.float32),
            jax.ShapeDtypeStruct((SEQ, 1), jnp.float32),
            jax.ShapeDtypeStruct((D_MODEL, D_MODEL), jnp.bfloat16),
        ],
        scratch_shapes=[pltpu.VMEM((IN_TM, D_MODEL), jnp.bfloat16),
                        pltpu.VMEM((IN_TM, D_MODEL), jnp.bfloat16)],
        compiler_params=pltpu.CompilerParams(
            dimension_semantics=("arbitrary", "arbitrary"),
            vmem_limit_bytes=V7X_VMEM_LIMIT_BYTES),
    )(h0, x2d, g, b, w_in, w_out)


class _MixerHalfBlock:
    N_CONV_CHUNKS = HALF_WIDTH // CONV_CHUNK
    N_KV_GROUPS = N_KV_HEADS // 2

    def __init__(self, blk, half, sink_ref, cb_ref, cc_ref, ch_ref, cz_ref,
                 ccp_ref, chp_ref, ccn_ref, chn_ref,
                 q_ref, kp_ref, kc_ref, kn_ref, vp_ref, vc_ref, vn_ref,
                 az_ref, cw_ref, store):
        self.blk, self.half, self.sink_ref, self.store = blk, half, sink_ref, store
        self.cb_ref, self.cc_ref, self.ch_ref, self.cz_ref = cb_ref, cc_ref, ch_ref, cz_ref
        self.ccp_ref, self.chp_ref, self.ccn_ref, self.chn_ref = ccp_ref, chp_ref, ccn_ref, chn_ref
        self.q_ref, self.az_ref, self.cw_ref = q_ref, az_ref, cw_ref
        self.k_refs = (kp_ref, kc_ref, kn_ref)
        self.v_refs = (vp_ref, vc_ref, vn_ref)
        self._mask = None

    def conv_chunk(self, c):
        f32 = jnp.float32
        cols = slice(c * CONV_CHUNK, (c + 1) * CONV_CHUNK)
        row = lax.broadcasted_iota(jnp.int32, (BLOCK, CONV_CHUNK), 0)
        u = self.cc_ref[:, cols].astype(f32) * self.ch_ref[:, cols].astype(f32)
        u_prev = jnp.where(self.blk > 0,
                           self.ccp_ref[HALO_ROWS - 1:HALO_ROWS, cols].astype(f32)
                           * self.chp_ref[HALO_ROWS - 1:HALO_ROWS, cols].astype(f32), 0.0)
        u_next = jnp.where(self.blk < N_BLOCKS - 1,
                           self.ccn_ref[0:1, cols].astype(f32)
                           * self.chn_ref[0:1, cols].astype(f32), 0.0)
        um1 = jnp.where(row == 0, u_prev, pltpu.roll(u, 1, axis=0))
        up1 = jnp.where(row == BLOCK - 1, u_next, pltpu.roll(u, BLOCK - 1, axis=0))
        cw_ref = self.cw_ref
        conv = um1 * cw_ref[0:1, cols] + u * cw_ref[1:2, cols] + up1 * cw_ref[2:3, cols]
        y = self.cb_ref[:, cols].astype(f32) * conv * _silu(self.cz_ref[:, cols].astype(f32))
        self.store(0, c * CONV_CHUNK, CONV_CHUNK, y.astype(jnp.bfloat16))

    def _band_mask(self):
        if self._mask is None:
            band = 3 * BLOCK
            qi = lax.broadcasted_iota(jnp.int32, (BLOCK, band), 0)
            kj = lax.broadcasted_iota(jnp.int32, (BLOCK, band), 1) - BLOCK
            dist_i = jnp.abs(qi - kj)
            k_pos = kj + self.blk * BLOCK
            valid = (dist_i <= WINDOW) & (k_pos >= 0) & (k_pos < SEQ)
            self._mask = (valid, dist_i.astype(jnp.float32) * -LOG2_E)
        return self._mask

    def scores(self, kg):
        kv_cols = slice(kg * HEAD_DIM, (kg + 1) * HEAD_DIM)
        q4 = jnp.concatenate(
            [self.q_ref[:, (kg * GQA_GROUP + g) * HEAD_DIM:(kg * GQA_GROUP + g + 1) * HEAD_DIM]
             for g in range(GQA_GROUP)], axis=0)
        kb = jnp.concatenate([r[:, kv_cols] for r in self.k_refs], axis=0)
        return lax.dot_general(q4, kb, (((1,), (1,)), ((), ())),
                               preferred_element_type=jnp.float32)

    def softmax_head(self, kg, g, s4):
        valid, neg_dist_log2 = self._band_mask()
        heads_per_half = N_Q_HEADS // 2
        half_slope = jnp.where(self.half == 1, 2.0 ** (-8.0 * heads_per_half / N_Q_HEADS),
                               1.0).astype(jnp.float32)
        local = kg * GQA_GROUP + g
        slope = half_slope * (2.0 ** (-8.0 * (local + 1) / N_Q_HEADS))
        sink = self.sink_ref[self.half * heads_per_half + local] * LOG2_E
        s = s4[g * BLOCK:(g + 1) * BLOCK, :] * (HEAD_DIM ** -0.5 * LOG2_E) + neg_dist_log2 * slope
        s = jnp.where(valid, s, NEG_INF)
        m = jnp.maximum(jnp.max(s, axis=-1, keepdims=True), sink)
        e = jnp.exp2(s - m)
        return e.astype(jnp.bfloat16), jnp.sum(e, axis=-1, keepdims=True) + jnp.exp2(sink - m)

    def weighted_values(self, kg, e_parts):
        kv_cols = slice(kg * HEAD_DIM, (kg + 1) * HEAD_DIM)
        vb = jnp.concatenate([r[:, kv_cols] for r in self.v_refs], axis=0)
        return jnp.dot(jnp.concatenate(e_parts, axis=0), vb,
                       preferred_element_type=jnp.float32)

    def finish(self, kg, o4, l_parts):
        for g in range(GQA_GROUP):
            local = kg * GQA_GROUP + g
            o = o4[g * BLOCK:(g + 1) * BLOCK, :] / l_parts[g]
            gate = _silu(self.az_ref[:, local * HEAD_DIM:(local + 1) * HEAD_DIM].astype(jnp.float32))
            self.store(CONV_WIDTH, local * HEAD_DIM, HEAD_DIM, (o * gate).astype(jnp.bfloat16))

    def run_all(self):
        for c in range(self.N_CONV_CHUNKS):
            self.conv_chunk(c)
        for kg in range(self.N_KV_GROUPS):
            s4 = self.scores(kg)
            heads = [self.softmax_head(kg, g, s4) for g in range(GQA_GROUP)]
            self.finish(kg, self.weighted_values(kg, [e for e, _ in heads]), [l for _, l in heads])


N_MIXER_REFS = 17


def _mixer_in_specs(blk_of, half_of):
    halo_per_block = BLOCK // HALO_ROWS
    n_halo = SEQ // HALO_ROWS

    def spec(shape, row_fn, col_base, col_width):
        def index_map(*idx):
            return (row_fn(blk_of(*idx)), col_base // col_width + half_of(*idx))
        return pl.BlockSpec(shape, index_map)

    cur = lambda b: b
    prev_blk = lambda b: jnp.maximum(b - 1, 0)
    next_blk = lambda b: jnp.minimum(b + 1, N_BLOCKS - 1)
    prev_halo = lambda b: jnp.maximum(b * halo_per_block - 1, 0)
    next_halo = lambda b: jnp.minimum((b + 1) * halo_per_block, n_halo - 1)

    wide = (BLOCK, HALF_WIDTH)
    halo = (HALO_ROWS, HALF_WIDTH)
    kv = (BLOCK, KV_WIDTH // 2)
    return [
        spec(wide, cur, OFF_CB, HALF_WIDTH),
        spec(wide, cur, OFF_CC, HALF_WIDTH),
        spec(wide, cur, OFF_CH, HALF_WIDTH),
        spec(wide, cur, OFF_CZ, HALF_WIDTH),
        spec(halo, prev_halo, OFF_CC, HALF_WIDTH),
        spec(halo, prev_halo, OFF_CH, HALF_WIDTH),
        spec(halo, next_halo, OFF_CC, HALF_WIDTH),
        spec(halo, next_halo, OFF_CH, HALF_WIDTH),
        spec(wide, cur, OFF_Q, HALF_WIDTH),
        spec(kv, prev_blk, OFF_K, KV_WIDTH // 2),
        spec(kv, cur, OFF_K, KV_WIDTH // 2),
        spec(kv, next_blk, OFF_K, KV_WIDTH // 2),
        spec(kv, prev_blk, OFF_V, KV_WIDTH // 2),
        spec(kv, cur, OFF_V, KV_WIDTH // 2),
        spec(kv, next_blk, OFF_V, KV_WIDTH // 2),
        spec(wide, cur, OFF_AZ, HALF_WIDTH),
        pl.BlockSpec((3, HALF_WIDTH), lambda *idx: (0, half_of(*idx))),
    ]


def _mixers_first_tile_kernel(sink_ref, *refs):
    mixer_refs, y_ref = refs[:N_MIXER_REFS], refs[N_MIXER_REFS]
    blk = pl.program_id(0)
    half = pl.program_id(1)

    def store(section, col, width, value):
        start = pl.multiple_of(half * HALF_WIDTH + (section + col), HEAD_DIM)
        y_ref[:, pl.ds(start, width)] = value

    _MixerHalfBlock(blk, half, sink_ref, *mixer_refs, store).run_all()


def _mixers_first_tile(p, conv_w, sink):
    in_specs = [pl.BlockSpec(memory_space=pltpu.SMEM)] + _mixer_in_specs(
        lambda b, hf: b, lambda b, hf: hf)
    return pl.pallas_call(
        _mixers_first_tile_kernel,
        name="mixers",
        grid=(OUT_TM // BLOCK, 2),
        in_specs=in_specs,
        out_specs=pl.BlockSpec((BLOCK, D_MODEL), lambda b, hf: (b, 0)),
        out_shape=jax.ShapeDtypeStruct((OUT_TM, D_MODEL), jnp.bfloat16),
        compiler_params=pltpu.CompilerParams(
            dimension_semantics=("arbitrary", "arbitrary"),
            vmem_limit_bytes=V7X_VMEM_LIMIT_BYTES),
    )(sink, *([p] * (N_MIXER_REFS - 1)), conv_w)


def _next_tile_block(i, n):
    return jnp.minimum((i + 1) * (OUT_TM // BLOCK) + n // 2, N_BLOCKS - 1)


def _out_proj_kernel(sink_ref, y0_ref, x_ref, mu_ref, rstd_ref, g0_ref, b0_ref, w_ref,
                     g1_ref, b1_ref, *refs):
    mixer_refs = refs[:N_MIXER_REFS]
    o_ref, z_ref, zmean_ref, zm2_ref, y_even_ref, y_odd_ref = refs[N_MIXER_REFS:]
    i = pl.program_id(0)
    n = pl.program_id(1)

    @pl.when((i == 0) & (n == 0))
    def _():
        y_even_ref[...] = y0_ref[...]

    @pl.when(n == 0)
    def _():
        zmean_ref[...] = jnp.zeros_like(zmean_ref)
        zm2_ref[...] = jnp.zeros_like(zm2_ref)

    def step(cur_ref, nxt_ref):
        half = n % 2
        row0 = pl.multiple_of((n // 2) * BLOCK, BLOCK)

        def store(section, col, width, value):
            start = pl.multiple_of(half * HALF_WIDTH + (section + col), HEAD_DIM)
            nxt_ref[pl.ds(row0, BLOCK), pl.ds(start, width)] = value

        mx = _MixerHalfBlock(_next_tile_block(i, n), half, sink_ref, *mixer_refs, store)

        g0 = g0_ref[...]
        b0 = b0_ref[...]
        own_lane = lax.broadcasted_iota(jnp.int32, (OUT_ROW_CHUNK, STAT_LANES), 1) == n

        def project_rows(m):
            r0 = m * OUT_M_PIECE
            y = jnp.dot(cur_ref[r0:r0 + OUT_M_PIECE, :], w_ref[...],
                        preferred_element_type=jnp.float32)
            for c in range(OUT_M_PIECE // OUT_ROW_CHUNK):
                rows = slice(r0 + c * OUT_ROW_CHUNK, r0 + (c + 1) * OUT_ROW_CHUNK)
                h = (x_ref[rows, :] - mu_ref[rows, :]) * rstd_ref[rows, :] * g0 + b0
                z = DN_ALPHA * h + y[c * OUT_ROW_CHUNK:(c + 1) * OUT_ROW_CHUNK, :]
                z_ref[n, rows, :] = z
                zm = jnp.mean(z, axis=-1, keepdims=True)
                zc = z - zm
                zmean_ref[rows, :] = jnp.where(own_lane, zm, zmean_ref[rows, :])
                zm2_ref[rows, :] = jnp.where(own_lane, jnp.sum(zc * zc, axis=-1, keepdims=True),
                                             zm2_ref[rows, :])

        assert mx.N_KV_GROUPS == OUT_TM // OUT_M_PIECE == 2
        pending = None
        for kg in range(mx.N_KV_GROUPS):
            s4 = mx.scores(kg)
            if pending is not None:
                pending = (pending[0], mx.weighted_values(*pending[:2]), pending[2])
            project_rows(kg)
            heads = []
            for g in range(GQA_GROUP):
                heads.append(mx.softmax_head(kg, g, s4))
                mx.conv_chunk(kg * GQA_GROUP + g)
            if pending is not None:
                mx.finish(*pending)
            pending = (kg, [e for e, _ in heads], [l for _, l in heads])
        mx.finish(pending[0], mx.weighted_values(*pending[:2]), pending[2])

    @pl.when(i % 2 == 0)
    def _():
        step(y_even_ref, y_odd_ref)

    @pl.when(i % 2 == 1)
    def _():
        step(y_odd_ref, y_even_ref)

    @pl.when(n == OUT_NT - 1)
    def _():
        used_lane = lax.broadcasted_iota(jnp.int32, (OUT_TM, STAT_LANES), 1) < OUT_NT
        means = zmean_ref[...]
        row_mean = jnp.sum(means, axis=-1, keepdims=True) * (1.0 / OUT_NT)
        dm = jnp.where(used_lane, means - row_mean, 0.0)
        m2 = jnp.sum(zm2_ref[...] + OUT_TN * (dm * dm), axis=-1, keepdims=True)
        zmean_ref[:, 0:1] = row_mean
        zm2_ref[:, 0:1] = lax.rsqrt(m2 * (1.0 / D_MODEL) + LN_EPS)

        def body(c, carry):
            rows = pl.ds(pl.multiple_of(c * OUT_ROW_CHUNK, OUT_ROW_CHUNK), OUT_ROW_CHUNK)
            mean = zmean_ref[rows, 0:1]
            rstd = zm2_ref[rows, 0:1]
            for k in range(OUT_NT):
                cols = slice(k * OUT_TN, (k + 1) * OUT_TN)
                o_ref[rows, cols] = (z_ref[k, rows, :] - mean) * rstd * g1_ref[:, cols] + b1_ref[:, cols]
            return carry

        lax.fori_loop(0, OUT_TM // OUT_ROW_CHUNK, body, 0)


def _out_proj(x2d, mu, rstd, g0, b0, ymix0, w, g1, b1, p, conv_w, sink):
    assert OUT_NT == 2 * (OUT_TM // BLOCK)
    grid = (SEQ // OUT_TM, OUT_NT)
    row_stat = pl.BlockSpec((OUT_TM, 1), lambda i, n: (i, 0))
    col_vec = pl.BlockSpec((1, OUT_TN), lambda i, n: (0, n))
    full_vec = pl.BlockSpec((1, D_MODEL), lambda i, n: (0, 0))
    in_specs = [
        pl.BlockSpec(memory_space=pltpu.SMEM),
        pl.BlockSpec((OUT_TM, D_MODEL), lambda i, n: (0, 0),
                     pipeline_mode=pl.Buffered(1)),
        pl.BlockSpec((OUT_TM, OUT_TN), lambda i, n: (i, n)),
        row_stat, row_stat,
        col_vec, col_vec,
        pl.BlockSpec((D_MODEL, OUT_TN), lambda i, n: (0, n)),
        full_vec, full_vec,
    ] + _mixer_in_specs(_next_tile_block, lambda i, n: n % 2)
    return pl.pallas_call(
        _out_proj_kernel,
        name="out_proj",
        grid=grid,
        in_specs=in_specs,
        out_specs=pl.BlockSpec((OUT_TM, D_MODEL), lambda i, n: (i, 0)),
        out_shape=jax.ShapeDtypeStruct((SEQ, D_MODEL), jnp.float32),
        scratch_shapes=[
            pltpu.VMEM((OUT_NT, OUT_TM, OUT_TN), jnp.float32),
            pltpu.VMEM((OUT_TM, STAT_LANES), jnp.float32),
            pltpu.VMEM((OUT_TM, STAT_LANES), jnp.float32),
            pltpu.VMEM((OUT_TM, D_MODEL), jnp.bfloat16),
            pltpu.VMEM((OUT_TM, D_MODEL), jnp.bfloat16),
        ],
        compiler_params=pltpu.CompilerParams(
            dimension_semantics=("arbitrary", "arbitrary"),
            vmem_limit_bytes=V7X_VMEM_LIMIT_BYTES),
    )(sink, ymix0, x2d, mu, rstd, g0, b0, w, g1, b1, *([p] * (N_MIXER_REFS - 1)), conv_w)


def kernel(x, emb_ln_g, emb_ln_b, w_in, conv_w, sink, w_out, ln_g, ln_b):
    batch, seq, d_model = x.shape
    assert (batch, seq, d_model) == (1, SEQ, D_MODEL)
    assert w_in.shape == (1, D_MODEL, PROJ_WIDTH) and w_out.shape == (1, D_MODEL, D_MODEL)
    x2d = x.reshape(SEQ, D_MODEL)
    g0 = emb_ln_g.reshape(1, D_MODEL)
    b0 = emb_ln_b.reshape(1, D_MODEL)
    g1 = ln_g.reshape(1, D_MODEL)
    b1 = ln_b.reshape(1, D_MODEL)
    h0 = _entry_norm_first_tile(x2d, g0, b0)
    p, mu, rstd, w_out_bf16 = _in_proj(h0, x2d, g0, b0, w_in[0], w_out[0])
    ymix0 = _mixers_first_tile(p, conv_w[0], sink[0])
    out = _out_proj(x2d, mu, rstd, g0, b0, ymix0, w_out_bf16, g1, b1, p, conv_w[0], sink[0])
    return out.reshape(1, SEQ, D_MODEL)
```

```python
import jax
import jax.numpy as jnp
from jax import lax
from jax.experimental import pallas as pl
from jax.experimental.pallas import tpu as pltpu

D_MODEL = 4096
SEQ = 8192
CONV_WIDTH = D_MODEL // 2
ATTN_WIDTH = D_MODEL - CONV_WIDTH
HEAD_DIM = 128
N_Q_HEADS = ATTN_WIDTH // HEAD_DIM
N_KV_HEADS = N_Q_HEADS // 4
GQA_GROUP = N_Q_HEADS // N_KV_HEADS
KV_WIDTH = N_KV_HEADS * HEAD_DIM
WINDOW = 128
BLOCK = 128
N_BLOCKS = SEQ // BLOCK
DN_ALPHA = 2.0 ** 0.25
LN_EPS = 1e-5
NEG_INF = -1e30
LOG2_E = 1.4426950408889634
PROJ_WIDTH = 4 * CONV_WIDTH + ATTN_WIDTH + 2 * KV_WIDTH + ATTN_WIDTH

OFF_CB = 0
OFF_CC = OFF_CB + CONV_WIDTH
OFF_CH = OFF_CC + CONV_WIDTH
OFF_CZ = OFF_CH + CONV_WIDTH
OFF_Q = OFF_CZ + CONV_WIDTH
OFF_K = OFF_Q + ATTN_WIDTH
OFF_V = OFF_K + KV_WIDTH
OFF_AZ = OFF_V + KV_WIDTH

V7X_VMEM_LIMIT_BYTES = 58 * 1024 * 1024
SUBLANES = 8

NORM_TM = 256
NORM_UNROLL = 8
BF16_ROWS = 16
IN_TM, IN_TN = 1024, 512
IN_LN_STEPS = 16
IN_LN_ROWS = IN_TM // IN_LN_STEPS
WOUT_CAST_ROWS = 32
OUT_TM, OUT_TN = 512, 512
OUT_NT = D_MODEL // OUT_TN
OUT_ROW_CHUNK = 64
OUT_N_PIECES = 8
OUT_K_PIECE = D_MODEL // OUT_N_PIECES
STAT_LANES = 128
assert OUT_NT <= STAT_LANES
HALO_ROWS = 16
CONV_CHUNK = 128
HALF_WIDTH = CONV_WIDTH // 2
assert HALF_WIDTH == ATTN_WIDTH // 2
assert HALF_WIDTH // CONV_CHUNK == OUT_N_PIECES == N_Q_HEADS // 2


def _silu(z):
    return z / (1.0 + jnp.exp(-z))


def _entry_norm_kernel(x_ref, g_ref, b_ref, h_ref, mu_ref, rstd_ref):
    n_groups = NORM_TM // SUBLANES

    def group(r):
        return pl.ds(pl.multiple_of(r * SUBLANES, SUBLANES), SUBLANES)

    def mean_body(r, carry):
        rows = group(r)
        mu_ref[rows, :] = jnp.mean(x_ref[rows, :], axis=-1, keepdims=True)
        return carry

    def var_body(r, carry):
        rows = group(r)
        xc = x_ref[rows, :] - mu_ref[rows, :]
        var = jnp.mean(xc * xc, axis=-1, keepdims=True)
        rstd_ref[rows, :] = lax.rsqrt(var + LN_EPS)
        return carry

    def scale_body(r, carry):
        rows = group(r)
        y = (x_ref[rows, :] - mu_ref[rows, :]) * rstd_ref[rows, :] * g_ref[...] + b_ref[...]
        h_ref[rows, :] = y.astype(jnp.bfloat16)
        return carry

    lax.fori_loop(0, n_groups, mean_body, 0, unroll=NORM_UNROLL)
    lax.fori_loop(0, n_groups, var_body, 0, unroll=NORM_UNROLL)
    lax.fori_loop(0, n_groups, scale_body, 0, unroll=NORM_UNROLL)


def _entry_norm_first_tile(x2d, g, b):
    vec = pl.BlockSpec((1, D_MODEL), lambda i: (0, 0))
    return pl.pallas_call(
        _entry_norm_kernel,
        name="entry_norm",
        grid=(IN_TM // NORM_TM,),
        in_specs=[pl.BlockSpec((NORM_TM, D_MODEL), lambda i: (i, 0)), vec, vec],
        out_specs=pl.BlockSpec((NORM_TM, D_MODEL), lambda i: (i, 0)),
        out_shape=jax.ShapeDtypeStruct((IN_TM, D_MODEL), jnp.bfloat16),
        scratch_shapes=[pltpu.VMEM((NORM_TM, 1), jnp.float32),
                        pltpu.VMEM((NORM_TM, 1), jnp.float32)],
        compiler_params=pltpu.CompilerParams(
            dimension_semantics=("arbitrary",),
            vmem_limit_bytes=V7X_VMEM_LIMIT_BYTES),
    )(x2d, g, b)


def _in_proj_kernel(h0_ref, x_ref, g_ref, b_ref, w_ref, wo_ref,
                    p_ref, mu_ref, rstd_ref, wob_ref, h_even_ref, h_odd_ref):
    i = pl.program_id(0)
    j = pl.program_id(1)

    @pl.when((i == 0) & (j == 0))
    def _():
        h_even_ref[...] = h0_ref[...]

    def step(cur_ref, nxt_ref):
        p_ref[...] = jnp.dot(cur_ref[...], w_ref[...].astype(jnp.bfloat16),
                             preferred_element_type=jnp.float32).astype(jnp.bfloat16)
        chunk = jnp.minimum(j, IN_LN_STEPS - 1)
        base = pl.multiple_of(chunk * IN_LN_ROWS, IN_LN_ROWS)
        g = g_ref[...]
        b = b_ref[...]
        for r in range(IN_LN_ROWS // BF16_ROWS):
            rows = slice(r * BF16_ROWS, (r + 1) * BF16_ROWS)
            x = x_ref[rows, :]
            mu = jnp.mean(x, axis=-1, keepdims=True)
            xc = x - mu
            rstd = lax.rsqrt(jnp.mean(xc * xc, axis=-1, keepdims=True) + LN_EPS)
            mu_ref[rows, :] = mu
            rstd_ref[rows, :] = rstd
            nxt_ref[pl.ds(base + r * BF16_ROWS, BF16_ROWS), :] = (
                (xc * rstd * g + b).astype(jnp.bfloat16))
        wob_ref[...] = wo_ref[...].astype(jnp.bfloat16)

    @pl.when(i % 2 == 0)
    def _():
        step(h_even_ref, h_odd_ref)

    @pl.when(i % 2 == 1)
    def _():
        step(h_odd_ref, h_even_ref)


def _in_proj(h0, x2d, g, b, w_in, w_out):
    n_i, n_j = SEQ // IN_TM, PROJ_WIDTH // IN_TN
    n_chunks = SEQ // IN_LN_ROWS
    n_slabs = D_MODEL // WOUT_CAST_ROWS
    assert n_i * n_j >= n_slabs and n_j >= IN_LN_STEPS

    def next_tile_chunk(i, j):
        return (lax.rem(i + 1, n_i) * IN_LN_STEPS + jnp.minimum(j, IN_LN_STEPS - 1), 0)

    def slab(i, j):
        return (jnp.minimum(i * n_j + j, n_slabs - 1), 0)

    vec = pl.BlockSpec((1, D_MODEL), lambda i, j: (0, 0))
    return pl.pallas_call(
        _in_proj_kernel,
        name="in_proj",
        grid=(n_i, n_j),
        in_specs=[
            pl.BlockSpec((IN_TM, D_MODEL), lambda i, j: (0, 0), pipeline_mode=pl.Buffered(1)),
            pl.BlockSpec((IN_LN_ROWS, D_MODEL), next_tile_chunk),
            vec, vec,
            pl.BlockSpec((D_MODEL, IN_TN), lambda i, j: (0, j)),
            pl.BlockSpec((WOUT_CAST_ROWS, D_MODEL), slab),
        ],
        out_specs=[
            pl.BlockSpec((IN_TM, IN_TN), lambda i, j: (i, j)),
            pl.BlockSpec((IN_LN_ROWS, 1), next_tile_chunk),
            pl.BlockSpec((IN_LN_ROWS, 1), next_tile_chunk),
            pl.BlockSpec((WOUT_CAST_ROWS, D_MODEL), slab),
        ],
        out_shape=[
            jax.ShapeDtypeStruct((SEQ, PROJ_WIDTH), jnp.bfloat16),
            jax.ShapeDtypeStruct((SEQ, 1), jnp.float32),
            jax.ShapeDtypeStruct((SEQ, 1), jnp.float32),
            jax.ShapeDtypeStruct((D_MODEL, D_MODEL), jnp.bfloat16),
        ],
        scratch_shapes=[pltpu.VMEM((IN_TM, D_MODEL), jnp.bfloat16),
                        pltpu.VMEM((IN_TM, D_MODEL), jnp.bfloat16)],
        compiler_params=pltpu.CompilerParams(
            dimension_semantics=("arbitrary", "arbitrary"),
            vmem_limit_bytes=V7X_VMEM_LIMIT_BYTES),
    )(h0, x2d, g, b, w_in, w_out)


class _MixerHalfBlock:
    N_CONV_CHUNKS = HALF_WIDTH // CONV_CHUNK
    N_KV_GROUPS = N_KV_HEADS // 2

    def __init__(self, blk, half, sink_ref, cb_ref, cc_ref, ch_ref, cz_ref,
                 ccp_ref, chp_ref, ccn_ref, chn_ref,
                 q_ref, kp_ref, kc_ref, kn_ref, vp_ref, vc_ref, vn_ref,
                 az_ref, cw_ref, store):
        self.blk, self.half, self.sink_ref, self.store = blk, half, sink_ref, store
        self.cb_ref, self.cc_ref, self.ch_ref, self.cz_ref = cb_ref, cc_ref, ch_ref, cz_ref
        self.ccp_ref, self.chp_ref, self.ccn_ref, self.chn_ref = ccp_ref, chp_ref, ccn_ref, chn_ref
        self.q_ref, self.az_ref, self.cw_ref = q_ref, az_ref, cw_ref
        self.k_refs = (kp_ref, kc_ref, kn_ref)
        self.v_refs = (vp_ref, vc_ref, vn_ref)
        self._mask = None

    def conv_chunk(self, c):
        f32 = jnp.float32
        cols = slice(c * CONV_CHUNK, (c + 1) * CONV_CHUNK)
        row = lax.broadcasted_iota(jnp.int32, (BLOCK, CONV_CHUNK), 0)
        u = self.cc_ref[:, cols].astype(f32) * self.ch_ref[:, cols].astype(f32)
        u_prev = jnp.where(self.blk > 0,
                           self.ccp_ref[HALO_ROWS - 1:HALO_ROWS, cols].astype(f32)
                           * self.chp_ref[HALO_ROWS - 1:HALO_ROWS, cols].astype(f32), 0.0)
        u_next = jnp.where(self.blk < N_BLOCKS - 1,
                           self.ccn_ref[0:1, cols].astype(f32)
                           * self.chn_ref[0:1, cols].astype(f32), 0.0)
        um1 = jnp.where(row == 0, u_prev, pltpu.roll(u, 1, axis=0))
        up1 = jnp.where(row == BLOCK - 1, u_next, pltpu.roll(u, BLOCK - 1, axis=0))
        cw_ref = self.cw_ref
        conv = um1 * cw_ref[0:1, cols] + u * cw_ref[1:2, cols] + up1 * cw_ref[2:3, cols]
        y = self.cb_ref[:, cols].astype(f32) * conv * _silu(self.cz_ref[:, cols].astype(f32))
        self.store(0, c * CONV_CHUNK, CONV_CHUNK, y.astype(jnp.bfloat16))

    def _band_mask(self):
        if self._mask is None:
            band = 3 * BLOCK
            qi = lax.broadcasted_iota(jnp.int32, (BLOCK, band), 0)
            kj = lax.broadcasted_iota(jnp.int32, (BLOCK, band), 1) - BLOCK
            dist_i = jnp.abs(qi - kj)
            k_pos = kj + self.blk * BLOCK
            valid = (dist_i <= WINDOW) & (k_pos >= 0) & (k_pos < SEQ)
            self._mask = (valid, dist_i.astype(jnp.float32) * -LOG2_E)
        return self._mask

    def scores(self, kg):
        kv_cols = slice(kg * HEAD_DIM, (kg + 1) * HEAD_DIM)
        q4 = jnp.concatenate(
            [self.q_ref[:, (kg * GQA_GROUP + g) * HEAD_DIM:(kg * GQA_GROUP + g + 1) * HEAD_DIM]
             for g in range(GQA_GROUP)], axis=0)
        kb = jnp.concatenate([r[:, kv_cols] for r in self.k_refs], axis=0)
        return lax.dot_general(q4, kb, (((1,), (1,)), ((), ())),
                               preferred_element_type=jnp.float32)

    def softmax_head(self, kg, g, s4):
        valid, neg_dist_log2 = self._band_mask()
        heads_per_half = N_Q_HEADS // 2
        half_slope = jnp.where(self.half == 1, 2.0 ** (-8.0 * heads_per_half / N_Q_HEADS),
                               1.0).astype(jnp.float32)
        local = kg * GQA_GROUP + g
        slope = half_slope * (2.0 ** (-8.0 * (local + 1) / N_Q_HEADS))
        sink = self.sink_ref[self.half * heads_per_half + local] * LOG2_E
        s = s4[g * BLOCK:(g + 1) * BLOCK, :] * (HEAD_DIM ** -0.5 * LOG2_E) + neg_dist_log2 * slope
        s = jnp.where(valid, s, NEG_INF)
        m = jnp.maximum(jnp.max(s, axis=-1, keepdims=True), sink)
        e = jnp.exp2(s - m)
        return e.astype(jnp.bfloat16), jnp.sum(e, axis=-1, keepdims=True) + jnp.exp2(sink - m)

    def weighted_values(self, kg, e_parts):
        kv_cols = slice(kg * HEAD_DIM, (kg + 1) * HEAD_DIM)
        vb = jnp.concatenate([r[:, kv_cols] for r in self.v_refs], axis=0)
        return jnp.dot(jnp.concatenate(e_parts, axis=0), vb,
                       preferred_element_type=jnp.float32)

    def finish(self, kg, o4, l_parts):
        for g in range(GQA_GROUP):
            local = kg * GQA_GROUP + g
            o = o4[g * BLOCK:(g + 1) * BLOCK, :] / l_parts[g]
            gate = _silu(self.az_ref[:, local * HEAD_DIM:(local + 1) * HEAD_DIM].astype(jnp.float32))
            self.store(CONV_WIDTH, local * HEAD_DIM, HEAD_DIM, (o * gate).astype(jnp.bfloat16))

    def run_all(self):
        for c in range(self.N_CONV_CHUNKS):
            self.conv_chunk(c)
        for kg in range(self.N_KV_GROUPS):
            s4 = self.scores(kg)
            heads = [self.softmax_head(kg, g, s4) for g in range(GQA_GROUP)]
            self.finish(kg, self.weighted_values(kg, [e for e, _ in heads]), [l for _, l in heads])


N_MIXER_REFS = 17


def _mixer_in_specs(blk_of, half_of):
    halo_per_block = BLOCK // HALO_ROWS
    n_halo = SEQ // HALO_ROWS

    def spec(shape, row_fn, col_base, col_width):
        def index_map(*idx):
            return (row_fn(blk_of(*idx)), col_base // col_width + half_of(*idx))
        return pl.BlockSpec(shape, index_map)

    cur = lambda b: b
    prev_blk = lambda b: jnp.maximum(b - 1, 0)
    next_blk = lambda b: jnp.minimum(b + 1, N_BLOCKS - 1)
    prev_halo = lambda b: jnp.maximum(b * halo_per_block - 1, 0)
    next_halo = lambda b: jnp.minimum((b + 1) * halo_per_block, n_halo - 1)

    wide = (BLOCK, HALF_WIDTH)
    halo = (HALO_ROWS, HALF_WIDTH)
    kv = (BLOCK, KV_WIDTH // 2)
    return [
        spec(wide, cur, OFF_CB, HALF_WIDTH),
        spec(wide, cur, OFF_CC, HALF_WIDTH),
        spec(wide, cur, OFF_CH, HALF_WIDTH),
        spec(wide, cur, OFF_CZ, HALF_WIDTH),
        spec(halo, prev_halo, OFF_CC, HALF_WIDTH),
        spec(halo, prev_halo, OFF_CH, HALF_WIDTH),
        spec(halo, next_halo, OFF_CC, HALF_WIDTH),
        spec(halo, next_halo, OFF_CH, HALF_WIDTH),
        spec(wide, cur, OFF_Q, HALF_WIDTH),
        spec(kv, prev_blk, OFF_K, KV_WIDTH // 2),
        spec(kv, cur, OFF_K, KV_WIDTH // 2),
        spec(kv, next_blk, OFF_K, KV_WIDTH // 2),
        spec(kv, prev_blk, OFF_V, KV_WIDTH // 2),
        spec(kv, cur, OFF_V, KV_WIDTH // 2),
        spec(kv, next_blk, OFF_V, KV_WIDTH // 2),
        spec(wide, cur, OFF_AZ, HALF_WIDTH),
        pl.BlockSpec((3, HALF_WIDTH), lambda *idx: (0, half_of(*idx))),
    ]


def _mixers_first_tile_kernel(sink_ref, *refs):
    mixer_refs, y_ref = refs[:N_MIXER_REFS], refs[N_MIXER_REFS]
    blk = pl.program_id(0)
    half = pl.program_id(1)

    def store(section, col, width, value):
        start = pl.multiple_of(half * HALF_WIDTH + (section + col), HEAD_DIM)
        y_ref[:, pl.ds(start, width)] = value

    _MixerHalfBlock(blk, half, sink_ref, *mixer_refs, store).run_all()


def _mixers_first_tile(p, conv_w, sink):
    in_specs = [pl.BlockSpec(memory_space=pltpu.SMEM)] + _mixer_in_specs(
        lambda b, hf: b, lambda b, hf: hf)
    return pl.pallas_call(
        _mixers_first_tile_kernel,
        name="mixers",
        grid=(OUT_TM // BLOCK, 2),
        in_specs=in_specs,
        out_specs=pl.BlockSpec((BLOCK, D_MODEL), lambda b, hf: (b, 0)),
        out_shape=jax.ShapeDtypeStruct((OUT_TM, D_MODEL), jnp.bfloat16),
        compiler_params=pltpu.CompilerParams(
            dimension_semantics=("arbitrary", "arbitrary"),
            vmem_limit_bytes=V7X_VMEM_LIMIT_BYTES),
    )(sink, *([p] * (N_MIXER_REFS - 1)), conv_w)


def _next_tile_block(i, n):
    return jnp.minimum((i + 1) * (OUT_TM // BLOCK) + n // 2, N_BLOCKS - 1)


def _out_proj_kernel(sink_ref, y0_ref, x_ref, mu_ref, rstd_ref, g0_ref, b0_ref, w_ref,
                     g1_ref, b1_ref, *refs):
    mixer_refs = refs[:N_MIXER_REFS]
    o_ref, z_ref, zmean_ref, zm2_ref, y_even_ref, y_odd_ref = refs[N_MIXER_REFS:]
    i = pl.program_id(0)
    n = pl.program_id(1)

    @pl.when((i == 0) & (n == 0))
    def _():
        y_even_ref[...] = y0_ref[...]

    @pl.when(n == 0)
    def _():
        zmean_ref[...] = jnp.zeros_like(zmean_ref)
        zm2_ref[...] = jnp.zeros_like(zm2_ref)

    def step(cur_ref, nxt_ref):
        half = n % 2
        row0 = pl.multiple_of((n // 2) * BLOCK, BLOCK)

        def store(section, col, width, value):
            start = pl.multiple_of(half * HALF_WIDTH + (section + col), HEAD_DIM)
            nxt_ref[pl.ds(row0, BLOCK), pl.ds(start, width)] = value

        mx = _MixerHalfBlock(_next_tile_block(i, n), half, sink_ref, *mixer_refs, store)

        def dot_piece(k):
            ks = slice(k * OUT_K_PIECE, (k + 1) * OUT_K_PIECE)
            return jnp.dot(cur_ref[:, ks], w_ref[ks, :], preferred_element_type=jnp.float32)

        band_scores = [mx.scores(kg) for kg in range(mx.N_KV_GROUPS)]
        heads = []
        y = None
        for k in range(OUT_N_PIECES):
            piece = dot_piece(k)
            y = piece if y is None else y + piece
            kg, g = divmod(k, GQA_GROUP)
            heads.append(mx.softmax_head(kg, g, band_scores[kg]))
            mx.conv_chunk(k)
        outs = [mx.weighted_values(kg, [e for e, _ in heads[kg * GQA_GROUP:(kg + 1) * GQA_GROUP]])
                for kg in range(mx.N_KV_GROUPS)]

        ag0 = DN_ALPHA * g0_ref[...]
        ab0 = DN_ALPHA * b0_ref[...]
        own_lane = lax.broadcasted_iota(jnp.int32, (OUT_ROW_CHUNK, STAT_LANES), 1) == n
        for c in range(OUT_TM // OUT_ROW_CHUNK):
            rows = slice(c * OUT_ROW_CHUNK, (c + 1) * OUT_ROW_CHUNK)
            z = ((x_ref[rows, :] - mu_ref[rows, :]) * rstd_ref[rows, :] * ag0 + ab0) + y[rows, :]
            z_ref[n, rows, :] = z
            zm = jnp.mean(z, axis=-1, keepdims=True)
            zc = z - zm
            zmean_ref[rows, :] = jnp.where(own_lane, zm, zmean_ref[rows, :])
            zm2_ref[rows, :] = jnp.where(own_lane, jnp.sum(zc * zc, axis=-1, keepdims=True),
                                         zm2_ref[rows, :])

        for kg in range(mx.N_KV_GROUPS):
            mx.finish(kg, outs[kg], [l for _, l in heads[kg * GQA_GROUP:(kg + 1) * GQA_GROUP]])

    @pl.when(i % 2 == 0)
    def _():
        step(y_even_ref, y_odd_ref)

    @pl.when(i % 2 == 1)
    def _():
        step(y_odd_ref, y_even_ref)

    @pl.when(n == OUT_NT - 1)
    def _():
        used_lane = lax.broadcasted_iota(jnp.int32, (OUT_TM, STAT_LANES), 1) < OUT_NT
        means = zmean_ref[...]
        row_mean = jnp.sum(means, axis=-1, keepdims=True) * (1.0 / OUT_NT)
        dm = jnp.where(used_lane, means - row_mean, 0.0)
        m2 = jnp.sum(zm2_ref[...] + OUT_TN * (dm * dm), axis=-1, keepdims=True)
        zmean_ref[:, 0:1] = row_mean
        zm2_ref[:, 0:1] = lax.rsqrt(m2 * (1.0 / D_MODEL) + LN_EPS)

        def body(c, carry):
            rows = pl.ds(pl.multiple_of(c * OUT_ROW_CHUNK, OUT_ROW_CHUNK), OUT_ROW_CHUNK)
            mean = zmean_ref[rows, 0:1]
            rstd = zm2_ref[rows, 0:1]
            for k in range(OUT_NT):
                cols = slice(k * OUT_TN, (k + 1) * OUT_TN)
                o_ref[rows, cols] = (z_ref[k, rows, :] - mean) * rstd * g1_ref[:, cols] + b1_ref[:, cols]
            return carry

        lax.fori_loop(0, OUT_TM // OUT_ROW_CHUNK, body, 0)


def _out_proj(x2d, mu, rstd, g0, b0, ymix0, w, g1, b1, p, conv_w, sink):
    assert OUT_NT == 2 * (OUT_TM // BLOCK)
    grid = (SEQ // OUT_TM, OUT_NT)
    row_stat = pl.BlockSpec((OUT_TM, 1), lambda i, n: (i, 0))
    col_vec = pl.BlockSpec((1, OUT_TN), lambda i, n: (0, n))
    full_vec = pl.BlockSpec((1, D_MODEL), lambda i, n: (0, 0))
    in_specs = [
        pl.BlockSpec(memory_space=pltpu.SMEM),
        pl.BlockSpec((OUT_TM, D_MODEL), lambda i, n: (0, 0),
                     pipeline_mode=pl.Buffered(1)),
        pl.BlockSpec((OUT_TM, OUT_TN), lambda i, n: (i, n)),
        row_stat, row_stat,
        col_vec, col_vec,
        pl.BlockSpec((D_MODEL, OUT_TN), lambda i, n: (0, n)),
        full_vec, full_vec,
    ] + _mixer_in_specs(_next_tile_block, lambda i, n: n % 2)
    return pl.pallas_call(
        _out_proj_kernel,
        name="out_proj",
        grid=grid,
        in_specs=in_specs,
        out_specs=pl.BlockSpec((OUT_TM, D_MODEL), lambda i, n: (i, 0)),
        out_shape=jax.ShapeDtypeStruct((SEQ, D_MODEL), jnp.float32),
        scratch_shapes=[
            pltpu.VMEM((OUT_NT, OUT_TM, OUT_TN), jnp.float32),
            pltpu.VMEM((OUT_TM, STAT_LANES), jnp.float32),
            pltpu.VMEM((OUT_TM, STAT_LANES), jnp.float32),
            pltpu.VMEM((OUT_TM, D_MODEL), jnp.bfloat16),
            pltpu.VMEM((OUT_TM, D_MODEL), jnp.bfloat16),
        ],
        compiler_params=pltpu.CompilerParams(
            dimension_semantics=("arbitrary", "arbitrary"),
            vmem_limit_bytes=V7X_VMEM_LIMIT_BYTES),
    )(sink, ymix0, x2d, mu, rstd, g0, b0, w, g1, b1, *([p] * (N_MIXER_REFS - 1)), conv_w)


def kernel(x, emb_ln_g, emb_ln_b, w_in, conv_w, sink, w_out, ln_g, ln_b):
    batch, seq, d_model = x.shape
    assert (batch, seq, d_model) == (1, SEQ, D_MODEL)
    assert w_in.shape == (1, D_MODEL, PROJ_WIDTH) and w_out.shape == (1, D_MODEL, D_MODEL)
    x2d = x.reshape(SEQ, D_MODEL)
    g0 = emb_ln_g.reshape(1, D_MODEL)
    b0 = emb_ln_b.reshape(1, D_MODEL)
    g1 = ln_g.reshape(1, D_MODEL)
    b1 = ln_b.reshape(1, D_MODEL)
    h0 = _entry_norm_first_tile(x2d, g0, b0)
    p, mu, rstd, w_out_bf16 = _in_proj(h0, x2d, g0, b0, w_in[0], w_out[0])
    ymix0 = _mixers_first_tile(p, conv_w[0], sink[0])
    out = _out_proj(x2d, mu, rstd, g0, b0, ymix0, w_out_bf16, g1, b1, p, conv_w[0], sink[0])
    return out.reshape(1, SEQ, D_MODEL)
```

```python
import jax
import jax.numpy as jnp
from jax import lax
from jax.experimental import pallas as pl
from jax.experimental.pallas import tpu as pltpu

D_MODEL = 4096
SEQ = 8192
CONV_WIDTH = D_MODEL // 2
ATTN_WIDTH = D_MODEL - CONV_WIDTH
HEAD_DIM = 128
N_Q_HEADS = ATTN_WIDTH // HEAD_DIM
N_KV_HEADS = N_Q_HEADS // 4
GQA_GROUP = N_Q_HEADS // N_KV_HEADS
KV_WIDTH = N_KV_HEADS * HEAD_DIM
WINDOW = 128
BLOCK = 128
N_BLOCKS = SEQ // BLOCK
DN_ALPHA = 2.0 ** 0.25
LN_EPS = 1e-5
NEG_INF = -1e30
LOG2_E = 1.4426950408889634
PROJ_WIDTH = 4 * CONV_WIDTH + ATTN_WIDTH + 2 * KV_WIDTH + ATTN_WIDTH

OFF_CB = 0
OFF_CC = OFF_CB + CONV_WIDTH
OFF_CH = OFF_CC + CONV_WIDTH
OFF_CZ = OFF_CH + CONV_WIDTH
OFF_Q = OFF_CZ + CONV_WIDTH
OFF_K = OFF_Q + ATTN_WIDTH
OFF_V = OFF_K + KV_WIDTH
OFF_AZ = OFF_V + KV_WIDTH

V7X_VMEM_LIMIT_BYTES = 58 * 1024 * 1024
SUBLANES = 8

NORM_TM = 256
NORM_UNROLL = 8
BF16_ROWS = 16
IN_TM, IN_TN = 1024, 512
IN_LN_STEPS = 16
IN_LN_ROWS = IN_TM // IN_LN_STEPS
WOUT_CAST_ROWS = 32
OUT_TM, OUT_TN = 512, 512
OUT_NT = D_MODEL // OUT_TN
OUT_ROW_CHUNK = 64
OUT_N_PIECES = 8
OUT_K_PIECE = D_MODEL // OUT_N_PIECES
STAT_LANES = 128
assert OUT_NT <= STAT_LANES
HALO_ROWS = 16
CONV_CHUNK = 128
HALF_WIDTH = CONV_WIDTH // 2
assert HALF_WIDTH == ATTN_WIDTH // 2
P_SECTIONS = ("cb", "cc", "ch", "cz", "q", "az")
P_SLAB_WIDTH = len(P_SECTIONS) * HALF_WIDTH
P_KV_START = 2 * P_SLAB_WIDTH
assert P_KV_START + 2 * KV_WIDTH == PROJ_WIDTH
assert HALF_WIDTH // CONV_CHUNK == OUT_N_PIECES == N_Q_HEADS // 2


def _silu(z):
    return z / (1.0 + jnp.exp(-z))


def _entry_norm_kernel(x_ref, g_ref, b_ref, h_ref, mu_ref, rstd_ref):
    n_groups = NORM_TM // SUBLANES

    def group(r):
        return pl.ds(pl.multiple_of(r * SUBLANES, SUBLANES), SUBLANES)

    def mean_body(r, carry):
        rows = group(r)
        mu_ref[rows, :] = jnp.mean(x_ref[rows, :], axis=-1, keepdims=True)
        return carry

    def var_body(r, carry):
        rows = group(r)
        xc = x_ref[rows, :] - mu_ref[rows, :]
        var = jnp.mean(xc * xc, axis=-1, keepdims=True)
        rstd_ref[rows, :] = lax.rsqrt(var + LN_EPS)
        return carry

    def scale_body(r, carry):
        rows = group(r)
        y = (x_ref[rows, :] - mu_ref[rows, :]) * rstd_ref[rows, :] * g_ref[...] + b_ref[...]
        h_ref[rows, :] = y.astype(jnp.bfloat16)
        return carry

    lax.fori_loop(0, n_groups, mean_body, 0, unroll=NORM_UNROLL)
    lax.fori_loop(0, n_groups, var_body, 0, unroll=NORM_UNROLL)
    lax.fori_loop(0, n_groups, scale_body, 0, unroll=NORM_UNROLL)


def _entry_norm_first_tile(x2d, g, b):
    vec = pl.BlockSpec((1, D_MODEL), lambda i: (0, 0))
    return pl.pallas_call(
        _entry_norm_kernel,
        name="entry_norm",
        grid=(IN_TM // NORM_TM,),
        in_specs=[pl.BlockSpec((NORM_TM, D_MODEL), lambda i: (i, 0)), vec, vec],
        out_specs=pl.BlockSpec((NORM_TM, D_MODEL), lambda i: (i, 0)),
        out_shape=jax.ShapeDtypeStruct((IN_TM, D_MODEL), jnp.bfloat16),
        scratch_shapes=[pltpu.VMEM((NORM_TM, 1), jnp.float32),
                        pltpu.VMEM((NORM_TM, 1), jnp.float32)],
        compiler_params=pltpu.CompilerParams(
            dimension_semantics=("arbitrary",),
            vmem_limit_bytes=V7X_VMEM_LIMIT_BYTES),
    )(x2d, g, b)


def _in_proj_kernel(h0_ref, x_ref, g_ref, b_ref, w_ref, wo_ref,
                    p_ref, mu_ref, rstd_ref, wob_ref, h_even_ref, h_odd_ref):
    i = pl.program_id(0)
    j = pl.program_id(1)

    @pl.when((i == 0) & (j == 0))
    def _():
        h_even_ref[...] = h0_ref[...]

    def step(cur_ref, nxt_ref):
        p_ref[...] = jnp.dot(cur_ref[...], w_ref[...].astype(jnp.bfloat16),
                             preferred_element_type=jnp.float32).astype(jnp.bfloat16)
        chunk = jnp.minimum(j, IN_LN_STEPS - 1)
        base = pl.multiple_of(chunk * IN_LN_ROWS, IN_LN_ROWS)
        g = g_ref[...]
        b = b_ref[...]
        for r in range(IN_LN_ROWS // BF16_ROWS):
            rows = slice(r * BF16_ROWS, (r + 1) * BF16_ROWS)
            x = x_ref[rows, :]
            mu = jnp.mean(x, axis=-1, keepdims=True)
            xc = x - mu
            rstd = lax.rsqrt(jnp.mean(xc * xc, axis=-1, keepdims=True) + LN_EPS)
            mu_ref[rows, :] = mu
            rstd_ref[rows, :] = rstd
            nxt_ref[pl.ds(base + r * BF16_ROWS, BF16_ROWS), :] = (
                (xc * rstd * g + b).astype(jnp.bfloat16))
        wob_ref[...] = wo_ref[...].astype(jnp.bfloat16)

    @pl.when(i % 2 == 0)
    def _():
        step(h_even_ref, h_odd_ref)

    @pl.when(i % 2 == 1)
    def _():
        step(h_odd_ref, h_even_ref)


def _in_proj(h0, x2d, g, b, w_in, w_out):
    n_i, n_j = SEQ // IN_TM, PROJ_WIDTH // IN_TN
    n_chunks = SEQ // IN_LN_ROWS
    n_slabs = D_MODEL // WOUT_CAST_ROWS
    assert n_i * n_j >= n_slabs and n_j >= IN_LN_STEPS

    def next_tile_chunk(i, j):
        return (lax.rem(i + 1, n_i) * IN_LN_STEPS + jnp.minimum(j, IN_LN_STEPS - 1), 0)

    def slab(i, j):
        return (jnp.minimum(i * n_j + j, n_slabs - 1), 0)

    def stored_tile(i, j):
        wide_tiles = CONV_WIDTH // IN_TN
        half_tiles = HALF_WIDTH // IN_TN
        k_tile, v_tile, az_tile = OFF_K // IN_TN, OFF_V // IN_TN, OFF_AZ // IN_TN
        is_az = j >= az_tile
        section = jnp.where(is_az, len(P_SECTIONS) - 1, j // wide_tiles)
        t = jnp.where(is_az, j - az_tile, j % wide_tiles)
        place = ((t // half_tiles) * (P_SLAB_WIDTH // IN_TN) + section * half_tiles
                 + t % half_tiles)
        place = jnp.where(j == k_tile, P_KV_START // IN_TN, place)
        place = jnp.where(j == v_tile, P_KV_START // IN_TN + KV_WIDTH // IN_TN, place)
        return (i, place)

    vec = pl.BlockSpec((1, D_MODEL), lambda i, j: (0, 0))
    return pl.pallas_call(
        _in_proj_kernel,
        name="in_proj",
        grid=(n_i, n_j),
        in_specs=[
            pl.BlockSpec((IN_TM, D_MODEL), lambda i, j: (0, 0), pipeline_mode=pl.Buffered(1)),
            pl.BlockSpec((IN_LN_ROWS, D_MODEL), next_tile_chunk),
            vec, vec,
            pl.BlockSpec((D_MODEL, IN_TN), lambda i, j: (0, j)),
            pl.BlockSpec((WOUT_CAST_ROWS, D_MODEL), slab),
        ],
        out_specs=[
            pl.BlockSpec((IN_TM, IN_TN), stored_tile),
            pl.BlockSpec((IN_LN_ROWS, 1), next_tile_chunk),
            pl.BlockSpec((IN_LN_ROWS, 1), next_tile_chunk),
            pl.BlockSpec((WOUT_CAST_ROWS, D_MODEL), slab),
        ],
        out_shape=[
            jax.ShapeDtypeStruct((SEQ, PROJ_WIDTH), jnp.bfloat16),
            jax.ShapeDtypeStruct((SEQ, 1), jnp.float32),
            jax.ShapeDtypeStruct((SEQ, 1), jnp.float32),
            jax.ShapeDtypeStruct((D_MODEL, D_MODEL), jnp.bfloat16),
        ],
        scratch_shapes=[pltpu.VMEM((IN_TM, D_MODEL), jnp.bfloat16),
                        pltpu.VMEM((IN_TM, D_MODEL), jnp.bfloat16)],
        compiler_params=pltpu.CompilerParams(
            dimension_semantics=("arbitrary", "arbitrary"),
            vmem_limit_bytes=V7X_VMEM_LIMIT_BYTES),
    )(h0, x2d, g, b, w_in, w_out)


class _MixerHalfBlock:
    N_CONV_CHUNKS = HALF_WIDTH // CONV_CHUNK
    N_KV_GROUPS = N_KV_HEADS // 2

    def __init__(self, blk, half, sink_ref, main_ref, halo_prev_ref, halo_next_ref,
                 kv_prev_ref, kv_cur_ref, kv_next_ref, cw_ref, store):
        self.blk, self.half, self.sink_ref, self.store = blk, half, sink_ref, store

        def section(ref, s):
            return ref.at[:, s * HALF_WIDTH:(s + 1) * HALF_WIDTH]

        self.cb_ref, self.cc_ref, self.ch_ref, self.cz_ref, self.q_ref, self.az_ref = (
            section(main_ref, s) for s in range(len(P_SECTIONS)))
        self.ccp_ref, self.chp_ref = section(halo_prev_ref, 1), section(halo_prev_ref, 2)
        self.ccn_ref, self.chn_ref = section(halo_next_ref, 1), section(halo_next_ref, 2)
        self.cw_ref = cw_ref
        half_kv = KV_WIDTH // 2
        k0 = pl.multiple_of(half * half_kv, half_kv)
        v0 = pl.multiple_of(KV_WIDTH + half * half_kv, half_kv)
        kv_refs = (kv_prev_ref, kv_cur_ref, kv_next_ref)
        self.k_refs = tuple(r.at[:, pl.ds(k0, half_kv)] for r in kv_refs)
        self.v_refs = tuple(r.at[:, pl.ds(v0, half_kv)] for r in kv_refs)
        self._mask = None

    def conv_chunk(self, c):
        f32 = jnp.float32
        cols = slice(c * CONV_CHUNK, (c + 1) * CONV_CHUNK)
        row = lax.broadcasted_iota(jnp.int32, (BLOCK, CONV_CHUNK), 0)
        u = self.cc_ref[:, cols].astype(f32) * self.ch_ref[:, cols].astype(f32)
        u_prev = jnp.where(self.blk > 0,
                           self.ccp_ref[HALO_ROWS - 1:HALO_ROWS, cols].astype(f32)
                           * self.chp_ref[HALO_ROWS - 1:HALO_ROWS, cols].astype(f32), 0.0)
        u_next = jnp.where(self.blk < N_BLOCKS - 1,
                           self.ccn_ref[0:1, cols].astype(f32)
                           * self.chn_ref[0:1, cols].astype(f32), 0.0)
        um1 = jnp.where(row == 0, u_prev, pltpu.roll(u, 1, axis=0))
        up1 = jnp.where(row == BLOCK - 1, u_next, pltpu.roll(u, BLOCK - 1, axis=0))
        cw_ref = self.cw_ref
        conv = um1 * cw_ref[0:1, cols] + u * cw_ref[1:2, cols] + up1 * cw_ref[2:3, cols]
        y = self.cb_ref[:, cols].astype(f32) * conv * _silu(self.cz_ref[:, cols].astype(f32))
        self.store(0, c * CONV_CHUNK, CONV_CHUNK, y.astype(jnp.bfloat16))

    def _band_mask(self):
        if self._mask is None:
            band = 3 * BLOCK
            qi = lax.broadcasted_iota(jnp.int32, (BLOCK, band), 0)
            kj = lax.broadcasted_iota(jnp.int32, (BLOCK, band), 1) - BLOCK
            dist_i = jnp.abs(qi - kj)
            k_pos = kj + self.blk * BLOCK
            valid = (dist_i <= WINDOW) & (k_pos >= 0) & (k_pos < SEQ)
            self._mask = (valid, dist_i.astype(jnp.float32) * -LOG2_E)
        return self._mask

    def scores(self, kg):
        kv_cols = slice(kg * HEAD_DIM, (kg + 1) * HEAD_DIM)
        q4 = jnp.concatenate(
            [self.q_ref[:, (kg * GQA_GROUP + g) * HEAD_DIM:(kg * GQA_GROUP + g + 1) * HEAD_DIM]
             for g in range(GQA_GROUP)], axis=0)
        kb = jnp.concatenate([r[:, kv_cols] for r in self.k_refs], axis=0)
        return lax.dot_general(q4, kb, (((1,), (1,)), ((), ())),
                               preferred_element_type=jnp.float32)

    def softmax_head(self, kg, g, s4):
        valid, neg_dist_log2 = self._band_mask()
        heads_per_half = N_Q_HEADS // 2
        half_slope = jnp.where(self.half == 1, 2.0 ** (-8.0 * heads_per_half / N_Q_HEADS),
                               1.0).astype(jnp.float32)
        local = kg * GQA_GROUP + g
        slope = half_slope * (2.0 ** (-8.0 * (local + 1) / N_Q_HEADS))
        sink = self.sink_ref[self.half * heads_per_half + local] * LOG2_E
        s = s4[g * BLOCK:(g + 1) * BLOCK, :] * (HEAD_DIM ** -0.5 * LOG2_E) + neg_dist_log2 * slope
        s = jnp.where(valid, s, NEG_INF)
        m = jnp.maximum(jnp.max(s, axis=-1, keepdims=True), sink)
        e = jnp.exp2(s - m)
        return e.astype(jnp.bfloat16), jnp.sum(e, axis=-1, keepdims=True) + jnp.exp2(sink - m)

    def weighted_values(self, kg, e_parts):
        kv_cols = slice(kg * HEAD_DIM, (kg + 1) * HEAD_DIM)
        vb = jnp.concatenate([r[:, kv_cols] for r in self.v_refs], axis=0)
        return jnp.dot(jnp.concatenate(e_parts, axis=0), vb,
                       preferred_element_type=jnp.float32)

    def finish(self, kg, o4, l_parts):
        for g in range(GQA_GROUP):
            local = kg * GQA_GROUP + g
            o = o4[g * BLOCK:(g + 1) * BLOCK, :] / l_parts[g]
            gate = _silu(self.az_ref[:, local * HEAD_DIM:(local + 1) * HEAD_DIM].astype(jnp.float32))
            self.store(CONV_WIDTH, local * HEAD_DIM, HEAD_DIM, (o * gate).astype(jnp.bfloat16))

    def run_all(self):
        for c in range(self.N_CONV_CHUNKS):
            self.conv_chunk(c)
        for kg in range(self.N_KV_GROUPS):
            s4 = self.scores(kg)
            heads = [self.softmax_head(kg, g, s4) for g in range(GQA_GROUP)]
            self.finish(kg, self.weighted_values(kg, [e for e, _ in heads]), [l for _, l in heads])


N_MIXER_REFS = 7


def _mixer_in_specs(blk_of, half_of):
    halo_per_block = BLOCK // HALO_ROWS
    n_halo = SEQ // HALO_ROWS

    def slab(rows, row_fn):
        return pl.BlockSpec((rows, P_SLAB_WIDTH),
                            lambda *idx: (row_fn(blk_of(*idx)), half_of(*idx)))

    def kv(row_fn):
        return pl.BlockSpec((BLOCK, 2 * KV_WIDTH),
                            lambda *idx: (row_fn(blk_of(*idx)), P_KV_START // (2 * KV_WIDTH)))

    cur = lambda b: b
    prev_blk = lambda b: jnp.maximum(b - 1, 0)
    next_blk = lambda b: jnp.minimum(b + 1, N_BLOCKS - 1)
    prev_halo = lambda b: jnp.maximum(b * halo_per_block - 1, 0)
    next_halo = lambda b: jnp.minimum((b + 1) * halo_per_block, n_halo - 1)
    return [
        slab(BLOCK, cur),
        slab(HALO_ROWS, prev_halo),
        slab(HALO_ROWS, next_halo),
        kv(prev_blk), kv(cur), kv(next_blk),
        pl.BlockSpec((3, HALF_WIDTH), lambda *idx: (0, half_of(*idx))),
    ]


def _mixers_first_tile_kernel(sink_ref, *refs):
    mixer_refs, y_ref = refs[:N_MIXER_REFS], refs[N_MIXER_REFS]
    blk = pl.program_id(0)
    half = pl.program_id(1)

    def store(section, col, width, value):
        start = pl.multiple_of(half * HALF_WIDTH + (section + col), HEAD_DIM)
        y_ref[:, pl.ds(start, width)] = value

    _MixerHalfBlock(blk, half, sink_ref, *mixer_refs, store).run_all()


def _mixers_first_tile(p, conv_w, sink):
    in_specs = [pl.BlockSpec(memory_space=pltpu.SMEM)] + _mixer_in_specs(
        lambda b, hf: b, lambda b, hf: hf)
    return pl.pallas_call(
        _mixers_first_tile_kernel,
        name="mixers",
        grid=(OUT_TM // BLOCK, 2),
        in_specs=in_specs,
        out_specs=pl.BlockSpec((BLOCK, D_MODEL), lambda b, hf: (b, 0)),
        out_shape=jax.ShapeDtypeStruct((OUT_TM, D_MODEL), jnp.bfloat16),
        compiler_params=pltpu.CompilerParams(
            dimension_semantics=("arbitrary", "arbitrary"),
            vmem_limit_bytes=V7X_VMEM_LIMIT_BYTES),
    )(sink, *([p] * (N_MIXER_REFS - 1)), conv_w)


def _next_tile_block(i, n):
    return jnp.minimum((i + 1) * (OUT_TM // BLOCK) + n // 2, N_BLOCKS - 1)


def _out_proj_kernel(sink_ref, y0_ref, x_ref, mu_ref, rstd_ref, g0_ref, b0_ref, w_ref,
                     g1_ref, b1_ref, *refs):
    mixer_refs = refs[:N_MIXER_REFS]
    o_ref, z_ref, zmean_ref, zm2_ref, y_even_ref, y_odd_ref = refs[N_MIXER_REFS:]
    i = pl.program_id(0)
    n = pl.program_id(1)

    @pl.when((i == 0) & (n == 0))
    def _():
        y_even_ref[...] = y0_ref[...]

    @pl.when(n == 0)
    def _():
        zmean_ref[...] = jnp.zeros_like(zmean_ref)
        zm2_ref[...] = jnp.zeros_like(zm2_ref)

    def step(cur_ref, nxt_ref):
        half = n % 2
        row0 = pl.multiple_of((n // 2) * BLOCK, BLOCK)

        def store(section, col, width, value):
            start = pl.multiple_of(half * HALF_WIDTH + (section + col), HEAD_DIM)
            nxt_ref[pl.ds(row0, BLOCK), pl.ds(start, width)] = value

        mx = _MixerHalfBlock(_next_tile_block(i, n), half, sink_ref, *mixer_refs, store)

        def dot_piece(k):
            ks = slice(k * OUT_K_PIECE, (k + 1) * OUT_K_PIECE)
            return jnp.dot(cur_ref[:, ks], w_ref[ks, :], preferred_element_type=jnp.float32)

        band_scores = [mx.scores(kg) for kg in range(mx.N_KV_GROUPS)]
        heads = []
        y = None
        for k in range(OUT_N_PIECES):
            piece = dot_piece(k)
            y = piece if y is None else y + piece
            kg, g = divmod(k, GQA_GROUP)
            heads.append(mx.softmax_head(kg, g, band_scores[kg]))
            mx.conv_chunk(k)
        outs = [mx.weighted_values(kg, [e for e, _ in heads[kg * GQA_GROUP:(kg + 1) * GQA_GROUP]])
                for kg in range(mx.N_KV_GROUPS)]

        ag0 = DN_ALPHA * g0_ref[...]
        ab0 = DN_ALPHA * b0_ref[...]
        own_lane = lax.broadcasted_iota(jnp.int32, (OUT_ROW_CHUNK, STAT_LANES), 1) == n
        for c in range(OUT_TM // OUT_ROW_CHUNK):
            rows = slice(c * OUT_ROW_CHUNK, (c + 1) * OUT_ROW_CHUNK)
            z = ((x_ref[rows, :] - mu_ref[rows, :]) * rstd_ref[rows, :] * ag0 + ab0) + y[rows, :]
            z_ref[n, rows, :] = z
            zm = jnp.mean(z, axis=-1, keepdims=True)
            zc = z - zm
            zmean_ref[rows, :] = jnp.where(own_lane, zm, zmean_ref[rows, :])
            zm2_ref[rows, :] = jnp.where(own_lane, jnp.sum(zc * zc, axis=-1, keepdims=True),
                                         zm2_ref[rows, :])

        for kg in range(mx.N_KV_GROUPS):
            mx.finish(kg, outs[kg], [l for _, l in heads[kg * GQA_GROUP:(kg + 1) * GQA_GROUP]])

    @pl.when(i % 2 == 0)
    def _():
        step(y_even_ref, y_odd_ref)

    @pl.when(i % 2 == 1)
    def _():
        step(y_odd_ref, y_even_ref)

    @pl.when(n == OUT_NT - 1)
    def _():
        used_lane = lax.broadcasted_iota(jnp.int32, (OUT_TM, STAT_LANES), 1) < OUT_NT
        means = zmean_ref[...]
        row_mean = jnp.sum(means, axis=-1, keepdims=True) * (1.0 / OUT_NT)
        dm = jnp.where(used_lane, means - row_mean, 0.0)
        m2 = jnp.sum(zm2_ref[...] + OUT_TN * (dm * dm), axis=-1, keepdims=True)
        zmean_ref[:, 0:1] = row_mean
        zm2_ref[:, 0:1] = lax.rsqrt(m2 * (1.0 / D_MODEL) + LN_EPS)

        def body(c, carry):
            rows = pl.ds(pl.multiple_of(c * OUT_ROW_CHUNK, OUT_ROW_CHUNK), OUT_ROW_CHUNK)
            mean = zmean_ref[rows, 0:1]
            rstd = zm2_ref[rows, 0:1]
            for k in range(OUT_NT):
                cols = slice(k * OUT_TN, (k + 1) * OUT_TN)
                o_ref[rows, cols] = (z_ref[k, rows, :] - mean) * rstd * g1_ref[:, cols] + b1_ref[:, cols]
            return carry

        lax.fori_loop(0, OUT_TM // OUT_ROW_CHUNK, body, 0)


def _out_proj(x2d, mu, rstd, g0, b0, ymix0, w, g1, b1, p, conv_w, sink):
    assert OUT_NT == 2 * (OUT_TM // BLOCK)
    grid = (SEQ // OUT_TM, OUT_NT)
    row_stat = pl.BlockSpec((OUT_TM, 1), lambda i, n: (i, 0))
    col_vec = pl.BlockSpec((1, OUT_TN), lambda i, n: (0, n))
    full_vec = pl.BlockSpec((1, D_MODEL), lambda i, n: (0, 0))
    in_specs = [
        pl.BlockSpec(memory_space=pltpu.SMEM),
        pl.BlockSpec((OUT_TM, D_MODEL), lambda i, n: (0, 0),
                     pipeline_mode=pl.Buffered(1)),
        pl.BlockSpec((OUT_TM, OUT_TN), lambda i, n: (i, n)),
        row_stat, row_stat,
        col_vec, col_vec,
        pl.BlockSpec((D_MODEL, OUT_TN), lambda i, n: (0, n)),
        full_vec, full_vec,
    ] + _mixer_in_specs(_next_tile_block, lambda i, n: n % 2)
    return pl.pallas_call(
        _out_proj_kernel,
        name="out_proj",
        grid=grid,
        in_specs=in_specs,
        out_specs=pl.BlockSpec((OUT_TM, D_MODEL), lambda i, n: (i, 0)),
        out_shape=jax.ShapeDtypeStruct((SEQ, D_MODEL), jnp.float32),
        scratch_shapes=[
            pltpu.VMEM((OUT_NT, OUT_TM, OUT_TN), jnp.float32),
            pltpu.VMEM((OUT_TM, STAT_LANES), jnp.float32),
            pltpu.VMEM((OUT_TM, STAT_LANES), jnp.float32),
            pltpu.VMEM((OUT_TM, D_MODEL), jnp.bfloat16),
            pltpu.VMEM((OUT_TM, D_MODEL), jnp.bfloat16),
        ],
        compiler_params=pltpu.CompilerParams(
            dimension_semantics=("arbitrary", "arbitrary"),
            vmem_limit_bytes=V7X_VMEM_LIMIT_BYTES),
    )(sink, ymix0, x2d, mu, rstd, g0, b0, w, g1, b1, *([p] * (N_MIXER_REFS - 1)), conv_w)


def kernel(x, emb_ln_g, emb_ln_b, w_in, conv_w, sink, w_out, ln_g, ln_b):
    batch, seq, d_model = x.shape
    assert (batch, seq, d_model) == (1, SEQ, D_MODEL)
    assert w_in.shape == (1, D_MODEL, PROJ_WIDTH) and w_out.shape == (1, D_MODEL, D_MODEL)
    x2d = x.reshape(SEQ, D_MODEL)
    g0 = emb_ln_g.reshape(1, D_MODEL)
    b0 = emb_ln_b.reshape(1, D_MODEL)
    g1 = ln_g.reshape(1, D_MODEL)
    b1 = ln_b.reshape(1, D_MODEL)
    h0 = _entry_norm_first_tile(x2d, g0, b0)
    p, mu, rstd, w_out_bf16 = _in_proj(h0, x2d, g0, b0, w_in[0], w_out[0])
    ymix0 = _mixers_first_tile(p, conv_w[0], sink[0])
    out = _out_proj(x2d, mu, rstd, g0, b0, ymix0, w_out_bf16, g1, b1, p, conv_w[0], sink[0])
    return out.reshape(1, SEQ, D_MODEL)
```

```python
import jax
import jax.numpy as jnp
from jax import lax
from jax.experimental import pallas as pl
from jax.experimental.pallas import tpu as pltpu

D_MODEL = 4096
SEQ = 8192
CONV_WIDTH = D_MODEL // 2
ATTN_WIDTH = D_MODEL - CONV_WIDTH
HEAD_DIM = 128
N_Q_HEADS = ATTN_WIDTH // HEAD_DIM
N_KV_HEADS = N_Q_HEADS // 4
GQA_GROUP = N_Q_HEADS // N_KV_HEADS
KV_WIDTH = N_KV_HEADS * HEAD_DIM
WINDOW = 128
BLOCK = 128
N_BLOCKS = SEQ // BLOCK
DN_ALPHA = 2.0 ** 0.25
LN_EPS = 1e-5
NEG_INF = -1e30
LOG2_E = 1.4426950408889634
PROJ_WIDTH = 4 * CONV_WIDTH + ATTN_WIDTH + 2 * KV_WIDTH + ATTN_WIDTH

OFF_CB = 0
OFF_CC = OFF_CB + CONV_WIDTH
OFF_CH = OFF_CC + CONV_WIDTH
OFF_CZ = OFF_CH + CONV_WIDTH
OFF_Q = OFF_CZ + CONV_WIDTH
OFF_K = OFF_Q + ATTN_WIDTH
OFF_V = OFF_K + KV_WIDTH
OFF_AZ = OFF_V + KV_WIDTH

V7X_VMEM_LIMIT_BYTES = 58 * 1024 * 1024
SUBLANES = 8

NORM_TM = 256
NORM_UNROLL = 8
BF16_ROWS = 16
IN_TM, IN_TN = 1024, 512
IN_LN_STEPS = 16
IN_LN_ROWS = IN_TM // IN_LN_STEPS
WOUT_CAST_ROWS = 32
OUT_TM, OUT_TN = 512, 512
OUT_NT = D_MODEL // OUT_TN
OUT_ROW_CHUNK = 64
OUT_N_PIECES = 8
OUT_K_PIECE = D_MODEL // OUT_N_PIECES
STAT_LANES = 128
assert OUT_NT <= STAT_LANES
HALO_ROWS = 16
CONV_CHUNK = 128
HALF_WIDTH = CONV_WIDTH // 2
assert HALF_WIDTH == ATTN_WIDTH // 2
P_SECTIONS = ("cb", "cc", "ch", "cz", "q", "az")
P_SLAB_WIDTH = len(P_SECTIONS) * HALF_WIDTH
P_KV_START = 2 * P_SLAB_WIDTH
assert P_KV_START + 2 * KV_WIDTH == PROJ_WIDTH
assert HALF_WIDTH // CONV_CHUNK == OUT_N_PIECES == N_Q_HEADS // 2


def _silu(z):
    return z / (1.0 + jnp.exp(-z))


def _entry_norm_kernel(x_ref, g_ref, b_ref, h_ref, mu_ref, rstd_ref):
    n_groups = NORM_TM // SUBLANES

    def group(r):
        return pl.ds(pl.multiple_of(r * SUBLANES, SUBLANES), SUBLANES)

    def mean_body(r, carry):
        rows = group(r)
        mu_ref[rows, :] = jnp.mean(x_ref[rows, :], axis=-1, keepdims=True)
        return carry

    def var_body(r, carry):
        rows = group(r)
        xc = x_ref[rows, :] - mu_ref[rows, :]
        var = jnp.mean(xc * xc, axis=-1, keepdims=True)
        rstd_ref[rows, :] = lax.rsqrt(var + LN_EPS)
        return carry

    def scale_body(r, carry):
        rows = group(r)
        y = (x_ref[rows, :] - mu_ref[rows, :]) * rstd_ref[rows, :] * g_ref[...] + b_ref[...]
        h_ref[rows, :] = y.astype(jnp.bfloat16)
        return carry

    lax.fori_loop(0, n_groups, mean_body, 0, unroll=NORM_UNROLL)
    lax.fori_loop(0, n_groups, var_body, 0, unroll=NORM_UNROLL)
    lax.fori_loop(0, n_groups, scale_body, 0, unroll=NORM_UNROLL)


def _entry_norm_first_tile(x2d, g, b):
    vec = pl.BlockSpec((1, D_MODEL), lambda i: (0, 0))
    return pl.pallas_call(
        _entry_norm_kernel,
        name="entry_norm",
        grid=(IN_TM // NORM_TM,),
        in_specs=[pl.BlockSpec((NORM_TM, D_MODEL), lambda i: (i, 0)), vec, vec],
        out_specs=pl.BlockSpec((NORM_TM, D_MODEL), lambda i: (i, 0)),
        out_shape=jax.ShapeDtypeStruct((IN_TM, D_MODEL), jnp.bfloat16),
        scratch_shapes=[pltpu.VMEM((NORM_TM, 1), jnp.float32),
                        pltpu.VMEM((NORM_TM, 1), jnp.float32)],
        compiler_params=pltpu.CompilerParams(
            dimension_semantics=("arbitrary",),
            vmem_limit_bytes=V7X_VMEM_LIMIT_BYTES),
    )(x2d, g, b)


def _in_proj_kernel(h0_ref, x_ref, g_ref, b_ref, w_ref, wo_ref,
                    p_ref, mu_ref, rstd_ref, wob_ref, h_even_ref, h_odd_ref):
    i = pl.program_id(0)
    j = pl.program_id(1)

    @pl.when((i == 0) & (j == 0))
    def _():
        h_even_ref[...] = h0_ref[...]

    def step(cur_ref, nxt_ref):
        p_ref[...] = jnp.dot(cur_ref[...], w_ref[...].astype(jnp.bfloat16),
                             preferred_element_type=jnp.float32).astype(jnp.bfloat16)
        chunk = jnp.minimum(j, IN_LN_STEPS - 1)
        base = pl.multiple_of(chunk * IN_LN_ROWS, IN_LN_ROWS)
        g = g_ref[...]
        b = b_ref[...]
        for r in range(IN_LN_ROWS // BF16_ROWS):
            rows = slice(r * BF16_ROWS, (r + 1) * BF16_ROWS)
            x = x_ref[rows, :]
            mu = jnp.mean(x, axis=-1, keepdims=True)
            xc = x - mu
            rstd = lax.rsqrt(jnp.mean(xc * xc, axis=-1, keepdims=True) + LN_EPS)
            mu_ref[rows, :] = mu
            rstd_ref[rows, :] = rstd
            nxt_ref[pl.ds(base + r * BF16_ROWS, BF16_ROWS), :] = (
                (xc * rstd * g + b).astype(jnp.bfloat16))
        wob_ref[...] = wo_ref[...].astype(jnp.bfloat16)

    @pl.when((i & 1) == 0)
    def _():
        step(h_even_ref, h_odd_ref)

    @pl.when((i & 1) == 1)
    def _():
        step(h_odd_ref, h_even_ref)


def _in_proj(h0, x2d, g, b, w_in, w_out):
    n_i, n_j = SEQ // IN_TM, PROJ_WIDTH // IN_TN
    n_chunks = SEQ // IN_LN_ROWS
    n_slabs = D_MODEL // WOUT_CAST_ROWS
    assert n_i * n_j >= n_slabs and n_j >= IN_LN_STEPS

    def next_tile_chunk(i, j):
        return (lax.rem(i + 1, n_i) * IN_LN_STEPS + jnp.minimum(j, IN_LN_STEPS - 1), 0)

    def slab(i, j):
        return (jnp.minimum(i * n_j + j, n_slabs - 1), 0)

    def stored_tile(i, j):
        wide_tiles = CONV_WIDTH // IN_TN
        half_tiles = HALF_WIDTH // IN_TN
        assert (wide_tiles, half_tiles) == (4, 2)
        k_tile, v_tile, az_tile = OFF_K // IN_TN, OFF_V // IN_TN, OFF_AZ // IN_TN
        is_az = j >= az_tile
        section = jnp.where(is_az, len(P_SECTIONS) - 1, lax.shift_right_logical(j, 2))
        t = jnp.where(is_az, j - az_tile, j & 3)
        place = (lax.shift_right_logical(t, 1) * (P_SLAB_WIDTH // IN_TN) + section * half_tiles
                 + (t & 1))
        place = jnp.where(j == k_tile, P_KV_START // IN_TN, place)
        place = jnp.where(j == v_tile, P_KV_START // IN_TN + KV_WIDTH // IN_TN, place)
        return (i, place)

    vec = pl.BlockSpec((1, D_MODEL), lambda i, j: (0, 0))
    return pl.pallas_call(
        _in_proj_kernel,
        name="in_proj",
        grid=(n_i, n_j),
        in_specs=[
            pl.BlockSpec((IN_TM, D_MODEL), lambda i, j: (0, 0), pipeline_mode=pl.Buffered(1)),
            pl.BlockSpec((IN_LN_ROWS, D_MODEL), next_tile_chunk),
            vec, vec,
            pl.BlockSpec((D_MODEL, IN_TN), lambda i, j: (0, j)),
            pl.BlockSpec((WOUT_CAST_ROWS, D_MODEL), slab),
        ],
        out_specs=[
            pl.BlockSpec((IN_TM, IN_TN), stored_tile),
            pl.BlockSpec((IN_LN_ROWS, 1), next_tile_chunk),
            pl.BlockSpec((IN_LN_ROWS, 1), next_tile_chunk),
            pl.BlockSpec((WOUT_CAST_ROWS, D_MODEL), slab),
        ],
        out_shape=[
            jax.ShapeDtypeStruct((SEQ, PROJ_WIDTH), jnp.bfloat16),
            jax.ShapeDtypeStruct((SEQ, 1), jnp.float32),
            jax.ShapeDtypeStruct((SEQ, 1), jnp.float32),
            jax.ShapeDtypeStruct((D_MODEL, D_MODEL), jnp.bfloat16),
        ],
        scratch_shapes=[pltpu.VMEM((IN_TM, D_MODEL), jnp.bfloat16),
                        pltpu.VMEM((IN_TM, D_MODEL), jnp.bfloat16)],
        compiler_params=pltpu.CompilerParams(
            dimension_semantics=("arbitrary", "arbitrary"),
            vmem_limit_bytes=V7X_VMEM_LIMIT_BYTES),
    )(h0, x2d, g, b, w_in, w_out)


class _MixerHalfBlock:
    N_CONV_CHUNKS = HALF_WIDTH // CONV_CHUNK
    N_KV_GROUPS = N_KV_HEADS // 2

    def __init__(self, blk, half, sink_ref, main_ref, halo_prev_ref, halo_next_ref,
                 kv_prev_ref, kv_cur_ref, kv_next_ref, cw_ref, store):
        self.blk, self.half, self.sink_ref, self.store = blk, half, sink_ref, store

        def section(ref, s):
            return ref.at[:, s * HALF_WIDTH:(s + 1) * HALF_WIDTH]

        self.cb_ref, self.cc_ref, self.ch_ref, self.cz_ref, self.q_ref, self.az_ref = (
            section(main_ref, s) for s in range(len(P_SECTIONS)))
        self.ccp_ref, self.chp_ref = section(halo_prev_ref, 1), section(halo_prev_ref, 2)
        self.ccn_ref, self.chn_ref = section(halo_next_ref, 1), section(halo_next_ref, 2)
        self.cw_ref = cw_ref
        half_kv = KV_WIDTH // 2
        k0 = pl.multiple_of(half * half_kv, half_kv)
        v0 = pl.multiple_of(KV_WIDTH + half * half_kv, half_kv)
        kv_refs = (kv_prev_ref, kv_cur_ref, kv_next_ref)
        self.k_refs = tuple(r.at[:, pl.ds(k0, half_kv)] for r in kv_refs)
        self.v_refs = tuple(r.at[:, pl.ds(v0, half_kv)] for r in kv_refs)
        self._mask = None

    def conv_chunk(self, c):
        f32 = jnp.float32
        cols = slice(c * CONV_CHUNK, (c + 1) * CONV_CHUNK)
        row = lax.broadcasted_iota(jnp.int32, (BLOCK, CONV_CHUNK), 0)
        u = self.cc_ref[:, cols].astype(f32) * self.ch_ref[:, cols].astype(f32)
        u_prev = jnp.where(self.blk > 0,
                           self.ccp_ref[HALO_ROWS - 1:HALO_ROWS, cols].astype(f32)
                           * self.chp_ref[HALO_ROWS - 1:HALO_ROWS, cols].astype(f32), 0.0)
        u_next = jnp.where(self.blk < N_BLOCKS - 1,
                           self.ccn_ref[0:1, cols].astype(f32)
                           * self.chn_ref[0:1, cols].astype(f32), 0.0)
        um1 = jnp.where(row == 0, u_prev, pltpu.roll(u, 1, axis=0))
        up1 = jnp.where(row == BLOCK - 1, u_next, pltpu.roll(u, BLOCK - 1, axis=0))
        cw_ref = self.cw_ref
        conv = um1 * cw_ref[0:1, cols] + u * cw_ref[1:2, cols] + up1 * cw_ref[2:3, cols]
        y = self.cb_ref[:, cols].astype(f32) * conv * _silu(self.cz_ref[:, cols].astype(f32))
        self.store(0, c * CONV_CHUNK, CONV_CHUNK, y.astype(jnp.bfloat16))

    def _band_mask(self):
        if self._mask is None:
            band = 3 * BLOCK
            qi = lax.broadcasted_iota(jnp.int32, (BLOCK, band), 0)
            kj = lax.broadcasted_iota(jnp.int32, (BLOCK, band), 1) - BLOCK
            dist_i = jnp.abs(qi - kj)
            k_pos = kj + self.blk * BLOCK
            valid = (dist_i <= WINDOW) & (k_pos >= 0) & (k_pos < SEQ)
            self._mask = (valid, dist_i.astype(jnp.float32) * -LOG2_E)
        return self._mask

    @staticmethod
    def _block_diagonal(blocks):
        zero = jnp.zeros_like(blocks[0])
        return jnp.concatenate([jnp.concatenate([blocks[0], zero], axis=1),
                                jnp.concatenate([zero, blocks[1]], axis=1)], axis=0)

    def scores(self):
        assert self.N_KV_GROUPS == 2
        q = jnp.concatenate(
            [jnp.concatenate(
                [self.q_ref[:, (kg * GQA_GROUP + g) * HEAD_DIM:(kg * GQA_GROUP + g + 1) * HEAD_DIM]
                 for kg in range(self.N_KV_GROUPS)], axis=1)
             for g in range(GQA_GROUP)], axis=0)
        keys = self._block_diagonal(
            [jnp.concatenate([r[:, kg * HEAD_DIM:(kg + 1) * HEAD_DIM] for r in self.k_refs], axis=0)
             for kg in range(self.N_KV_GROUPS)])
        return lax.dot_general(q, keys, (((1,), (1,)), ((), ())),
                               preferred_element_type=jnp.float32)

    def softmax_head(self, kg, g, scores):
        valid, neg_dist_log2 = self._band_mask()
        heads_per_half = N_Q_HEADS // 2
        half_slope = jnp.where(self.half == 1, 2.0 ** (-8.0 * heads_per_half / N_Q_HEADS),
                               1.0).astype(jnp.float32)
        local = kg * GQA_GROUP + g
        slope = half_slope * (2.0 ** (-8.0 * (local + 1) / N_Q_HEADS))
        sink = self.sink_ref[self.half * heads_per_half + local] * LOG2_E
        band = 3 * BLOCK
        s = (scores[g * BLOCK:(g + 1) * BLOCK, kg * band:(kg + 1) * band]
             * (HEAD_DIM ** -0.5 * LOG2_E) + neg_dist_log2 * slope)
        s = jnp.where(valid, s, NEG_INF)
        m = jnp.maximum(jnp.max(s, axis=-1, keepdims=True), sink)
        e = jnp.exp2(s - m)
        return e.astype(jnp.bfloat16), jnp.sum(e, axis=-1, keepdims=True) + jnp.exp2(sink - m)

    def weighted_values(self, heads):
        probs = jnp.concatenate(
            [jnp.concatenate([heads[kg * GQA_GROUP + g][0] for kg in range(self.N_KV_GROUPS)], axis=1)
             for g in range(GQA_GROUP)], axis=0)
        values = self._block_diagonal(
            [jnp.concatenate([r[:, kg * HEAD_DIM:(kg + 1) * HEAD_DIM] for r in self.v_refs], axis=0)
             for kg in range(self.N_KV_GROUPS)])
        return jnp.dot(probs, values, preferred_element_type=jnp.float32)

    def finish(self, outs, heads):
        for kg in range(self.N_KV_GROUPS):
            for g in range(GQA_GROUP):
                local = kg * GQA_GROUP + g
                o = (outs[g * BLOCK:(g + 1) * BLOCK, kg * HEAD_DIM:(kg + 1) * HEAD_DIM]
                     / heads[local][1])
                gate = _silu(
                    self.az_ref[:, local * HEAD_DIM:(local + 1) * HEAD_DIM].astype(jnp.float32))
                self.store(CONV_WIDTH, local * HEAD_DIM, HEAD_DIM, (o * gate).astype(jnp.bfloat16))

    def run_all(self):
        for c in range(self.N_CONV_CHUNKS):
            self.conv_chunk(c)
        scores = self.scores()
        heads = [self.softmax_head(kg, g, scores)
                 for kg in range(self.N_KV_GROUPS) for g in range(GQA_GROUP)]
        self.finish(self.weighted_values(heads), heads)


N_MIXER_REFS = 7


def _mixer_in_specs(blk_of, half_of):
    halo_per_block = BLOCK // HALO_ROWS
    n_halo = SEQ // HALO_ROWS

    def slab(rows, row_fn):
        return pl.BlockSpec((rows, P_SLAB_WIDTH),
                            lambda *idx: (row_fn(blk_of(*idx)), half_of(*idx)))

    def kv(row_fn):
        return pl.BlockSpec((BLOCK, 2 * KV_WIDTH),
                            lambda *idx: (row_fn(blk_of(*idx)), P_KV_START // (2 * KV_WIDTH)))

    cur = lambda b: b
    prev_blk = lambda b: jnp.maximum(b - 1, 0)
    next_blk = lambda b: jnp.minimum(b + 1, N_BLOCKS - 1)
    prev_halo = lambda b: jnp.maximum(b * halo_per_block - 1, 0)
    next_halo = lambda b: jnp.minimum((b + 1) * halo_per_block, n_halo - 1)
    return [
        slab(BLOCK, cur),
        slab(HALO_ROWS, prev_halo),
        slab(HALO_ROWS, next_halo),
        kv(prev_blk), kv(cur), kv(next_blk),
        pl.BlockSpec((3, HALF_WIDTH), lambda *idx: (0, half_of(*idx))),
    ]


def _mixers_first_tile_kernel(sink_ref, *refs):
    mixer_refs, y_ref = refs[:N_MIXER_REFS], refs[N_MIXER_REFS]
    blk = pl.program_id(0)
    half = pl.program_id(1)

    def store(section, col, width, value):
        start = pl.multiple_of(half * HALF_WIDTH + (section + col), HEAD_DIM)
        y_ref[:, pl.ds(start, width)] = value

    _MixerHalfBlock(blk, half, sink_ref, *mixer_refs, store).run_all()


def _mixers_first_tile(p, conv_w, sink):
    in_specs = [pl.BlockSpec(memory_space=pltpu.SMEM)] + _mixer_in_specs(
        lambda b, hf: b, lambda b, hf: hf)
    return pl.pallas_call(
        _mixers_first_tile_kernel,
        name="mixers",
        grid=(OUT_TM // BLOCK, 2),
        in_specs=in_specs,
        out_specs=pl.BlockSpec((BLOCK, D_MODEL), lambda b, hf: (b, 0)),
        out_shape=jax.ShapeDtypeStruct((OUT_TM, D_MODEL), jnp.bfloat16),
        compiler_params=pltpu.CompilerParams(
            dimension_semantics=("arbitrary", "arbitrary"),
            vmem_limit_bytes=V7X_VMEM_LIMIT_BYTES),
    )(sink, *([p] * (N_MIXER_REFS - 1)), conv_w)


def _next_tile_block(i, n):
    return jnp.minimum((i + 1) * (OUT_TM // BLOCK) + lax.shift_right_logical(n, 1), N_BLOCKS - 1)


def _out_proj_kernel(sink_ref, y0_ref, x_ref, mu_ref, rstd_ref, g0_ref, b0_ref, w_ref,
                     g1_ref, b1_ref, *refs):
    mixer_refs = refs[:N_MIXER_REFS]
    o_ref, z_ref, zmean_ref, zm2_ref, y_even_ref, y_odd_ref = refs[N_MIXER_REFS:]
    i = pl.program_id(0)
    n = pl.program_id(1)

    @pl.when((i == 0) & (n == 0))
    def _():
        y_even_ref[...] = y0_ref[...]

    @pl.when(n == 0)
    def _():
        zmean_ref[...] = jnp.zeros_like(zmean_ref)
        zm2_ref[...] = jnp.zeros_like(zm2_ref)

    def step(cur_ref, nxt_ref):
        half = n & 1
        row0 = pl.multiple_of(lax.shift_right_logical(n, 1) * BLOCK, BLOCK)

        def store(section, col, width, value):
            start = pl.multiple_of(half * HALF_WIDTH + (section + col), HEAD_DIM)
            nxt_ref[pl.ds(row0, BLOCK), pl.ds(start, width)] = value

        mx = _MixerHalfBlock(_next_tile_block(i, n), half, sink_ref, *mixer_refs, store)

        def dot_piece(k):
            ks = slice(k * OUT_K_PIECE, (k + 1) * OUT_K_PIECE)
            return jnp.dot(cur_ref[:, ks], w_ref[ks, :], preferred_element_type=jnp.float32)

        band_scores = mx.scores()
        heads = []
        y = None
        for k in range(OUT_N_PIECES):
            piece = dot_piece(k)
            y = piece if y is None else y + piece
            kg, g = divmod(k, GQA_GROUP)
            heads.append(mx.softmax_head(kg, g, band_scores))
            mx.conv_chunk(k)
        outs = mx.weighted_values(heads)

        ag0 = DN_ALPHA * g0_ref[...]
        ab0 = DN_ALPHA * b0_ref[...]
        own_lane = lax.broadcasted_iota(jnp.int32, (OUT_ROW_CHUNK, STAT_LANES), 1) == n
        for c in range(OUT_TM // OUT_ROW_CHUNK):
            rows = slice(c * OUT_ROW_CHUNK, (c + 1) * OUT_ROW_CHUNK)
            z = ((x_ref[rows, :] - mu_ref[rows, :]) * rstd_ref[rows, :] * ag0 + ab0) + y[rows, :]
            z_ref[n, rows, :] = z
            zm = jnp.mean(z, axis=-1, keepdims=True)
            zc = z - zm
            zmean_ref[rows, :] = jnp.where(own_lane, zm, zmean_ref[rows, :])
            zm2_ref[rows, :] = jnp.where(own_lane, jnp.sum(zc * zc, axis=-1, keepdims=True),
                                         zm2_ref[rows, :])

        mx.finish(outs, heads)

    @pl.when((i & 1) == 0)
    def _():
        step(y_even_ref, y_odd_ref)

    @pl.when((i & 1) == 1)
    def _():
        step(y_odd_ref, y_even_ref)

    @pl.when(n == OUT_NT - 1)
    def _():
        used_lane = lax.broadcasted_iota(jnp.int32, (OUT_TM, STAT_LANES), 1) < OUT_NT
        means = zmean_ref[...]
        row_mean = jnp.sum(means, axis=-1, keepdims=True) * (1.0 / OUT_NT)
        dm = jnp.where(used_lane, means - row_mean, 0.0)
        m2 = jnp.sum(zm2_ref[...] + OUT_TN * (dm * dm), axis=-1, keepdims=True)
        zmean_ref[:, 0:1] = row_mean
        zm2_ref[:, 0:1] = lax.rsqrt(m2 * (1.0 / D_MODEL) + LN_EPS)

        def body(c, carry):
            rows = pl.ds(pl.multiple_of(c * OUT_ROW_CHUNK, OUT_ROW_CHUNK), OUT_ROW_CHUNK)
            mean = zmean_ref[rows, 0:1]
            rstd = zm2_ref[rows, 0:1]
            for k in range(OUT_NT):
                cols = slice(k * OUT_TN, (k + 1) * OUT_TN)
                o_ref[rows, cols] = (z_ref[k, rows, :] - mean) * rstd * g1_ref[:, cols] + b1_ref[:, cols]
            return carry

        lax.fori_loop(0, OUT_TM // OUT_ROW_CHUNK, body, 0)


def _out_proj(x2d, mu, rstd, g0, b0, ymix0, w, g1, b1, p, conv_w, sink):
    assert OUT_NT == 2 * (OUT_TM // BLOCK)
    grid = (SEQ // OUT_TM, OUT_NT)
    row_stat = pl.BlockSpec((OUT_TM, 1), lambda i, n: (i, 0))
    col_vec = pl.BlockSpec((1, OUT_TN), lambda i, n: (0, n))
    full_vec = pl.BlockSpec((1, D_MODEL), lambda i, n: (0, 0))
    in_specs = [
        pl.BlockSpec(memory_space=pltpu.SMEM),
        pl.BlockSpec((OUT_TM, D_MODEL), lambda i, n: (0, 0),
                     pipeline_mode=pl.Buffered(1)),
        pl.BlockSpec((OUT_TM, OUT_TN), lambda i, n: (i, n)),
        row_stat, row_stat,
        col_vec, col_vec,
        pl.BlockSpec((D_MODEL, OUT_TN), lambda i, n: (0, n)),
        full_vec, full_vec,
    ] + _mixer_in_specs(_next_tile_block, lambda i, n: n & 1)
    return pl.pallas_call(
        _out_proj_kernel,
        name="out_proj",
        grid=grid,
        in_specs=in_specs,
        out_specs=pl.BlockSpec((OUT_TM, D_MODEL), lambda i, n: (i, 0)),
        out_shape=jax.ShapeDtypeStruct((SEQ, D_MODEL), jnp.float32),
        scratch_shapes=[
            pltpu.VMEM((OUT_NT, OUT_TM, OUT_TN), jnp.float32),
            pltpu.VMEM((OUT_TM, STAT_LANES), jnp.float32),
            pltpu.VMEM((OUT_TM, STAT_LANES), jnp.float32),
            pltpu.VMEM((OUT_TM, D_MODEL), jnp.bfloat16),
            pltpu.VMEM((OUT_TM, D_MODEL), jnp.bfloat16),
        ],
        compiler_params=pltpu.CompilerParams(
            dimension_semantics=("arbitrary", "arbitrary"),
            vmem_limit_bytes=V7X_VMEM_LIMIT_BYTES),
    )(sink, ymix0, x2d, mu, rstd, g0, b0, w, g1, b1, *([p] * (N_MIXER_REFS - 1)), conv_w)


def kernel(x, emb_ln_g, emb_ln_b, w_in, conv_w, sink, w_out, ln_g, ln_b):
    batch, seq, d_model = x.shape
    assert (batch, seq, d_model) == (1, SEQ, D_MODEL)
    assert w_in.shape == (1, D_MODEL, PROJ_WIDTH) and w_out.shape == (1, D_MODEL, D_MODEL)
    x2d = x.reshape(SEQ, D_MODEL)
    g0 = emb_ln_g.reshape(1, D_MODEL)
    b0 = emb_ln_b.reshape(1, D_MODEL)
    g1 = ln_g.reshape(1, D_MODEL)
    b1 = ln_b.reshape(1, D_MODEL)
    h0 = _entry_norm_first_tile(x2d, g0, b0)
    p, mu, rstd, w_out_bf16 = _in_proj(h0, x2d, g0, b0, w_in[0], w_out[0])
    ymix0 = _mixers_first_tile(p, conv_w[0], sink[0])
    out = _out_proj(x2d, mu, rstd, g0, b0, ymix0, w_out_bf16, g1, b1, p, conv_w[0], sink[0])
    return out.reshape(1, SEQ, D_MODEL)
```

```python
import jax
import jax.numpy as jnp
from jax import lax
from jax.experimental import pallas as pl
from jax.experimental.pallas import tpu as pltpu

D_MODEL = 4096
SEQ = 8192
CONV_WIDTH = D_MODEL // 2
ATTN_WIDTH = D_MODEL - CONV_WIDTH
HEAD_DIM = 128
N_Q_HEADS = ATTN_WIDTH // HEAD_DIM
N_KV_HEADS = N_Q_HEADS // 4
GQA_GROUP = N_Q_HEADS // N_KV_HEADS
KV_WIDTH = N_KV_HEADS * HEAD_DIM
WINDOW = 128
BLOCK = 128
N_BLOCKS = SEQ // BLOCK
DN_ALPHA = 2.0 ** 0.25
LN_EPS = 1e-5
NEG_INF = -1e30
LOG2_E = 1.4426950408889634
PROJ_WIDTH = 4 * CONV_WIDTH + ATTN_WIDTH + 2 * KV_WIDTH + ATTN_WIDTH

OFF_CB = 0
OFF_CC = OFF_CB + CONV_WIDTH
OFF_CH = OFF_CC + CONV_WIDTH
OFF_CZ = OFF_CH + CONV_WIDTH
OFF_Q = OFF_CZ + CONV_WIDTH
OFF_K = OFF_Q + ATTN_WIDTH
OFF_V = OFF_K + KV_WIDTH
OFF_AZ = OFF_V + KV_WIDTH

V7X_VMEM_LIMIT_BYTES = 58 * 1024 * 1024

NORM_TM = 256
BF16_ROWS = 16
IN_TM, IN_TN = 1024, 512
IN_LN_STEPS = 16
IN_LN_ROWS = IN_TM // IN_LN_STEPS
WOUT_CAST_ROWS = 32
OUT_TM, OUT_TN = 512, 512
OUT_NT = D_MODEL // OUT_TN
OUT_ROW_CHUNK = 64
OUT_N_PIECES = 8
OUT_K_PIECE = D_MODEL // OUT_N_PIECES
STAT_LANES = 128
assert OUT_NT <= STAT_LANES
HALO_ROWS = 16
CONV_CHUNK = 128
HALF_WIDTH = CONV_WIDTH // 2
assert HALF_WIDTH == ATTN_WIDTH // 2
P_SECTIONS = ("cb", "cc", "ch", "cz", "q", "az")
P_SLAB_WIDTH = len(P_SECTIONS) * HALF_WIDTH
P_KV_START = 2 * P_SLAB_WIDTH
assert P_KV_START + 2 * KV_WIDTH == PROJ_WIDTH
assert HALF_WIDTH // CONV_CHUNK == OUT_N_PIECES == N_Q_HEADS // 2


def _silu(z):
    return z / (1.0 + jnp.exp(-z))


def _row_stats(x):
    mu = jnp.mean(x, axis=-1, keepdims=True)
    xc = x - mu
    return mu, lax.rsqrt(jnp.mean(xc * xc, axis=-1, keepdims=True) + LN_EPS), xc


def _entry_norm_kernel(x_ref, g_ref, b_ref, h_ref):
    g = g_ref[...]
    b = b_ref[...]

    def chunk(c, carry):
        for r in range(IN_LN_ROWS // BF16_ROWS):
            rows = pl.ds(pl.multiple_of(c * IN_LN_ROWS, IN_LN_ROWS) + r * BF16_ROWS, BF16_ROWS)
            _, rstd, xc = _row_stats(x_ref[rows, :])
            h_ref[rows, :] = (xc * rstd * g + b).astype(jnp.bfloat16)
        return carry

    lax.fori_loop(0, NORM_TM // IN_LN_ROWS, chunk, 0)


def _entry_norm_first_tile(x2d, g, b):
    vec = pl.BlockSpec((1, D_MODEL), lambda i: (0, 0))
    return pl.pallas_call(
        _entry_norm_kernel,
        name="entry_norm",
        grid=(IN_TM // NORM_TM,),
        in_specs=[pl.BlockSpec((NORM_TM, D_MODEL), lambda i: (i, 0)), vec, vec],
        out_specs=pl.BlockSpec((NORM_TM, D_MODEL), lambda i: (i, 0)),
        out_shape=jax.ShapeDtypeStruct((IN_TM, D_MODEL), jnp.bfloat16),
        compiler_params=pltpu.CompilerParams(
            dimension_semantics=("arbitrary",),
            vmem_limit_bytes=V7X_VMEM_LIMIT_BYTES),
    )(x2d, g, b)


def _in_proj_kernel(h0_ref, x_ref, g_ref, b_ref, w_ref, wo_ref,
                    p_ref, mu_ref, rstd_ref, wob_ref, h_even_ref, h_odd_ref):
    i = pl.program_id(0)
    j = pl.program_id(1)

    @pl.when((i == 0) & (j == 0))
    def _():
        h_even_ref[...] = h0_ref[...]

    def step(cur_ref, nxt_ref):
        p_ref[...] = jnp.dot(cur_ref[...], w_ref[...].astype(jnp.bfloat16),
                             preferred_element_type=jnp.float32).astype(jnp.bfloat16)
        chunk = jnp.minimum(j, IN_LN_STEPS - 1)
        base = pl.multiple_of(chunk * IN_LN_ROWS, IN_LN_ROWS)
        g = g_ref[...]
        b = b_ref[...]
        for r in range(IN_LN_ROWS // BF16_ROWS):
            rows = slice(r * BF16_ROWS, (r + 1) * BF16_ROWS)
            mu, rstd, xc = _row_stats(x_ref[rows, :])
            mu_ref[rows, :] = mu
            rstd_ref[rows, :] = rstd
            nxt_ref[pl.ds(base + r * BF16_ROWS, BF16_ROWS), :] = (
                (xc * rstd * g + b).astype(jnp.bfloat16))
        wob_ref[...] = wo_ref[...].astype(jnp.bfloat16)

    @pl.when((i & 1) == 0)
    def _():
        step(h_even_ref, h_odd_ref)

    @pl.when((i & 1) == 1)
    def _():
        step(h_odd_ref, h_even_ref)


def _in_proj(h0, x2d, g, b, w_in, w_out):
    n_i, n_j = SEQ // IN_TM, PROJ_WIDTH // IN_TN
    n_chunks = SEQ // IN_LN_ROWS
    n_slabs = D_MODEL // WOUT_CAST_ROWS
    assert n_i * n_j >= n_slabs and n_j >= IN_LN_STEPS

    def next_tile_chunk(i, j):
        return (lax.rem(i + 1, n_i) * IN_LN_STEPS + jnp.minimum(j, IN_LN_STEPS - 1), 0)

    def slab(i, j):
        return (jnp.minimum(i * n_j + j, n_slabs - 1), 0)

    def stored_tile(i, j):
        wide_tiles = CONV_WIDTH // IN_TN
        half_tiles = HALF_WIDTH // IN_TN
        assert (wide_tiles, half_tiles) == (4, 2)
        k_tile, v_tile, az_tile = OFF_K // IN_TN, OFF_V // IN_TN, OFF_AZ // IN_TN
        is_az = j >= az_tile
        section = jnp.where(is_az, len(P_SECTIONS) - 1, lax.shift_right_logical(j, 2))
        t = jnp.where(is_az, j - az_tile, j & 3)
        place = (lax.shift_right_logical(t, 1) * (P_SLAB_WIDTH // IN_TN) + section * half_tiles
                 + (t & 1))
        place = jnp.where(j == k_tile, P_KV_START // IN_TN, place)
        place = jnp.where(j == v_tile, P_KV_START // IN_TN + KV_WIDTH // IN_TN, place)
        return (i, place)

    vec = pl.BlockSpec((1, D_MODEL), lambda i, j: (0, 0))
    return pl.pallas_call(
        _in_proj_kernel,
        name="in_proj",
        grid=(n_i, n_j),
        in_specs=[
            pl.BlockSpec((IN_TM, D_MODEL), lambda i, j: (0, 0), pipeline_mode=pl.Buffered(1)),
            pl.BlockSpec((IN_LN_ROWS, D_MODEL), next_tile_chunk),
            vec, vec,
            pl.BlockSpec((D_MODEL, IN_TN), lambda i, j: (0, j)),
            pl.BlockSpec((WOUT_CAST_ROWS, D_MODEL), slab),
        ],
        out_specs=[
            pl.BlockSpec((IN_TM, IN_TN), stored_tile),
            pl.BlockSpec((IN_LN_ROWS, 1), next_tile_chunk),
            pl.BlockSpec((IN_LN_ROWS, 1), next_tile_chunk),
            pl.BlockSpec((WOUT_CAST_ROWS, D_MODEL), slab),
        ],
        out_shape=[
            jax.ShapeDtypeStruct((SEQ, PROJ_WIDTH), jnp.bfloat16),
            jax.ShapeDtypeStruct((SEQ, 1), jnp.float32),
            jax.ShapeDtypeStruct((SEQ, 1), jnp.float32),
            jax.ShapeDtypeStruct((D_MODEL, D_MODEL), jnp.bfloat16),
        ],
        scratch_shapes=[pltpu.VMEM((IN_TM, D_MODEL), jnp.bfloat16),
                        pltpu.VMEM((IN_TM, D_MODEL), jnp.bfloat16)],
        compiler_params=pltpu.CompilerParams(
            dimension_semantics=("arbitrary", "arbitrary"),
            vmem_limit_bytes=V7X_VMEM_LIMIT_BYTES),
    )(h0, x2d, g, b, w_in, w_out)


class _MixerHalfBlock:
    N_CONV_CHUNKS = HALF_WIDTH // CONV_CHUNK
    N_KV_GROUPS = N_KV_HEADS // 2

    def __init__(self, blk, half, sink_ref, main_ref, halo_prev_ref, halo_next_ref,
                 kv_prev_ref, kv_cur_ref, kv_next_ref, cw_ref, store):
        self.blk, self.half, self.sink_ref, self.store = blk, half, sink_ref, store

        def section(ref, s):
            return ref.at[:, s * HALF_WIDTH:(s + 1) * HALF_WIDTH]

        self.cb_ref, self.cc_ref, self.ch_ref, self.cz_ref, self.q_ref, self.az_ref = (
            section(main_ref, s) for s in range(len(P_SECTIONS)))
        self.ccp_ref, self.chp_ref = section(halo_prev_ref, 1), section(halo_prev_ref, 2)
        self.ccn_ref, self.chn_ref = section(halo_next_ref, 1), section(halo_next_ref, 2)
        self.cw_ref = cw_ref
        half_kv = KV_WIDTH // 2
        k0 = pl.multiple_of(half * half_kv, half_kv)
        v0 = pl.multiple_of(KV_WIDTH + half * half_kv, half_kv)
        kv_refs = (kv_prev_ref, kv_cur_ref, kv_next_ref)
        self.k_refs = tuple(r.at[:, pl.ds(k0, half_kv)] for r in kv_refs)
        self.v_refs = tuple(r.at[:, pl.ds(v0, half_kv)] for r in kv_refs)
        self._mask = None

    def conv_chunk(self, c):
        f32 = jnp.float32
        cols = slice(c * CONV_CHUNK, (c + 1) * CONV_CHUNK)
        row = lax.broadcasted_iota(jnp.int32, (BLOCK, CONV_CHUNK), 0)
        u = self.cc_ref[:, cols].astype(f32) * self.ch_ref[:, cols].astype(f32)
        u_prev = jnp.where(self.blk > 0,
                           self.ccp_ref[HALO_ROWS - 1:HALO_ROWS, cols].astype(f32)
                           * self.chp_ref[HALO_ROWS - 1:HALO_ROWS, cols].astype(f32), 0.0)
        u_next = jnp.where(self.blk < N_BLOCKS - 1,
                           self.ccn_ref[0:1, cols].astype(f32)
                           * self.chn_ref[0:1, cols].astype(f32), 0.0)
        um1 = jnp.where(row == 0, u_prev, pltpu.roll(u, 1, axis=0))
        up1 = jnp.where(row == BLOCK - 1, u_next, pltpu.roll(u, BLOCK - 1, axis=0))
        cw_ref = self.cw_ref
        conv = um1 * cw_ref[0:1, cols] + u * cw_ref[1:2, cols] + up1 * cw_ref[2:3, cols]
        y = self.cb_ref[:, cols].astype(f32) * conv * _silu(self.cz_ref[:, cols].astype(f32))
        self.store(0, c * CONV_CHUNK, CONV_CHUNK, y.astype(jnp.bfloat16))

    def _band_mask(self):
        if self._mask is None:
            band = 3 * BLOCK
            qi = lax.broadcasted_iota(jnp.int32, (BLOCK, band), 0)
            kj = lax.broadcasted_iota(jnp.int32, (BLOCK, band), 1) - BLOCK
            dist_i = jnp.abs(qi - kj)
            k_pos = kj + self.blk * BLOCK
            valid = (dist_i <= WINDOW) & (k_pos >= 0) & (k_pos < SEQ)
            self._mask = (valid, dist_i.astype(jnp.float32) * -LOG2_E)
        return self._mask

    @staticmethod
    def _block_diagonal(blocks):
        zero = jnp.zeros_like(blocks[0])
        return jnp.concatenate([jnp.concatenate([blocks[0], zero], axis=1),
                                jnp.concatenate([zero, blocks[1]], axis=1)], axis=0)

    def scores(self):
        assert self.N_KV_GROUPS == 2
        q = jnp.concatenate(
            [jnp.concatenate(
                [self.q_ref[:, (kg * GQA_GROUP + g) * HEAD_DIM:(kg * GQA_GROUP + g + 1) * HEAD_DIM]
                 for kg in range(self.N_KV_GROUPS)], axis=1)
             for g in range(GQA_GROUP)], axis=0)
        keys = self._block_diagonal(
            [jnp.concatenate([r[:, kg * HEAD_DIM:(kg + 1) * HEAD_DIM] for r in self.k_refs], axis=0)
             for kg in range(self.N_KV_GROUPS)])
        return lax.dot_general(q, keys, (((1,), (1,)), ((), ())),
                               preferred_element_type=jnp.float32)

    def softmax_head(self, kg, g, scores):
        valid, neg_dist_log2 = self._band_mask()
        heads_per_half = N_Q_HEADS // 2
        half_slope = jnp.where(self.half == 1, 2.0 ** (-8.0 * heads_per_half / N_Q_HEADS),
                               1.0).astype(jnp.float32)
        local = kg * GQA_GROUP + g
        slope = half_slope * (2.0 ** (-8.0 * (local + 1) / N_Q_HEADS))
        sink = self.sink_ref[self.half * heads_per_half + local] * LOG2_E
        band = 3 * BLOCK
        s = (scores[g * BLOCK:(g + 1) * BLOCK, kg * band:(kg + 1) * band]
             * (HEAD_DIM ** -0.5 * LOG2_E) + neg_dist_log2 * slope)
        s = jnp.where(valid, s, NEG_INF)
        m = jnp.maximum(jnp.max(s, axis=-1, keepdims=True), sink)
        e = jnp.exp2(s - m)
        return e.astype(jnp.bfloat16), jnp.sum(e, axis=-1, keepdims=True) + jnp.exp2(sink - m)

    def weighted_values(self, heads):
        probs = jnp.concatenate(
            [jnp.concatenate([heads[kg * GQA_GROUP + g][0] for kg in range(self.N_KV_GROUPS)], axis=1)
             for g in range(GQA_GROUP)], axis=0)
        values = self._block_diagonal(
            [jnp.concatenate([r[:, kg * HEAD_DIM:(kg + 1) * HEAD_DIM] for r in self.v_refs], axis=0)
             for kg in range(self.N_KV_GROUPS)])
        return jnp.dot(probs, values, preferred_element_type=jnp.float32)

    def finish(self, outs, heads):
        for kg in range(self.N_KV_GROUPS):
            for g in range(GQA_GROUP):
                local = kg * GQA_GROUP + g
                o = (outs[g * BLOCK:(g + 1) * BLOCK, kg * HEAD_DIM:(kg + 1) * HEAD_DIM]
                     / heads[local][1])
                gate = _silu(
                    self.az_ref[:, local * HEAD_DIM:(local + 1) * HEAD_DIM].astype(jnp.float32))
                self.store(CONV_WIDTH, local * HEAD_DIM, HEAD_DIM, (o * gate).astype(jnp.bfloat16))

    def run_all(self):
        for c in range(self.N_CONV_CHUNKS):
            self.conv_chunk(c)
        scores = self.scores()
        heads = [self.softmax_head(kg, g, scores)
                 for kg in range(self.N_KV_GROUPS) for g in range(GQA_GROUP)]
        self.finish(self.weighted_values(heads), heads)


N_MIXER_REFS = 7


def _mixer_in_specs(blk_of, half_of):
    halo_per_block = BLOCK // HALO_ROWS
    n_halo = SEQ // HALO_ROWS

    def slab(rows, row_fn):
        return pl.BlockSpec((rows, P_SLAB_WIDTH),
                            lambda *idx: (row_fn(blk_of(*idx)), half_of(*idx)))

    def kv(row_fn):
        return pl.BlockSpec((BLOCK, 2 * KV_WIDTH),
                            lambda *idx: (row_fn(blk_of(*idx)), P_KV_START // (2 * KV_WIDTH)))

    cur = lambda b: b
    prev_blk = lambda b: jnp.maximum(b - 1, 0)
    next_blk = lambda b: jnp.minimum(b + 1, N_BLOCKS - 1)
    prev_halo = lambda b: jnp.maximum(b * halo_per_block - 1, 0)
    next_halo = lambda b: jnp.minimum((b + 1) * halo_per_block, n_halo - 1)
    return [
        slab(BLOCK, cur),
        slab(HALO_ROWS, prev_halo),
        slab(HALO_ROWS, next_halo),
        kv(prev_blk), kv(cur), kv(next_blk),
        pl.BlockSpec((3, HALF_WIDTH), lambda *idx: (0, half_of(*idx))),
    ]


def _mixers_first_tile_kernel(sink_ref, *refs):
    mixer_refs, y_ref = refs[:N_MIXER_REFS], refs[N_MIXER_REFS]
    blk = pl.program_id(0)
    half = pl.program_id(1)

    def store(section, col, width, value):
        start = pl.multiple_of(half * HALF_WIDTH + (section + col), HEAD_DIM)
        y_ref[:, pl.ds(start, width)] = value

    _MixerHalfBlock(blk, half, sink_ref, *mixer_refs, store).run_all()


def _mixers_first_tile(p, conv_w, sink):
    in_specs = [pl.BlockSpec(memory_space=pltpu.SMEM)] + _mixer_in_specs(
        lambda b, hf: b, lambda b, hf: hf)
    return pl.pallas_call(
        _mixers_first_tile_kernel,
        name="mixers",
        grid=(OUT_TM // BLOCK, 2),
        in_specs=in_specs,
        out_specs=pl.BlockSpec((BLOCK, D_MODEL), lambda b, hf: (b, 0)),
        out_shape=jax.ShapeDtypeStruct((OUT_TM, D_MODEL), jnp.bfloat16),
        compiler_params=pltpu.CompilerParams(
            dimension_semantics=("arbitrary", "arbitrary"),
            vmem_limit_bytes=V7X_VMEM_LIMIT_BYTES),
    )(sink, *([p] * (N_MIXER_REFS - 1)), conv_w)


def _next_tile_block(i, n):
    return jnp.minimum((i + 1) * (OUT_TM // BLOCK) + lax.shift_right_logical(n, 1), N_BLOCKS - 1)


def _out_proj_kernel(sink_ref, y0_ref, x_ref, mu_ref, rstd_ref, g0_ref, b0_ref, w_ref,
                     g1_ref, b1_ref, *refs):
    mixer_refs = refs[:N_MIXER_REFS]
    o_ref, z_ref, zmean_ref, zm2_ref, y_even_ref, y_odd_ref = refs[N_MIXER_REFS:]
    i = pl.program_id(0)
    n = pl.program_id(1)

    @pl.when((i == 0) & (n == 0))
    def _():
        y_even_ref[...] = y0_ref[...]

    @pl.when(n == 0)
    def _():
        zmean_ref[...] = jnp.zeros_like(zmean_ref)
        zm2_ref[...] = jnp.zeros_like(zm2_ref)

    def step(cur_ref, nxt_ref):
        half = n & 1
        row0 = pl.multiple_of(lax.shift_right_logical(n, 1) * BLOCK, BLOCK)

        def store(section, col, width, value):
            start = pl.multiple_of(half * HALF_WIDTH + (section + col), HEAD_DIM)
            nxt_ref[pl.ds(row0, BLOCK), pl.ds(start, width)] = value

        mx = _MixerHalfBlock(_next_tile_block(i, n), half, sink_ref, *mixer_refs, store)

        def dot_piece(k):
            ks = slice(k * OUT_K_PIECE, (k + 1) * OUT_K_PIECE)
            return jnp.dot(cur_ref[:, ks], w_ref[ks, :], preferred_element_type=jnp.float32)

        band_scores = mx.scores()
        heads = []
        y = None
        for k in range(OUT_N_PIECES):
            piece = dot_piece(k)
            y = piece if y is None else y + piece
            kg, g = divmod(k, GQA_GROUP)
            heads.append(mx.softmax_head(kg, g, band_scores))
            mx.conv_chunk(k)
        outs = mx.weighted_values(heads)

        ag0 = DN_ALPHA * g0_ref[...]
        ab0 = DN_ALPHA * b0_ref[...]
        own_lane = lax.broadcasted_iota(jnp.int32, (OUT_ROW_CHUNK, STAT_LANES), 1) == n
        for c in range(OUT_TM // OUT_ROW_CHUNK):
            rows = slice(c * OUT_ROW_CHUNK, (c + 1) * OUT_ROW_CHUNK)
            z = ((x_ref[rows, :] - mu_ref[rows, :]) * rstd_ref[rows, :] * ag0 + ab0) + y[rows, :]
            z_ref[n, rows, :] = z
            zm = jnp.mean(z, axis=-1, keepdims=True)
            zc = z - zm
            zmean_ref[rows, :] = jnp.where(own_lane, zm, zmean_ref[rows, :])
            zm2_ref[rows, :] = jnp.where(own_lane, jnp.sum(zc * zc, axis=-1, keepdims=True),
                                         zm2_ref[rows, :])

        mx.finish(outs, heads)

    @pl.when((i & 1) == 0)
    def _():
        step(y_even_ref, y_odd_ref)

    @pl.when((i & 1) == 1)
    def _():
        step(y_odd_ref, y_even_ref)

    @pl.when(n == OUT_NT - 1)
    def _():
        used_lane = lax.broadcasted_iota(jnp.int32, (OUT_TM, STAT_LANES), 1) < OUT_NT
        means = zmean_ref[...]
        row_mean = jnp.sum(means, axis=-1, keepdims=True) * (1.0 / OUT_NT)
        dm = jnp.where(used_lane, means - row_mean, 0.0)
        m2 = jnp.sum(zm2_ref[...] + OUT_TN * (dm * dm), axis=-1, keepdims=True)
        zmean_ref[:, 0:1] = row_mean
        zm2_ref[:, 0:1] = lax.rsqrt(m2 * (1.0 / D_MODEL) + LN_EPS)

        def body(c, carry):
            rows = pl.ds(pl.multiple_of(c * OUT_ROW_CHUNK, OUT_ROW_CHUNK), OUT_ROW_CHUNK)
            mean = zmean_ref[rows, 0:1]
            rstd = zm2_ref[rows, 0:1]
            for k in range(OUT_NT):
                cols = slice(k * OUT_TN, (k + 1) * OUT_TN)
                o_ref[rows, cols] = (z_ref[k, rows, :] - mean) * rstd * g1_ref[:, cols] + b1_ref[:, cols]
            return carry

        lax.fori_loop(0, OUT_TM // OUT_ROW_CHUNK, body, 0)


def _out_proj(x2d, mu, rstd, g0, b0, ymix0, w, g1, b1, p, conv_w, sink):
    assert OUT_NT == 2 * (OUT_TM // BLOCK)
    grid = (SEQ // OUT_TM, OUT_NT)
    row_stat = pl.BlockSpec((OUT_TM, 1), lambda i, n: (i, 0))
    col_vec = pl.BlockSpec((1, OUT_TN), lambda i, n: (0, n))
    full_vec = pl.BlockSpec((1, D_MODEL), lambda i, n: (0, 0))
    in_specs = [
        pl.BlockSpec(memory_space=pltpu.SMEM),
        pl.BlockSpec((OUT_TM, D_MODEL), lambda i, n: (0, 0),
                     pipeline_mode=pl.Buffered(1)),
        pl.BlockSpec((OUT_TM, OUT_TN), lambda i, n: (i, n)),
        row_stat, row_stat,
        col_vec, col_vec,
        pl.BlockSpec((D_MODEL, OUT_TN), lambda i, n: (0, n)),
        full_vec, full_vec,
    ] + _mixer_in_specs(_next_tile_block, lambda i, n: n & 1)
    return pl.pallas_call(
        _out_proj_kernel,
        name="out_proj",
        grid=grid,
        in_specs=in_specs,
        out_specs=pl.BlockSpec((OUT_TM, D_MODEL), lambda i, n: (i, 0)),
        out_shape=jax.ShapeDtypeStruct((SEQ, D_MODEL), jnp.float32),
        scratch_shapes=[
            pltpu.VMEM((OUT_NT, OUT_TM, OUT_TN), jnp.float32),
            pltpu.VMEM((OUT_TM, STAT_LANES), jnp.float32),
            pltpu.VMEM((OUT_TM, STAT_LANES), jnp.float32),
            pltpu.VMEM((OUT_TM, D_MODEL), jnp.bfloat16),
            pltpu.VMEM((OUT_TM, D_MODEL), jnp.bfloat16),
        ],
        compiler_params=pltpu.CompilerParams(
            dimension_semantics=("arbitrary", "arbitrary"),
            vmem_limit_bytes=V7X_VMEM_LIMIT_BYTES),
    )(sink, ymix0, x2d, mu, rstd, g0, b0, w, g1, b1, *([p] * (N_MIXER_REFS - 1)), conv_w)


def kernel(x, emb_ln_g, emb_ln_b, w_in, conv_w, sink, w_out, ln_g, ln_b):
    batch, seq, d_model = x.shape
    assert (batch, seq, d_model) == (1, SEQ, D_MODEL)
    assert w_in.shape == (1, D_MODEL, PROJ_WIDTH) and w_out.shape == (1, D_MODEL, D_MODEL)
    x2d = x.reshape(SEQ, D_MODEL)
    g0 = emb_ln_g.reshape(1, D_MODEL)
    b0 = emb_ln_b.reshape(1, D_MODEL)
    g1 = ln_g.reshape(1, D_MODEL)
    b1 = ln_b.reshape(1, D_MODEL)
    h0 = _entry_norm_first_tile(x2d, g0, b0)
    p, mu, rstd, w_out_bf16 = _in_proj(h0, x2d, g0, b0, w_in[0], w_out[0])
    ymix0 = _mixers_first_tile(p, conv_w[0], sink[0])
    out = _out_proj(x2d, mu, rstd, g0, b0, ymix0, w_out_bf16, g1, b1, p, conv_w[0], sink[0])
    return out.reshape(1, SEQ, D_MODEL)
```

```python
import jax
import jax.numpy as jnp
from jax import lax
from jax.experimental import pallas as pl
from jax.experimental.pallas import tpu as pltpu

D_MODEL = 4096
SEQ = 8192
CONV_WIDTH = D_MODEL // 2
ATTN_WIDTH = D_MODEL - CONV_WIDTH
HEAD_DIM = 128
N_Q_HEADS = ATTN_WIDTH // HEAD_DIM
N_KV_HEADS = N_Q_HEADS // 4
GQA_GROUP = N_Q_HEADS // N_KV_HEADS
KV_WIDTH = N_KV_HEADS * HEAD_DIM
WINDOW = 128
BLOCK = 128
N_BLOCKS = SEQ // BLOCK
DN_ALPHA = 2.0 ** 0.25
LN_EPS = 1e-5
NEG_INF = -1e30
LOG2_E = 1.4426950408889634
PROJ_WIDTH = 4 * CONV_WIDTH + ATTN_WIDTH + 2 * KV_WIDTH + ATTN_WIDTH

OFF_CB = 0
OFF_CC = OFF_CB + CONV_WIDTH
OFF_CH = OFF_CC + CONV_WIDTH
OFF_CZ = OFF_CH + CONV_WIDTH
OFF_Q = OFF_CZ + CONV_WIDTH
OFF_K = OFF_Q + ATTN_WIDTH
OFF_V = OFF_K + KV_WIDTH
OFF_AZ = OFF_V + KV_WIDTH

V7X_VMEM_LIMIT_BYTES = 58 * 1024 * 1024

NORM_TM = 256
BF16_ROWS = 16
IN_TM, IN_TN = 1024, 512
IN_M_PIECE = 256
IN_LN_STEPS = 16
IN_LN_ROWS = IN_TM // IN_LN_STEPS
WOUT_CAST_ROWS = 32
OUT_TM, OUT_TN = 512, 512
OUT_NT = D_MODEL // OUT_TN
OUT_ROW_CHUNK = 64
OUT_N_PIECES = 8
OUT_K_PIECE = D_MODEL // OUT_N_PIECES
STAT_LANES = 128
assert OUT_NT <= STAT_LANES
HALO_ROWS = 16
CONV_CHUNK = 128
HALF_WIDTH = CONV_WIDTH // 2
assert HALF_WIDTH == ATTN_WIDTH // 2
P_SECTIONS = ("cb", "cc", "ch", "cz", "q", "az")
P_SLAB_WIDTH = len(P_SECTIONS) * HALF_WIDTH
P_KV_START = 2 * P_SLAB_WIDTH
assert P_KV_START + 2 * KV_WIDTH == PROJ_WIDTH
assert HALF_WIDTH // CONV_CHUNK == OUT_N_PIECES == N_Q_HEADS // 2


def _silu(z):
    return z / (1.0 + jnp.exp(-z))


def _row_stats(x):
    mu = jnp.mean(x, axis=-1, keepdims=True)
    xc = x - mu
    return mu, lax.rsqrt(jnp.mean(xc * xc, axis=-1, keepdims=True) + LN_EPS), xc


def _entry_norm_kernel(x_ref, g_ref, b_ref, h_ref):
    g = g_ref[...]
    b = b_ref[...]

    def chunk(c, carry):
        for r in range(IN_LN_ROWS // BF16_ROWS):
            rows = pl.ds(pl.multiple_of(c * IN_LN_ROWS, IN_LN_ROWS) + r * BF16_ROWS, BF16_ROWS)
            _, rstd, xc = _row_stats(x_ref[rows, :])
            h_ref[rows, :] = (xc * rstd * g + b).astype(jnp.bfloat16)
        return carry

    lax.fori_loop(0, NORM_TM // IN_LN_ROWS, chunk, 0)


def _entry_norm_first_tile(x2d, g, b):
    vec = pl.BlockSpec((1, D_MODEL), lambda i: (0, 0))
    return pl.pallas_call(
        _entry_norm_kernel,
        name="entry_norm",
        grid=(IN_TM // NORM_TM,),
        in_specs=[pl.BlockSpec((NORM_TM, D_MODEL), lambda i: (i, 0)), vec, vec],
        out_specs=pl.BlockSpec((NORM_TM, D_MODEL), lambda i: (i, 0)),
        out_shape=jax.ShapeDtypeStruct((IN_TM, D_MODEL), jnp.bfloat16),
        compiler_params=pltpu.CompilerParams(
            dimension_semantics=("arbitrary",),
            vmem_limit_bytes=V7X_VMEM_LIMIT_BYTES),
    )(x2d, g, b)


def _in_proj_kernel(h0_ref, x_ref, g_ref, b_ref, w_ref, wo_ref,
                    p_ref, mu_ref, rstd_ref, wob_ref, h_even_ref, h_odd_ref):
    i = pl.program_id(0)
    j = pl.program_id(1)

    @pl.when((i == 0) & (j == 0))
    def _():
        h_even_ref[...] = h0_ref[...]

    def step(cur_ref, nxt_ref):
        w_bf16 = w_ref[...].astype(jnp.bfloat16)
        for m in range(IN_TM // IN_M_PIECE):
            rows = slice(m * IN_M_PIECE, (m + 1) * IN_M_PIECE)
            p_ref[rows, :] = jnp.dot(cur_ref[rows, :], w_bf16,
                                     preferred_element_type=jnp.float32).astype(jnp.bfloat16)
        chunk = jnp.minimum(j, IN_LN_STEPS - 1)
        base = pl.multiple_of(chunk * IN_LN_ROWS, IN_LN_ROWS)
        g = g_ref[...]
        b = b_ref[...]
        for r in range(IN_LN_ROWS // BF16_ROWS):
            rows = slice(r * BF16_ROWS, (r + 1) * BF16_ROWS)
            mu, rstd, xc = _row_stats(x_ref[rows, :])
            mu_ref[rows, :] = mu
            rstd_ref[rows, :] = rstd
            nxt_ref[pl.ds(base + r * BF16_ROWS, BF16_ROWS), :] = (
                (xc * rstd * g + b).astype(jnp.bfloat16))
        wob_ref[...] = wo_ref[...].astype(jnp.bfloat16)

    @pl.when((i & 1) == 0)
    def _():
        step(h_even_ref, h_odd_ref)

    @pl.when((i & 1) == 1)
    def _():
        step(h_odd_ref, h_even_ref)


def _in_proj(h0, x2d, g, b, w_in, w_out):
    n_i, n_j = SEQ // IN_TM, PROJ_WIDTH // IN_TN
    n_chunks = SEQ // IN_LN_ROWS
    n_slabs = D_MODEL // WOUT_CAST_ROWS
    assert n_i * n_j >= n_slabs and n_j >= IN_LN_STEPS

    def next_tile_chunk(i, j):
        return (lax.rem(i + 1, n_i) * IN_LN_STEPS + jnp.minimum(j, IN_LN_STEPS - 1), 0)

    def slab(i, j):
        return (jnp.minimum(i * n_j + j, n_slabs - 1), 0)

    def stored_tile(i, j):
        wide_tiles = CONV_WIDTH // IN_TN
        half_tiles = HALF_WIDTH // IN_TN
        assert (wide_tiles, half_tiles) == (4, 2)
        k_tile, v_tile, az_tile = OFF_K // IN_TN, OFF_V // IN_TN, OFF_AZ // IN_TN
        is_az = j >= az_tile
        section = jnp.where(is_az, len(P_SECTIONS) - 1, lax.shift_right_logical(j, 2))
        t = jnp.where(is_az, j - az_tile, j & 3)
        place = (lax.shift_right_logical(t, 1) * (P_SLAB_WIDTH // IN_TN) + section * half_tiles
                 + (t & 1))
        place = jnp.where(j == k_tile, P_KV_START // IN_TN, place)
        place = jnp.where(j == v_tile, P_KV_START // IN_TN + KV_WIDTH // IN_TN, place)
        return (i, place)

    vec = pl.BlockSpec((1, D_MODEL), lambda i, j: (0, 0))
    return pl.pallas_call(
        _in_proj_kernel,
        name="in_proj",
        grid=(n_i, n_j),
        in_specs=[
            pl.BlockSpec((IN_TM, D_MODEL), lambda i, j: (0, 0), pipeline_mode=pl.Buffered(1)),
            pl.BlockSpec((IN_LN_ROWS, D_MODEL), next_tile_chunk),
            vec, vec,
            pl.BlockSpec((D_MODEL, IN_TN), lambda i, j: (0, j)),
            pl.BlockSpec((WOUT_CAST_ROWS, D_MODEL), slab),
        ],
        out_specs=[
            pl.BlockSpec((IN_TM, IN_TN), stored_tile),
            pl.BlockSpec((IN_LN_ROWS, 1), next_tile_chunk),
            pl.BlockSpec((IN_LN_ROWS, 1), next_tile_chunk),
            pl.BlockSpec((WOUT_CAST_ROWS, D_MODEL), slab),
        ],
        out_shape=[
            jax.ShapeDtypeStruct((SEQ, PROJ_WIDTH), jnp.bfloat16),
            jax.ShapeDtypeStruct((SEQ, 1), jnp.float32),
            jax.ShapeDtypeStruct((SEQ, 1), jnp.float32),
            jax.ShapeDtypeStruct((D_MODEL, D_MODEL), jnp.bfloat16),
        ],
        scratch_shapes=[pltpu.VMEM((IN_TM, D_MODEL), jnp.bfloat16),
                        pltpu.VMEM((IN_TM, D_MODEL), jnp.bfloat16)],
        compiler_params=pltpu.CompilerParams(
            dimension_semantics=("arbitrary", "arbitrary"),
            vmem_limit_bytes=V7X_VMEM_LIMIT_BYTES),
    )(h0, x2d, g, b, w_in, w_out)


class _MixerHalfBlock:
    N_CONV_CHUNKS = HALF_WIDTH // CONV_CHUNK
    N_KV_GROUPS = N_KV_HEADS // 2

    def __init__(self, blk, half, sink_ref, main_ref, halo_prev_ref, halo_next_ref,
                 kv_prev_ref, kv_cur_ref, kv_next_ref, cw_ref, store):
        self.blk, self.half, self.sink_ref, self.store = blk, half, sink_ref, store

        def section(ref, s):
            return ref.at[:, s * HALF_WIDTH:(s + 1) * HALF_WIDTH]

        self.cb_ref, self.cc_ref, self.ch_ref, self.cz_ref, self.q_ref, self.az_ref = (
            section(main_ref, s) for s in range(len(P_SECTIONS)))
        self.ccp_ref, self.chp_ref = section(halo_prev_ref, 1), section(halo_prev_ref, 2)
        self.ccn_ref, self.chn_ref = section(halo_next_ref, 1), section(halo_next_ref, 2)
        self.cw_ref = cw_ref
        half_kv = KV_WIDTH // 2
        k0 = pl.multiple_of(half * half_kv, half_kv)
        v0 = pl.multiple_of(KV_WIDTH + half * half_kv, half_kv)
        kv_refs = (kv_prev_ref, kv_cur_ref, kv_next_ref)
        self.k_refs = tuple(r.at[:, pl.ds(k0, half_kv)] for r in kv_refs)
        self.v_refs = tuple(r.at[:, pl.ds(v0, half_kv)] for r in kv_refs)
        self._mask = None

    def conv_chunk(self, c):
        f32 = jnp.float32
        cols = slice(c * CONV_CHUNK, (c + 1) * CONV_CHUNK)
        row = lax.broadcasted_iota(jnp.int32, (BLOCK, CONV_CHUNK), 0)
        u = self.cc_ref[:, cols].astype(f32) * self.ch_ref[:, cols].astype(f32)
        u_prev = jnp.where(self.blk > 0,
                           self.ccp_ref[HALO_ROWS - 1:HALO_ROWS, cols].astype(f32)
                           * self.chp_ref[HALO_ROWS - 1:HALO_ROWS, cols].astype(f32), 0.0)
        u_next = jnp.where(self.blk < N_BLOCKS - 1,
                           self.ccn_ref[0:1, cols].astype(f32)
                           * self.chn_ref[0:1, cols].astype(f32), 0.0)
        um1 = jnp.where(row == 0, u_prev, pltpu.roll(u, 1, axis=0))
        up1 = jnp.where(row == BLOCK - 1, u_next, pltpu.roll(u, BLOCK - 1, axis=0))
        cw_ref = self.cw_ref
        conv = um1 * cw_ref[0:1, cols] + u * cw_ref[1:2, cols] + up1 * cw_ref[2:3, cols]
        y = self.cb_ref[:, cols].astype(f32) * conv * _silu(self.cz_ref[:, cols].astype(f32))
        self.store(0, c * CONV_CHUNK, CONV_CHUNK, y.astype(jnp.bfloat16))

    def _band_mask(self):
        if self._mask is None:
            band = 3 * BLOCK
            qi = lax.broadcasted_iota(jnp.int32, (BLOCK, band), 0)
            kj = lax.broadcasted_iota(jnp.int32, (BLOCK, band), 1) - BLOCK
            dist_i = jnp.abs(qi - kj)
            k_pos = kj + self.blk * BLOCK
            valid = (dist_i <= WINDOW) & (k_pos >= 0) & (k_pos < SEQ)
            self._mask = (valid, dist_i.astype(jnp.float32) * -LOG2_E)
        return self._mask

    @staticmethod
    def _block_diagonal(blocks):
        zero = jnp.zeros_like(blocks[0])
        return jnp.concatenate([jnp.concatenate([blocks[0], zero], axis=1),
                                jnp.concatenate([zero, blocks[1]], axis=1)], axis=0)

    def scores(self):
        assert self.N_KV_GROUPS == 2
        q = jnp.concatenate(
            [jnp.concatenate(
                [self.q_ref[:, (kg * GQA_GROUP + g) * HEAD_DIM:(kg * GQA_GROUP + g + 1) * HEAD_DIM]
                 for kg in range(self.N_KV_GROUPS)], axis=1)
             for g in range(GQA_GROUP)], axis=0)
        keys = self._block_diagonal(
            [jnp.concatenate([r[:, kg * HEAD_DIM:(kg + 1) * HEAD_DIM] for r in self.k_refs], axis=0)
             for kg in range(self.N_KV_GROUPS)])
        return lax.dot_general(q, keys, (((1,), (1,)), ((), ())),
                               preferred_element_type=jnp.float32)

    def softmax_head(self, kg, g, scores):
        valid, neg_dist_log2 = self._band_mask()
        heads_per_half = N_Q_HEADS // 2
        half_slope = jnp.where(self.half == 1, 2.0 ** (-8.0 * heads_per_half / N_Q_HEADS),
                               1.0).astype(jnp.float32)
        local = kg * GQA_GROUP + g
        slope = half_slope * (2.0 ** (-8.0 * (local + 1) / N_Q_HEADS))
        sink = self.sink_ref[self.half * heads_per_half + local] * LOG2_E
        band = 3 * BLOCK
        s = (scores[g * BLOCK:(g + 1) * BLOCK, kg * band:(kg + 1) * band]
             * (HEAD_DIM ** -0.5 * LOG2_E) + neg_dist_log2 * slope)
        s = jnp.where(valid, s, NEG_INF)
        m = jnp.maximum(jnp.max(s, axis=-1, keepdims=True), sink)
        e = jnp.exp2(s - m)
        return e.astype(jnp.bfloat16), jnp.sum(e, axis=-1, keepdims=True) + jnp.exp2(sink - m)

    def weighted_values(self, heads):
        probs = jnp.concatenate(
            [jnp.concatenate([heads[kg * GQA_GROUP + g][0] for kg in range(self.N_KV_GROUPS)], axis=1)
             for g in range(GQA_GROUP)], axis=0)
        values = self._block_diagonal(
            [jnp.concatenate([r[:, kg * HEAD_DIM:(kg + 1) * HEAD_DIM] for r in self.v_refs], axis=0)
             for kg in range(self.N_KV_GROUPS)])
        return jnp.dot(probs, values, preferred_element_type=jnp.float32)

    def finish(self, outs, heads):
        for kg in range(self.N_KV_GROUPS):
            for g in range(GQA_GROUP):
                local = kg * GQA_GROUP + g
                o = (outs[g * BLOCK:(g + 1) * BLOCK, kg * HEAD_DIM:(kg + 1) * HEAD_DIM]
                     / heads[local][1])
                gate = _silu(
                    self.az_ref[:, local * HEAD_DIM:(local + 1) * HEAD_DIM].astype(jnp.float32))
                self.store(CONV_WIDTH, local * HEAD_DIM, HEAD_DIM, (o * gate).astype(jnp.bfloat16))

    def run_all(self):
        for c in range(self.N_CONV_CHUNKS):
            self.conv_chunk(c)
        scores = self.scores()
        heads = [self.softmax_head(kg, g, scores)
                 for kg in range(self.N_KV_GROUPS) for g in range(GQA_GROUP)]
        self.finish(self.weighted_values(heads), heads)


N_MIXER_REFS = 7


def _mixer_in_specs(blk_of, half_of):
    halo_per_block = BLOCK // HALO_ROWS
    n_halo = SEQ // HALO_ROWS

    def slab(rows, row_fn):
        return pl.BlockSpec((rows, P_SLAB_WIDTH),
                            lambda *idx: (row_fn(blk_of(*idx)), half_of(*idx)))

    def kv(row_fn):
        return pl.BlockSpec((BLOCK, 2 * KV_WIDTH),
                            lambda *idx: (row_fn(blk_of(*idx)), P_KV_START // (2 * KV_WIDTH)))

    cur = lambda b: b
    prev_blk = lambda b: jnp.maximum(b - 1, 0)
    next_blk = lambda b: jnp.minimum(b + 1, N_BLOCKS - 1)
    prev_halo = lambda b: jnp.maximum(b * halo_per_block - 1, 0)
    next_halo = lambda b: jnp.minimum((b + 1) * halo_per_block, n_halo - 1)
    return [
        slab(BLOCK, cur),
        slab(HALO_ROWS, prev_halo),
        slab(HALO_ROWS, next_halo),
        kv(prev_blk), kv(cur), kv(next_blk),
        pl.BlockSpec((3, HALF_WIDTH), lambda *idx: (0, half_of(*idx))),
    ]


def _mixers_first_tile_kernel(sink_ref, *refs):
    mixer_refs, y_ref = refs[:N_MIXER_REFS], refs[N_MIXER_REFS]
    blk = pl.program_id(0)
    half = pl.program_id(1)

    def store(section, col, width, value):
        start = pl.multiple_of(half * HALF_WIDTH + (section + col), HEAD_DIM)
        y_ref[:, pl.ds(start, width)] = value

    _MixerHalfBlock(blk, half, sink_ref, *mixer_refs, store).run_all()


def _mixers_first_tile(p, conv_w, sink):
    in_specs = [pl.BlockSpec(memory_space=pltpu.SMEM)] + _mixer_in_specs(
        lambda b, hf: b, lambda b, hf: hf)
    return pl.pallas_call(
        _mixers_first_tile_kernel,
        name="mixers",
        grid=(OUT_TM // BLOCK, 2),
        in_specs=in_specs,
        out_specs=pl.BlockSpec((BLOCK, D_MODEL), lambda b, hf: (b, 0)),
        out_shape=jax.ShapeDtypeStruct((OUT_TM, D_MODEL), jnp.bfloat16),
        compiler_params=pltpu.CompilerParams(
            dimension_semantics=("arbitrary", "arbitrary"),
            vmem_limit_bytes=V7X_VMEM_LIMIT_BYTES),
    )(sink, *([p] * (N_MIXER_REFS - 1)), conv_w)


def _next_tile_block(i, n):
    return jnp.minimum((i + 1) * (OUT_TM // BLOCK) + lax.shift_right_logical(n, 1), N_BLOCKS - 1)


def _out_proj_kernel(sink_ref, y0_ref, x_ref, mu_ref, rstd_ref, g0_ref, b0_ref, w_ref,
                     g1_ref, b1_ref, *refs):
    mixer_refs = refs[:N_MIXER_REFS]
    o_ref, z_ref, zmean_ref, zm2_ref, y_even_ref, y_odd_ref = refs[N_MIXER_REFS:]
    i = pl.program_id(0)
    n = pl.program_id(1)

    @pl.when((i == 0) & (n == 0))
    def _():
        y_even_ref[...] = y0_ref[...]

    @pl.when(n == 0)
    def _():
        zmean_ref[...] = jnp.zeros_like(zmean_ref)
        zm2_ref[...] = jnp.zeros_like(zm2_ref)

    def step(cur_ref, nxt_ref):
        half = n & 1
        row0 = pl.multiple_of(lax.shift_right_logical(n, 1) * BLOCK, BLOCK)

        def store(section, col, width, value):
            start = pl.multiple_of(half * HALF_WIDTH + (section + col), HEAD_DIM)
            nxt_ref[pl.ds(row0, BLOCK), pl.ds(start, width)] = value

        mx = _MixerHalfBlock(_next_tile_block(i, n), half, sink_ref, *mixer_refs, store)

        def dot_piece(k):
            ks = slice(k * OUT_K_PIECE, (k + 1) * OUT_K_PIECE)
            return jnp.dot(cur_ref[:, ks], w_ref[ks, :], preferred_element_type=jnp.float32)

        band_scores = mx.scores()
        heads = []
        y = None
        for k in range(OUT_N_PIECES):
            piece = dot_piece(k)
            y = piece if y is None else y + piece
            kg, g = divmod(k, GQA_GROUP)
            heads.append(mx.softmax_head(kg, g, band_scores))
            mx.conv_chunk(k)
        outs = mx.weighted_values(heads)

        ag0 = DN_ALPHA * g0_ref[...]
        ab0 = DN_ALPHA * b0_ref[...]
        own_lane = lax.broadcasted_iota(jnp.int32, (OUT_ROW_CHUNK, STAT_LANES), 1) == n
        for c in range(OUT_TM // OUT_ROW_CHUNK):
            rows = slice(c * OUT_ROW_CHUNK, (c + 1) * OUT_ROW_CHUNK)
            z = ((x_ref[rows, :] - mu_ref[rows, :]) * rstd_ref[rows, :] * ag0 + ab0) + y[rows, :]
            z_ref[n, rows, :] = z
            zm = jnp.mean(z, axis=-1, keepdims=True)
            zc = z - zm
            zmean_ref[rows, :] = jnp.where(own_lane, zm, zmean_ref[rows, :])
            zm2_ref[rows, :] = jnp.where(own_lane, jnp.sum(zc * zc, axis=-1, keepdims=True),
                                         zm2_ref[rows, :])

        mx.finish(outs, heads)

    @pl.when((i & 1) == 0)
    def _():
        step(y_even_ref, y_odd_ref)

    @pl.when((i & 1) == 1)
    def _():
        step(y_odd_ref, y_even_ref)

    @pl.when(n == OUT_NT - 1)
    def _():
        used_lane = lax.broadcasted_iota(jnp.int32, (OUT_TM, STAT_LANES), 1) < OUT_NT
        means = zmean_ref[...]
        row_mean = jnp.sum(means, axis=-1, keepdims=True) * (1.0 / OUT_NT)
        dm = jnp.where(used_lane, means - row_mean, 0.0)
        m2 = jnp.sum(zm2_ref[...] + OUT_TN * (dm * dm), axis=-1, keepdims=True)
        zmean_ref[:, 0:1] = row_mean
        zm2_ref[:, 0:1] = lax.rsqrt(m2 * (1.0 / D_MODEL) + LN_EPS)

        def body(c, carry):
            rows = pl.ds(pl.multiple_of(c * OUT_ROW_CHUNK, OUT_ROW_CHUNK), OUT_ROW_CHUNK)
            mean = zmean_ref[rows, 0:1]
            rstd = zm2_ref[rows, 0:1]
            for k in range(OUT_NT):
                cols = slice(k * OUT_TN, (k + 1) * OUT_TN)
                o_ref[rows, cols] = (z_ref[k, rows, :] - mean) * rstd * g1_ref[:, cols] + b1_ref[:, cols]
            return carry

        lax.fori_loop(0, OUT_TM // OUT_ROW_CHUNK, body, 0)


def _out_proj(x2d, mu, rstd, g0, b0, ymix0, w, g1, b1, p, conv_w, sink):
    assert OUT_NT == 2 * (OUT_TM // BLOCK)
    grid = (SEQ // OUT_TM, OUT_NT)
    row_stat = pl.BlockSpec((OUT_TM, 1), lambda i, n: (i, 0))
    col_vec = pl.BlockSpec((1, OUT_TN), lambda i, n: (0, n))
    full_vec = pl.BlockSpec((1, D_MODEL), lambda i, n: (0, 0))
    in_specs = [
        pl.BlockSpec(memory_space=pltpu.SMEM),
        pl.BlockSpec((OUT_TM, D_MODEL), lambda i, n: (0, 0),
                     pipeline_mode=pl.Buffered(1)),
        pl.BlockSpec((OUT_TM, OUT_TN), lambda i, n: (i, n)),
        row_stat, row_stat,
        col_vec, col_vec,
        pl.BlockSpec((D_MODEL, OUT_TN), lambda i, n: (0, n)),
        full_vec, full_vec,
    ] + _mixer_in_specs(_next_tile_block, lambda i, n: n & 1)
    return pl.pallas_call(
        _out_proj_kernel,
        name="out_proj",
        grid=grid,
        in_specs=in_specs,
        out_specs=pl.BlockSpec((OUT_TM, D_MODEL), lambda i, n: (i, 0)),
        out_shape=jax.ShapeDtypeStruct((SEQ, D_MODEL), jnp.float32),
        scratch_shapes=[
            pltpu.VMEM((OUT_NT, OUT_TM, OUT_TN), jnp.float32),
            pltpu.VMEM((OUT_TM, STAT_LANES), jnp.float32),
            pltpu.VMEM((OUT_TM, STAT_LANES), jnp.float32),
            pltpu.VMEM((OUT_TM, D_MODEL), jnp.bfloat16),
            pltpu.VMEM((OUT_TM, D_MODEL), jnp.bfloat16),
        ],
        compiler_params=pltpu.CompilerParams(
            dimension_semantics=("arbitrary", "arbitrary"),
            vmem_limit_bytes=V7X_VMEM_LIMIT_BYTES),
    )(sink, ymix0, x2d, mu, rstd, g0, b0, w, g1, b1, *([p] * (N_MIXER_REFS - 1)), conv_w)


def kernel(x, emb_ln_g, emb_ln_b, w_in, conv_w, sink, w_out, ln_g, ln_b):
    batch, seq, d_model = x.shape
    assert (batch, seq, d_model) == (1, SEQ, D_MODEL)
    assert w_in.shape == (1, D_MODEL, PROJ_WIDTH) and w_out.shape == (1, D_MODEL, D_MODEL)
    x2d = x.reshape(SEQ, D_MODEL)
    g0 = emb_ln_g.reshape(1, D_MODEL)
    b0 = emb_ln_b.reshape(1, D_MODEL)
    g1 = ln_g.reshape(1, D_MODEL)
    b1 = ln_b.reshape(1, D_MODEL)
    h0 = _entry_norm_first_tile(x2d, g0, b0)
    p, mu, rstd, w_out_bf16 = _in_proj(h0, x2d, g0, b0, w_in[0], w_out[0])
    ymix0 = _mixers_first_tile(p, conv_w[0], sink[0])
    out = _out_proj(x2d, mu, rstd, g0, b0, ymix0, w_out_bf16, g1, b1, p, conv_w[0], sink[0])
    return out.reshape(1, SEQ, D_MODEL)
```

```python
import jax
import jax.numpy as jnp
from jax import lax
from jax.experimental import pallas as pl
from jax.experimental.pallas import tpu as pltpu

D_MODEL = 4096
SEQ = 8192
CONV_WIDTH = D_MODEL // 2
ATTN_WIDTH = D_MODEL - CONV_WIDTH
HEAD_DIM = 128
N_Q_HEADS = ATTN_WIDTH // HEAD_DIM
N_KV_HEADS = N_Q_HEADS // 4
GQA_GROUP = N_Q_HEADS // N_KV_HEADS
KV_WIDTH = N_KV_HEADS * HEAD_DIM
WINDOW = 128
BLOCK = 128
N_BLOCKS = SEQ // BLOCK
DN_ALPHA = 2.0 ** 0.25
LN_EPS = 1e-5
NEG_INF = -1e30
LOG2_E = 1.4426950408889634
PROJ_WIDTH = 4 * CONV_WIDTH + ATTN_WIDTH + 2 * KV_WIDTH + ATTN_WIDTH

OFF_CB = 0
OFF_CC = OFF_CB + CONV_WIDTH
OFF_CH = OFF_CC + CONV_WIDTH
OFF_CZ = OFF_CH + CONV_WIDTH
OFF_Q = OFF_CZ + CONV_WIDTH
OFF_K = OFF_Q + ATTN_WIDTH
OFF_V = OFF_K + KV_WIDTH
OFF_AZ = OFF_V + KV_WIDTH

V7X_VMEM_LIMIT_BYTES = 58 * 1024 * 1024

NORM_TM = 256
BF16_ROWS = 16
IN_TM, IN_TN = 1024, 512
IN_LN_STEPS = 16
IN_LN_ROWS = IN_TM // IN_LN_STEPS
WOUT_CAST_ROWS = 32
OUT_TM, OUT_TN = 512, 512
OUT_NT = D_MODEL // OUT_TN
OUT_ROW_CHUNK = 64
OUT_N_PIECES = 8
OUT_K_PIECE = D_MODEL // OUT_N_PIECES
OUT_HEAD_SLOTS = ((0, 1), (2,), (3, 4), (5,), (6,), (7,), (), ())
OUT_PV_AFTER_PIECE = 5
STAT_LANES = 128
assert OUT_NT <= STAT_LANES
HALO_ROWS = 16
CONV_CHUNK = 128
HALF_WIDTH = CONV_WIDTH // 2
assert HALF_WIDTH == ATTN_WIDTH // 2
P_SECTIONS = ("cb", "cc", "ch", "cz", "q", "az")
P_SLAB_WIDTH = len(P_SECTIONS) * HALF_WIDTH
P_KV_START = 2 * P_SLAB_WIDTH
assert P_KV_START + 2 * KV_WIDTH == PROJ_WIDTH
assert HALF_WIDTH // CONV_CHUNK == OUT_N_PIECES == N_Q_HEADS // 2


def _silu(z):
    return z / (1.0 + jnp.exp(-z))


def _row_stats(x):
    mu = jnp.mean(x, axis=-1, keepdims=True)
    xc = x - mu
    return mu, lax.rsqrt(jnp.mean(xc * xc, axis=-1, keepdims=True) + LN_EPS), xc


def _entry_norm_kernel(x_ref, g_ref, b_ref, h_ref):
    g = g_ref[...]
    b = b_ref[...]

    def chunk(c, carry):
        for r in range(IN_LN_ROWS // BF16_ROWS):
            rows = pl.ds(pl.multiple_of(c * IN_LN_ROWS, IN_LN_ROWS) + r * BF16_ROWS, BF16_ROWS)
            _, rstd, xc = _row_stats(x_ref[rows, :])
            h_ref[rows, :] = (xc * rstd * g + b).astype(jnp.bfloat16)
        return carry

    lax.fori_loop(0, NORM_TM // IN_LN_ROWS, chunk, 0)


def _entry_norm_first_tile(x2d, g, b):
    vec = pl.BlockSpec((1, D_MODEL), lambda i: (0, 0))
    return pl.pallas_call(
        _entry_norm_kernel,
        name="entry_norm",
        grid=(IN_TM // NORM_TM,),
        in_specs=[pl.BlockSpec((NORM_TM, D_MODEL), lambda i: (i, 0)), vec, vec],
        out_specs=pl.BlockSpec((NORM_TM, D_MODEL), lambda i: (i, 0)),
        out_shape=jax.ShapeDtypeStruct((IN_TM, D_MODEL), jnp.bfloat16),
        compiler_params=pltpu.CompilerParams(
            dimension_semantics=("arbitrary",),
            vmem_limit_bytes=V7X_VMEM_LIMIT_BYTES),
    )(x2d, g, b)


def _in_proj_kernel(h0_ref, x_ref, g_ref, b_ref, w_ref, wo_ref,
                    p_ref, mu_ref, rstd_ref, wob_ref, h_even_ref, h_odd_ref):
    i = pl.program_id(0)
    j = pl.program_id(1)

    @pl.when((i == 0) & (j == 0))
    def _():
        h_even_ref[...] = h0_ref[...]

    def step(cur_ref, nxt_ref):
        p_ref[...] = jnp.dot(cur_ref[...], w_ref[...].astype(jnp.bfloat16),
                             preferred_element_type=jnp.float32).astype(jnp.bfloat16)
        chunk = jnp.minimum(j, IN_LN_STEPS - 1)
        base = pl.multiple_of(chunk * IN_LN_ROWS, IN_LN_ROWS)
        g = g_ref[...]
        b = b_ref[...]
        for r in range(IN_LN_ROWS // BF16_ROWS):
            rows = slice(r * BF16_ROWS, (r + 1) * BF16_ROWS)
            mu, rstd, xc = _row_stats(x_ref[rows, :])
            mu_ref[rows, :] = mu
            rstd_ref[rows, :] = rstd
            nxt_ref[pl.ds(base + r * BF16_ROWS, BF16_ROWS), :] = (
                (xc * rstd * g + b).astype(jnp.bfloat16))
        wob_ref[...] = wo_ref[...].astype(jnp.bfloat16)

    @pl.when((i & 1) == 0)
    def _():
        step(h_even_ref, h_odd_ref)

    @pl.when((i & 1) == 1)
    def _():
        step(h_odd_ref, h_even_ref)


def _in_proj(h0, x2d, g, b, w_in, w_out):
    n_i, n_j = SEQ // IN_TM, PROJ_WIDTH // IN_TN
    n_chunks = SEQ // IN_LN_ROWS
    n_slabs = D_MODEL // WOUT_CAST_ROWS
    assert n_i * n_j >= n_slabs and n_j >= IN_LN_STEPS

    def next_tile_chunk(i, j):
        return (lax.rem(i + 1, n_i) * IN_LN_STEPS + jnp.minimum(j, IN_LN_STEPS - 1), 0)

    def slab(i, j):
        return (jnp.minimum(i * n_j + j, n_slabs - 1), 0)

    def stored_tile(i, j):
        wide_tiles = CONV_WIDTH // IN_TN
        half_tiles = HALF_WIDTH // IN_TN
        assert (wide_tiles, half_tiles) == (4, 2)
        k_tile, v_tile, az_tile = OFF_K // IN_TN, OFF_V // IN_TN, OFF_AZ // IN_TN
        is_az = j >= az_tile
        section = jnp.where(is_az, len(P_SECTIONS) - 1, lax.shift_right_logical(j, 2))
        t = jnp.where(is_az, j - az_tile, j & 3)
        place = (lax.shift_right_logical(t, 1) * (P_SLAB_WIDTH // IN_TN) + section * half_tiles
                 + (t & 1))
        place = jnp.where(j == k_tile, P_KV_START // IN_TN, place)
        place = jnp.where(j == v_tile, P_KV_START // IN_TN + KV_WIDTH // IN_TN, place)
        return (i, place)

    vec = pl.BlockSpec((1, D_MODEL), lambda i, j: (0, 0))
    return pl.pallas_call(
        _in_proj_kernel,
        name="in_proj",
        grid=(n_i, n_j),
        in_specs=[
            pl.BlockSpec((IN_TM, D_MODEL), lambda i, j: (0, 0), pipeline_mode=pl.Buffered(1)),
            pl.BlockSpec((IN_LN_ROWS, D_MODEL), next_tile_chunk),
            vec, vec,
            pl.BlockSpec((D_MODEL, IN_TN), lambda i, j: (0, j)),
            pl.BlockSpec((WOUT_CAST_ROWS, D_MODEL), slab),
        ],
        out_specs=[
            pl.BlockSpec((IN_TM, IN_TN), stored_tile),
            pl.BlockSpec((IN_LN_ROWS, 1), next_tile_chunk),
            pl.BlockSpec((IN_LN_ROWS, 1), next_tile_chunk),
            pl.BlockSpec((WOUT_CAST_ROWS, D_MODEL), slab),
        ],
        out_shape=[
            jax.ShapeDtypeStruct((SEQ, PROJ_WIDTH), jnp.bfloat16),
            jax.ShapeDtypeStruct((SEQ, 1), jnp.float32),
            jax.ShapeDtypeStruct((SEQ, 1), jnp.float32),
            jax.ShapeDtypeStruct((D_MODEL, D_MODEL), jnp.bfloat16),
        ],
        scratch_shapes=[pltpu.VMEM((IN_TM, D_MODEL), jnp.bfloat16),
                        pltpu.VMEM((IN_TM, D_MODEL), jnp.bfloat16)],
        compiler_params=pltpu.CompilerParams(
            dimension_semantics=("arbitrary", "arbitrary"),
            vmem_limit_bytes=V7X_VMEM_LIMIT_BYTES),
    )(h0, x2d, g, b, w_in, w_out)


class _MixerHalfBlock:
    N_CONV_CHUNKS = HALF_WIDTH // CONV_CHUNK
    N_KV_GROUPS = N_KV_HEADS // 2

    def __init__(self, blk, half, sink_ref, main_ref, halo_prev_ref, halo_next_ref,
                 kv_prev_ref, kv_cur_ref, kv_next_ref, cw_ref, store):
        self.blk, self.half, self.sink_ref, self.store = blk, half, sink_ref, store

        def section(ref, s):
            return ref.at[:, s * HALF_WIDTH:(s + 1) * HALF_WIDTH]

        self.cb_ref, self.cc_ref, self.ch_ref, self.cz_ref, self.q_ref, self.az_ref = (
            section(main_ref, s) for s in range(len(P_SECTIONS)))
        self.ccp_ref, self.chp_ref = section(halo_prev_ref, 1), section(halo_prev_ref, 2)
        self.ccn_ref, self.chn_ref = section(halo_next_ref, 1), section(halo_next_ref, 2)
        self.cw_ref = cw_ref
        half_kv = KV_WIDTH // 2
        k0 = pl.multiple_of(half * half_kv, half_kv)
        v0 = pl.multiple_of(KV_WIDTH + half * half_kv, half_kv)
        kv_refs = (kv_prev_ref, kv_cur_ref, kv_next_ref)
        self.k_refs = tuple(r.at[:, pl.ds(k0, half_kv)] for r in kv_refs)
        self.v_refs = tuple(r.at[:, pl.ds(v0, half_kv)] for r in kv_refs)
        self._mask = None

    def conv_chunk(self, c):
        f32 = jnp.float32
        cols = slice(c * CONV_CHUNK, (c + 1) * CONV_CHUNK)
        row = lax.broadcasted_iota(jnp.int32, (BLOCK, CONV_CHUNK), 0)
        u = self.cc_ref[:, cols].astype(f32) * self.ch_ref[:, cols].astype(f32)
        u_prev = jnp.where(self.blk > 0,
                           self.ccp_ref[HALO_ROWS - 1:HALO_ROWS, cols].astype(f32)
                           * self.chp_ref[HALO_ROWS - 1:HALO_ROWS, cols].astype(f32), 0.0)
        u_next = jnp.where(self.blk < N_BLOCKS - 1,
                           self.ccn_ref[0:1, cols].astype(f32)
                           * self.chn_ref[0:1, cols].astype(f32), 0.0)
        um1 = jnp.where(row == 0, u_prev, pltpu.roll(u, 1, axis=0))
        up1 = jnp.where(row == BLOCK - 1, u_next, pltpu.roll(u, BLOCK - 1, axis=0))
        cw_ref = self.cw_ref
        conv = um1 * cw_ref[0:1, cols] + u * cw_ref[1:2, cols] + up1 * cw_ref[2:3, cols]
        y = self.cb_ref[:, cols].astype(f32) * conv * _silu(self.cz_ref[:, cols].astype(f32))
        self.store(0, c * CONV_CHUNK, CONV_CHUNK, y.astype(jnp.bfloat16))

    def _band_mask(self):
        if self._mask is None:
            band = 3 * BLOCK
            qi = lax.broadcasted_iota(jnp.int32, (BLOCK, band), 0)
            kj = lax.broadcasted_iota(jnp.int32, (BLOCK, band), 1) - BLOCK
            dist_i = jnp.abs(qi - kj)
            k_pos = kj + self.blk * BLOCK
            valid = (dist_i <= WINDOW) & (k_pos >= 0) & (k_pos < SEQ)
            self._mask = (valid, dist_i.astype(jnp.float32) * -LOG2_E)
        return self._mask

    @staticmethod
    def _block_diagonal(blocks):
        zero = jnp.zeros_like(blocks[0])
        return jnp.concatenate([jnp.concatenate([blocks[0], zero], axis=1),
                                jnp.concatenate([zero, blocks[1]], axis=1)], axis=0)

    def scores(self):
        assert self.N_KV_GROUPS == 2
        q = jnp.concatenate(
            [jnp.concatenate(
                [self.q_ref[:, (kg * GQA_GROUP + g) * HEAD_DIM:(kg * GQA_GROUP + g + 1) * HEAD_DIM]
                 for kg in range(self.N_KV_GROUPS)], axis=1)
             for g in range(GQA_GROUP)], axis=0)
        keys = self._block_diagonal(
            [jnp.concatenate([r[:, kg * HEAD_DIM:(kg + 1) * HEAD_DIM] for r in self.k_refs], axis=0)
             for kg in range(self.N_KV_GROUPS)])
        return lax.dot_general(q, keys, (((1,), (1,)), ((), ())),
                               preferred_element_type=jnp.float32)

    def softmax_head(self, kg, g, scores):
        valid, neg_dist_log2 = self._band_mask()
        heads_per_half = N_Q_HEADS // 2
        half_slope = jnp.where(self.half == 1, 2.0 ** (-8.0 * heads_per_half / N_Q_HEADS),
                               1.0).astype(jnp.float32)
        local = kg * GQA_GROUP + g
        slope = half_slope * (2.0 ** (-8.0 * (local + 1) / N_Q_HEADS))
        sink = self.sink_ref[self.half * heads_per_half + local] * LOG2_E
        band = 3 * BLOCK
        s = (scores[g * BLOCK:(g + 1) * BLOCK, kg * band:(kg + 1) * band]
             * (HEAD_DIM ** -0.5 * LOG2_E) + neg_dist_log2 * slope)
        s = jnp.where(valid, s, NEG_INF)
        m = jnp.maximum(jnp.max(s, axis=-1, keepdims=True), sink)
        e = jnp.exp2(s - m)
        return e.astype(jnp.bfloat16), jnp.sum(e, axis=-1, keepdims=True) + jnp.exp2(sink - m)

    def weighted_values(self, heads):
        probs = jnp.concatenate(
            [jnp.concatenate([heads[kg * GQA_GROUP + g][0] for kg in range(self.N_KV_GROUPS)], axis=1)
             for g in range(GQA_GROUP)], axis=0)
        values = self._block_diagonal(
            [jnp.concatenate([r[:, kg * HEAD_DIM:(kg + 1) * HEAD_DIM] for r in self.v_refs], axis=0)
             for kg in range(self.N_KV_GROUPS)])
        return jnp.dot(probs, values, preferred_element_type=jnp.float32)

    def finish(self, outs, heads):
        for kg in range(self.N_KV_GROUPS):
            for g in range(GQA_GROUP):
                local = kg * GQA_GROUP + g
                o = (outs[g * BLOCK:(g + 1) * BLOCK, kg * HEAD_DIM:(kg + 1) * HEAD_DIM]
                     / heads[local][1])
                gate = _silu(
                    self.az_ref[:, local * HEAD_DIM:(local + 1) * HEAD_DIM].astype(jnp.float32))
                self.store(CONV_WIDTH, local * HEAD_DIM, HEAD_DIM, (o * gate).astype(jnp.bfloat16))

    def run_all(self):
        for c in range(self.N_CONV_CHUNKS):
            self.conv_chunk(c)
        scores = self.scores()
        heads = [self.softmax_head(kg, g, scores)
                 for kg in range(self.N_KV_GROUPS) for g in range(GQA_GROUP)]
        self.finish(self.weighted_values(heads), heads)


N_MIXER_REFS = 7


def _mixer_in_specs(blk_of, half_of):
    halo_per_block = BLOCK // HALO_ROWS
    n_halo = SEQ // HALO_ROWS

    def slab(rows, row_fn):
        return pl.BlockSpec((rows, P_SLAB_WIDTH),
                            lambda *idx: (row_fn(blk_of(*idx)), half_of(*idx)))

    def kv(row_fn):
        return pl.BlockSpec((BLOCK, 2 * KV_WIDTH),
                            lambda *idx: (row_fn(blk_of(*idx)), P_KV_START // (2 * KV_WIDTH)))

    cur = lambda b: b
    prev_blk = lambda b: jnp.maximum(b - 1, 0)
    next_blk = lambda b: jnp.minimum(b + 1, N_BLOCKS - 1)
    prev_halo = lambda b: jnp.maximum(b * halo_per_block - 1, 0)
    next_halo = lambda b: jnp.minimum((b + 1) * halo_per_block, n_halo - 1)
    return [
        slab(BLOCK, cur),
        slab(HALO_ROWS, prev_halo),
        slab(HALO_ROWS, next_halo),
        kv(prev_blk), kv(cur), kv(next_blk),
        pl.BlockSpec((3, HALF_WIDTH), lambda *idx: (0, half_of(*idx))),
    ]


def _mixers_first_tile_kernel(sink_ref, *refs):
    mixer_refs, y_ref = refs[:N_MIXER_REFS], refs[N_MIXER_REFS]
    blk = pl.program_id(0)
    half = pl.program_id(1)

    def store(section, col, width, value):
        start = pl.multiple_of(half * HALF_WIDTH + (section + col), HEAD_DIM)
        y_ref[:, pl.ds(start, width)] = value

    _MixerHalfBlock(blk, half, sink_ref, *mixer_refs, store).run_all()


def _mixers_first_tile(p, conv_w, sink):
    in_specs = [pl.BlockSpec(memory_space=pltpu.SMEM)] + _mixer_in_specs(
        lambda b, hf: b, lambda b, hf: hf)
    return pl.pallas_call(
        _mixers_first_tile_kernel,
        name="mixers",
        grid=(OUT_TM // BLOCK, 2),
        in_specs=in_specs,
        out_specs=pl.BlockSpec((BLOCK, D_MODEL), lambda b, hf: (b, 0)),
        out_shape=jax.ShapeDtypeStruct((OUT_TM, D_MODEL), jnp.bfloat16),
        compiler_params=pltpu.CompilerParams(
            dimension_semantics=("arbitrary", "arbitrary"),
            vmem_limit_bytes=V7X_VMEM_LIMIT_BYTES),
    )(sink, *([p] * (N_MIXER_REFS - 1)), conv_w)


def _next_tile_block(i, n):
    return jnp.minimum((i + 1) * (OUT_TM // BLOCK) + lax.shift_right_logical(n, 1), N_BLOCKS - 1)


def _out_proj_kernel(sink_ref, y0_ref, x_ref, mu_ref, rstd_ref, g0_ref, b0_ref, w_ref,
                     g1_ref, b1_ref, *refs):
    mixer_refs = refs[:N_MIXER_REFS]
    o_ref, z_ref, zmean_ref, zm2_ref, y_even_ref, y_odd_ref = refs[N_MIXER_REFS:]
    i = pl.program_id(0)
    n = pl.program_id(1)

    @pl.when((i == 0) & (n == 0))
    def _():
        y_even_ref[...] = y0_ref[...]

    @pl.when(n == 0)
    def _():
        zmean_ref[...] = jnp.zeros_like(zmean_ref)
        zm2_ref[...] = jnp.zeros_like(zm2_ref)

    def step(cur_ref, nxt_ref):
        half = n & 1
        row0 = pl.multiple_of(lax.shift_right_logical(n, 1) * BLOCK, BLOCK)

        def store(section, col, width, value):
            start = pl.multiple_of(half * HALF_WIDTH + (section + col), HEAD_DIM)
            nxt_ref[pl.ds(row0, BLOCK), pl.ds(start, width)] = value

        mx = _MixerHalfBlock(_next_tile_block(i, n), half, sink_ref, *mixer_refs, store)

        def dot_piece(k):
            ks = slice(k * OUT_K_PIECE, (k + 1) * OUT_K_PIECE)
            return jnp.dot(cur_ref[:, ks], w_ref[ks, :], preferred_element_type=jnp.float32)

        band_scores = mx.scores()
        heads = []
        y = None
        outs = None
        for k in range(OUT_N_PIECES):
            piece = dot_piece(k)
            y = piece if y is None else y + piece
            for head in OUT_HEAD_SLOTS[k]:
                kg, g = divmod(head, GQA_GROUP)
                heads.append(mx.softmax_head(kg, g, band_scores))
            mx.conv_chunk(k)
            if k == OUT_PV_AFTER_PIECE:
                outs = mx.weighted_values(heads)
            if k == OUT_PV_AFTER_PIECE + 1:
                mx.finish(outs, heads)

        ag0 = DN_ALPHA * g0_ref[...]
        ab0 = DN_ALPHA * b0_ref[...]
        own_lane = lax.broadcasted_iota(jnp.int32, (OUT_ROW_CHUNK, STAT_LANES), 1) == n
        for c in range(OUT_TM // OUT_ROW_CHUNK):
            rows = slice(c * OUT_ROW_CHUNK, (c + 1) * OUT_ROW_CHUNK)
            z = ((x_ref[rows, :] - mu_ref[rows, :]) * rstd_ref[rows, :] * ag0 + ab0) + y[rows, :]
            z_ref[n, rows, :] = z
            zm = jnp.mean(z, axis=-1, keepdims=True)
            zc = z - zm
            zmean_ref[rows, :] = jnp.where(own_lane, zm, zmean_ref[rows, :])
            zm2_ref[rows, :] = jnp.where(own_lane, jnp.sum(zc * zc, axis=-1, keepdims=True),
                                         zm2_ref[rows, :])

    @pl.when((i & 1) == 0)
    def _():
        step(y_even_ref, y_odd_ref)

    @pl.when((i & 1) == 1)
    def _():
        step(y_odd_ref, y_even_ref)

    @pl.when(n == OUT_NT - 1)
    def _():
        used_lane = lax.broadcasted_iota(jnp.int32, (OUT_TM, STAT_LANES), 1) < OUT_NT
        means = zmean_ref[...]
        row_mean = jnp.sum(means, axis=-1, keepdims=True) * (1.0 / OUT_NT)
        dm = jnp.where(used_lane, means - row_mean, 0.0)
        m2 = jnp.sum(zm2_ref[...] + OUT_TN * (dm * dm), axis=-1, keepdims=True)
        zmean_ref[:, 0:1] = row_mean
        zm2_ref[:, 0:1] = lax.rsqrt(m2 * (1.0 / D_MODEL) + LN_EPS)

        def body(c, carry):
            rows = pl.ds(pl.multiple_of(c * OUT_ROW_CHUNK, OUT_ROW_CHUNK), OUT_ROW_CHUNK)
            mean = zmean_ref[rows, 0:1]
            rstd = zm2_ref[rows, 0:1]
            for k in range(OUT_NT):
                cols = slice(k * OUT_TN, (k + 1) * OUT_TN)
                o_ref[rows, cols] = (z_ref[k, rows, :] - mean) * rstd * g1_ref[:, cols] + b1_ref[:, cols]
            return carry

        lax.fori_loop(0, OUT_TM // OUT_ROW_CHUNK, body, 0)


def _out_proj(x2d, mu, rstd, g0, b0, ymix0, w, g1, b1, p, conv_w, sink):
    assert OUT_NT == 2 * (OUT_TM // BLOCK)
    grid = (SEQ // OUT_TM, OUT_NT)
    row_stat = pl.BlockSpec((OUT_TM, 1), lambda i, n: (i, 0))
    col_vec = pl.BlockSpec((1, OUT_TN), lambda i, n: (0, n))
    full_vec = pl.BlockSpec((1, D_MODEL), lambda i, n: (0, 0))
    in_specs = [
        pl.BlockSpec(memory_space=pltpu.SMEM),
        pl.BlockSpec((OUT_TM, D_MODEL), lambda i, n: (0, 0),
                     pipeline_mode=pl.Buffered(1)),
        pl.BlockSpec((OUT_TM, OUT_TN), lambda i, n: (i, n)),
        row_stat, row_stat,
        col_vec, col_vec,
        pl.BlockSpec((D_MODEL, OUT_TN), lambda i, n: (0, n)),
        full_vec, full_vec,
    ] + _mixer_in_specs(_next_tile_block, lambda i, n: n & 1)
    return pl.pallas_call(
        _out_proj_kernel,
        name="out_proj",
        grid=grid,
        in_specs=in_specs,
        out_specs=pl.BlockSpec((OUT_TM, D_MODEL), lambda i, n: (i, 0)),
        out_shape=jax.ShapeDtypeStruct((SEQ, D_MODEL), jnp.float32),
        scratch_shapes=[
            pltpu.VMEM((OUT_NT, OUT_TM, OUT_TN), jnp.float32),
            pltpu.VMEM((OUT_TM, STAT_LANES), jnp.float32),
            pltpu.VMEM((OUT_TM, STAT_LANES), jnp.float32),
            pltpu.VMEM((OUT_TM, D_MODEL), jnp.bfloat16),
            pltpu.VMEM((OUT_TM, D_MODEL), jnp.bfloat16),
        ],
        compiler_params=pltpu.CompilerParams(
            dimension_semantics=("arbitrary", "arbitrary"),
            vmem_limit_bytes=V7X_VMEM_LIMIT_BYTES),
    )(sink, ymix0, x2d, mu, rstd, g0, b0, w, g1, b1, *([p] * (N_MIXER_REFS - 1)), conv_w)


def kernel(x, emb_ln_g, emb_ln_b, w_in, conv_w, sink, w_out, ln_g, ln_b):
    batch, seq, d_model = x.shape
    assert (batch, seq, d_model) == (1, SEQ, D_MODEL)
    assert w_in.shape == (1, D_MODEL, PROJ_WIDTH) and w_out.shape == (1, D_MODEL, D_MODEL)
    x2d = x.reshape(SEQ, D_MODEL)
    g0 = emb_ln_g.reshape(1, D_MODEL)
    b0 = emb_ln_b.reshape(1, D_MODEL)
    g1 = ln_g.reshape(1, D_MODEL)
    b1 = ln_b.reshape(1, D_MODEL)
    h0 = _entry_norm_first_tile(x2d, g0, b0)
    p, mu, rstd, w_out_bf16 = _in_proj(h0, x2d, g0, b0, w_in[0], w_out[0])
    ymix0 = _mixers_first_tile(p, conv_w[0], sink[0])
    out = _out_proj(x2d, mu, rstd, g0, b0, ymix0, w_out_bf16, g1, b1, p, conv_w[0], sink[0])
    return out.reshape(1, SEQ, D_MODEL)
```

```python
import jax
import jax.numpy as jnp
from jax import lax
from jax.experimental import pallas as pl
from jax.experimental.pallas import tpu as pltpu

D_MODEL = 4096
SEQ = 8192
CONV_WIDTH = D_MODEL // 2
ATTN_WIDTH = D_MODEL - CONV_WIDTH
HEAD_DIM = 128
N_Q_HEADS = ATTN_WIDTH // HEAD_DIM
N_KV_HEADS = N_Q_HEADS // 4
GQA_GROUP = N_Q_HEADS // N_KV_HEADS
KV_WIDTH = N_KV_HEADS * HEAD_DIM
WINDOW = 128
BLOCK = 128
N_BLOCKS = SEQ // BLOCK
DN_ALPHA = 2.0 ** 0.25
LN_EPS = 1e-5
NEG_INF = -1e30
LOG2_E = 1.4426950408889634
PROJ_WIDTH = 4 * CONV_WIDTH + ATTN_WIDTH + 2 * KV_WIDTH + ATTN_WIDTH

OFF_CB = 0
OFF_CC = OFF_CB + CONV_WIDTH
OFF_CH = OFF_CC + CONV_WIDTH
OFF_CZ = OFF_CH + CONV_WIDTH
OFF_Q = OFF_CZ + CONV_WIDTH
OFF_K = OFF_Q + ATTN_WIDTH
OFF_V = OFF_K + KV_WIDTH
OFF_AZ = OFF_V + KV_WIDTH

V7X_VMEM_LIMIT_BYTES = 58 * 1024 * 1024

NORM_TM = 256
BF16_ROWS = 16
IN_TM, IN_TN = 1024, 512
IN_LN_STEPS = 16
IN_LN_ROWS = IN_TM // IN_LN_STEPS
WOUT_CAST_ROWS = 32
OUT_TM, OUT_TN = 512, 512
OUT_NT = D_MODEL // OUT_TN
OUT_ROW_CHUNK = 64
OUT_N_PIECES = 8
OUT_K_PIECE = D_MODEL // OUT_N_PIECES
OUT_HEAD_SLOTS = ((0, 1), (2,), (3, 4), (5,), (6,), (7,), (), ())
OUT_PV_AFTER_PIECE = 5
STAT_LANES = 128
assert OUT_NT <= STAT_LANES
HALO_ROWS = 16
CONV_CHUNK = 128
HALF_WIDTH = CONV_WIDTH // 2
assert HALF_WIDTH == ATTN_WIDTH // 2
P_SECTIONS = ("cb", "cc", "ch", "cz", "q", "az")
P_SLAB_WIDTH = len(P_SECTIONS) * HALF_WIDTH
P_KV_START = 2 * P_SLAB_WIDTH
assert P_KV_START + 2 * KV_WIDTH == PROJ_WIDTH
assert HALF_WIDTH // CONV_CHUNK == OUT_N_PIECES == N_Q_HEADS // 2


def _silu(z):
    return z / (1.0 + jnp.exp(-z))


def _row_stats(x):
    mu = jnp.mean(x, axis=-1, keepdims=True)
    xc = x - mu
    return mu, lax.rsqrt(jnp.mean(xc * xc, axis=-1, keepdims=True) + LN_EPS), xc


def _entry_norm_kernel(x_ref, g_ref, b_ref, h_ref):
    g = g_ref[...]
    b = b_ref[...]

    def chunk(c, carry):
        for r in range(IN_LN_ROWS // BF16_ROWS):
            rows = pl.ds(pl.multiple_of(c * IN_LN_ROWS, IN_LN_ROWS) + r * BF16_ROWS, BF16_ROWS)
            _, rstd, xc = _row_stats(x_ref[rows, :])
            h_ref[rows, :] = (xc * rstd * g + b).astype(jnp.bfloat16)
        return carry

    lax.fori_loop(0, NORM_TM // IN_LN_ROWS, chunk, 0)


def _entry_norm_first_tile(x2d, g, b):
    vec = pl.BlockSpec((1, D_MODEL), lambda i: (0, 0))
    return pl.pallas_call(
        _entry_norm_kernel,
        name="entry_norm",
        grid=(IN_TM // NORM_TM,),
        in_specs=[pl.BlockSpec((NORM_TM, D_MODEL), lambda i: (i, 0)), vec, vec],
        out_specs=pl.BlockSpec((NORM_TM, D_MODEL), lambda i: (i, 0)),
        out_shape=jax.ShapeDtypeStruct((IN_TM, D_MODEL), jnp.bfloat16),
        compiler_params=pltpu.CompilerParams(
            dimension_semantics=("arbitrary",),
            vmem_limit_bytes=V7X_VMEM_LIMIT_BYTES),
    )(x2d, g, b)


def _in_proj_kernel(h0_ref, x_ref, g_ref, b_ref, w_ref, wo_ref,
                    p_ref, mu_ref, rstd_ref, wob_ref, h_even_ref, h_odd_ref):
    i = pl.program_id(0)
    j = pl.program_id(1)

    @pl.when((i == 0) & (j == 0))
    def _():
        h_even_ref[...] = h0_ref[...]

    def step(cur_ref, nxt_ref):
        p_ref[...] = jnp.dot(cur_ref[...], w_ref[...].astype(jnp.bfloat16),
                             preferred_element_type=jnp.float32).astype(jnp.bfloat16)
        chunk = jnp.minimum(j, IN_LN_STEPS - 1)
        base = pl.multiple_of(chunk * IN_LN_ROWS, IN_LN_ROWS)
        g = g_ref[...]
        b = b_ref[...]
        for r in range(IN_LN_ROWS // BF16_ROWS):
            rows = slice(r * BF16_ROWS, (r + 1) * BF16_ROWS)
            mu, rstd, xc = _row_stats(x_ref[rows, :])
            mu_ref[rows, :] = mu
            rstd_ref[rows, :] = rstd
            nxt_ref[pl.ds(base + r * BF16_ROWS, BF16_ROWS), :] = (
                (xc * rstd * g + b).astype(jnp.bfloat16))
        wob_ref[...] = wo_ref[...].astype(jnp.bfloat16)

    @pl.when((i & 1) == 0)
    def _():
        step(h_even_ref, h_odd_ref)

    @pl.when((i & 1) == 1)
    def _():
        step(h_odd_ref, h_even_ref)


def _in_proj(h0, x2d, g, b, w_in, w_out):
    n_i, n_j = SEQ // IN_TM, PROJ_WIDTH // IN_TN
    n_chunks = SEQ // IN_LN_ROWS
    n_slabs = D_MODEL // WOUT_CAST_ROWS
    assert n_i * n_j >= n_slabs and n_j >= IN_LN_STEPS

    def next_tile_chunk(i, j):
        return (lax.rem(i + 1, n_i) * IN_LN_STEPS + jnp.minimum(j, IN_LN_STEPS - 1), 0)

    def slab(i, j):
        return (jnp.minimum(i * n_j + j, n_slabs - 1), 0)

    def stored_tile(i, j):
        wide_tiles = CONV_WIDTH // IN_TN
        half_tiles = HALF_WIDTH // IN_TN
        assert (wide_tiles, half_tiles) == (4, 2)
        k_tile, v_tile, az_tile = OFF_K // IN_TN, OFF_V // IN_TN, OFF_AZ // IN_TN
        is_az = j >= az_tile
        section = jnp.where(is_az, len(P_SECTIONS) - 1, lax.shift_right_logical(j, 2))
        t = jnp.where(is_az, j - az_tile, j & 3)
        place = (lax.shift_right_logical(t, 1) * (P_SLAB_WIDTH // IN_TN) + section * half_tiles
                 + (t & 1))
        place = jnp.where(j == k_tile, P_KV_START // IN_TN, place)
        place = jnp.where(j == v_tile, P_KV_START // IN_TN + KV_WIDTH // IN_TN, place)
        return (i, place)

    vec = pl.BlockSpec((1, D_MODEL), lambda i, j: (0, 0))
    return pl.pallas_call(
        _in_proj_kernel,
        name="in_proj",
        grid=(n_i, n_j),
        in_specs=[
            pl.BlockSpec((IN_TM, D_MODEL), lambda i, j: (0, 0), pipeline_mode=pl.Buffered(1)),
            pl.BlockSpec((IN_LN_ROWS, D_MODEL), next_tile_chunk),
            vec, vec,
            pl.BlockSpec((D_MODEL, IN_TN), lambda i, j: (0, j)),
            pl.BlockSpec((WOUT_CAST_ROWS, D_MODEL), slab),
        ],
        out_specs=[
            pl.BlockSpec((IN_TM, IN_TN), stored_tile),
            pl.BlockSpec((IN_LN_ROWS, 1), next_tile_chunk),
            pl.BlockSpec((IN_LN_ROWS, 1), next_tile_chunk),
            pl.BlockSpec((WOUT_CAST_ROWS, D_MODEL), slab),
        ],
        out_shape=[
            jax.ShapeDtypeStruct((SEQ, PROJ_WIDTH), jnp.bfloat16),
            jax.ShapeDtypeStruct((SEQ, 1), jnp.float32),
            jax.ShapeDtypeStruct((SEQ, 1), jnp.float32),
            jax.ShapeDtypeStruct((D_MODEL, D_MODEL), jnp.bfloat16),
        ],
        scratch_shapes=[pltpu.VMEM((IN_TM, D_MODEL), jnp.bfloat16),
                        pltpu.VMEM((IN_TM, D_MODEL), jnp.bfloat16)],
        compiler_params=pltpu.CompilerParams(
            dimension_semantics=("arbitrary", "arbitrary"),
            vmem_limit_bytes=V7X_VMEM_LIMIT_BYTES),
    )(h0, x2d, g, b, w_in, w_out)


class _MixerHalfBlock:
    N_CONV_CHUNKS = HALF_WIDTH // CONV_CHUNK
    N_KV_GROUPS = N_KV_HEADS // 2

    def __init__(self, blk, half, sink_ref, main_ref, halo_prev_ref, halo_next_ref,
                 kv_prev_ref, kv_cur_ref, kv_next_ref, cw_ref, store):
        self.blk, self.half, self.sink_ref, self.store = blk, half, sink_ref, store

        def section(ref, s):
            return ref.at[:, s * HALF_WIDTH:(s + 1) * HALF_WIDTH]

        self.cb_ref, self.cc_ref, self.ch_ref, self.cz_ref, self.q_ref, self.az_ref = (
            section(main_ref, s) for s in range(len(P_SECTIONS)))
        self.ccp_ref, self.chp_ref = section(halo_prev_ref, 1), section(halo_prev_ref, 2)
        self.ccn_ref, self.chn_ref = section(halo_next_ref, 1), section(halo_next_ref, 2)
        self.cw_ref = cw_ref
        half_kv = KV_WIDTH // 2
        k0 = pl.multiple_of(half * half_kv, half_kv)
        v0 = pl.multiple_of(KV_WIDTH + half * half_kv, half_kv)
        kv_refs = (kv_prev_ref, kv_cur_ref, kv_next_ref)
        self.k_refs = tuple(r.at[:, pl.ds(k0, half_kv)] for r in kv_refs)
        self.v_refs = tuple(r.at[:, pl.ds(v0, half_kv)] for r in kv_refs)
        self._mask = None

    def conv_chunk(self, c):
        f32 = jnp.float32
        cols = slice(c * CONV_CHUNK, (c + 1) * CONV_CHUNK)
        row = lax.broadcasted_iota(jnp.int32, (BLOCK, CONV_CHUNK), 0)
        u = self.cc_ref[:, cols].astype(f32) * self.ch_ref[:, cols].astype(f32)
        u_prev = jnp.where(self.blk > 0,
                           self.ccp_ref[HALO_ROWS - 1:HALO_ROWS, cols].astype(f32)
                           * self.chp_ref[HALO_ROWS - 1:HALO_ROWS, cols].astype(f32), 0.0)
        u_next = jnp.where(self.blk < N_BLOCKS - 1,
                           self.ccn_ref[0:1, cols].astype(f32)
                           * self.chn_ref[0:1, cols].astype(f32), 0.0)
        um1 = jnp.where(row == 0, u_prev, pltpu.roll(u, 1, axis=0))
        up1 = jnp.where(row == BLOCK - 1, u_next, pltpu.roll(u, BLOCK - 1, axis=0))
        cw_ref = self.cw_ref
        conv = um1 * cw_ref[0:1, cols] + u * cw_ref[1:2, cols] + up1 * cw_ref[2:3, cols]
        y = self.cb_ref[:, cols].astype(f32) * conv * _silu(self.cz_ref[:, cols].astype(f32))
        self.store(0, c * CONV_CHUNK, CONV_CHUNK, y.astype(jnp.bfloat16))

    def _band_mask(self):
        if self._mask is None:
            band = 3 * BLOCK
            qi = lax.broadcasted_iota(jnp.int32, (BLOCK, band), 0)
            kj = lax.broadcasted_iota(jnp.int32, (BLOCK, band), 1) - BLOCK
            dist_i = jnp.abs(qi - kj)
            k_pos = kj + self.blk * BLOCK
            valid = (dist_i <= WINDOW) & (k_pos >= 0) & (k_pos < SEQ)
            self._mask = (valid, dist_i.astype(jnp.float32) * -LOG2_E)
        return self._mask

    @staticmethod
    def _block_diagonal(blocks):
        zero = jnp.zeros_like(blocks[0])
        return jnp.concatenate([jnp.concatenate([blocks[0], zero], axis=1),
                                jnp.concatenate([zero, blocks[1]], axis=1)], axis=0)

    def scores(self):
        assert self.N_KV_GROUPS == 2
        q = jnp.concatenate(
            [jnp.concatenate(
                [self.q_ref[:, (kg * GQA_GROUP + g) * HEAD_DIM:(kg * GQA_GROUP + g + 1) * HEAD_DIM]
                 for kg in range(self.N_KV_GROUPS)], axis=1)
             for g in range(GQA_GROUP)], axis=0)
        keys = self._block_diagonal(
            [jnp.concatenate([r[:, kg * HEAD_DIM:(kg + 1) * HEAD_DIM] for r in self.k_refs], axis=0)
             for kg in range(self.N_KV_GROUPS)])
        return lax.dot_general(q, keys, (((1,), (1,)), ((), ())),
                               preferred_element_type=jnp.float32)

    def softmax_head(self, kg, g, scores):
        valid, neg_dist_log2 = self._band_mask()
        heads_per_half = N_Q_HEADS // 2
        half_slope = jnp.where(self.half == 1, 2.0 ** (-8.0 * heads_per_half / N_Q_HEADS),
                               1.0).astype(jnp.float32)
        local = kg * GQA_GROUP + g
        slope = half_slope * (2.0 ** (-8.0 * (local + 1) / N_Q_HEADS))
        sink = self.sink_ref[self.half * heads_per_half + local] * LOG2_E
        band = 3 * BLOCK
        s = (scores[g * BLOCK:(g + 1) * BLOCK, kg * band:(kg + 1) * band]
             * (HEAD_DIM ** -0.5 * LOG2_E) + neg_dist_log2 * slope)
        s = jnp.where(valid, s, NEG_INF)
        m = jnp.maximum(jnp.max(s, axis=-1, keepdims=True), sink)
        e = jnp.exp2(s - m)
        return e.astype(jnp.bfloat16), jnp.sum(e, axis=-1, keepdims=True) + jnp.exp2(sink - m)

    def weighted_values(self, heads):
        probs = jnp.concatenate(
            [jnp.concatenate([heads[kg * GQA_GROUP + g][0] for kg in range(self.N_KV_GROUPS)], axis=1)
             for g in range(GQA_GROUP)], axis=0)
        values = self._block_diagonal(
            [jnp.concatenate([r[:, kg * HEAD_DIM:(kg + 1) * HEAD_DIM] for r in self.v_refs], axis=0)
             for kg in range(self.N_KV_GROUPS)])
        return jnp.dot(probs, values, preferred_element_type=jnp.float32)

    def finish(self, outs, heads):
        for kg in range(self.N_KV_GROUPS):
            for g in range(GQA_GROUP):
                local = kg * GQA_GROUP + g
                o = (outs[g * BLOCK:(g + 1) * BLOCK, kg * HEAD_DIM:(kg + 1) * HEAD_DIM]
                     / heads[local][1])
                gate = _silu(
                    self.az_ref[:, local * HEAD_DIM:(local + 1) * HEAD_DIM].astype(jnp.float32))
                self.store(CONV_WIDTH, local * HEAD_DIM, HEAD_DIM, (o * gate).astype(jnp.bfloat16))

    def run_all(self):
        for c in range(self.N_CONV_CHUNKS):
            self.conv_chunk(c)
        scores = self.scores()
        heads = [self.softmax_head(kg, g, scores)
                 for kg in range(self.N_KV_GROUPS) for g in range(GQA_GROUP)]
        self.finish(self.weighted_values(heads), heads)


N_MIXER_REFS = 7


def _mixer_in_specs(blk_of, half_of):
    halo_per_block = BLOCK // HALO_ROWS
    n_halo = SEQ // HALO_ROWS

    def slab(rows, row_fn):
        return pl.BlockSpec((rows, P_SLAB_WIDTH),
                            lambda *idx: (row_fn(blk_of(*idx)), half_of(*idx)))

    def kv(row_fn):
        return pl.BlockSpec((BLOCK, 2 * KV_WIDTH),
                            lambda *idx: (row_fn(blk_of(*idx)), P_KV_START // (2 * KV_WIDTH)))

    cur = lambda b: b
    prev_blk = lambda b: jnp.maximum(b - 1, 0)
    next_blk = lambda b: jnp.minimum(b + 1, N_BLOCKS - 1)
    prev_halo = lambda b: jnp.maximum(b * halo_per_block - 1, 0)
    next_halo = lambda b: jnp.minimum((b + 1) * halo_per_block, n_halo - 1)
    return [
        slab(BLOCK, cur),
        slab(HALO_ROWS, prev_halo),
        slab(HALO_ROWS, next_halo),
        kv(prev_blk), kv(cur), kv(next_blk),
        pl.BlockSpec((3, HALF_WIDTH), lambda *idx: (0, half_of(*idx))),
    ]


def _mixers_first_tile_kernel(sink_ref, *refs):
    mixer_refs, y_ref = refs[:N_MIXER_REFS], refs[N_MIXER_REFS]
    blk = pl.program_id(0)
    half = pl.program_id(1)

    def store(section, col, width, value):
        start = pl.multiple_of(half * HALF_WIDTH + (section + col), HEAD_DIM)
        y_ref[:, pl.ds(start, width)] = value

    _MixerHalfBlock(blk, half, sink_ref, *mixer_refs, store).run_all()


def _mixers_first_tile(p, conv_w, sink):
    in_specs = [pl.BlockSpec(memory_space=pltpu.SMEM)] + _mixer_in_specs(
        lambda b, hf: b, lambda b, hf: hf)
    return pl.pallas_call(
        _mixers_first_tile_kernel,
        name="mixers",
        grid=(OUT_TM // BLOCK, 2),
        in_specs=in_specs,
        out_specs=pl.BlockSpec((BLOCK, D_MODEL), lambda b, hf: (b, 0)),
        out_shape=jax.ShapeDtypeStruct((OUT_TM, D_MODEL), jnp.bfloat16),
        compiler_params=pltpu.CompilerParams(
            dimension_semantics=("arbitrary", "arbitrary"),
            vmem_limit_bytes=V7X_VMEM_LIMIT_BYTES),
    )(sink, *([p] * (N_MIXER_REFS - 1)), conv_w)


def _next_tile_block(i, n):
    return jnp.minimum((i + 1) * (OUT_TM // BLOCK) + lax.shift_right_logical(n, 1), N_BLOCKS - 1)


def _out_proj_kernel(sink_ref, y0_ref, x_ref, mu_ref, rstd_ref, g0_ref, b0_ref, w_ref,
                     g1_ref, b1_ref, *refs):
    mixer_refs = refs[:N_MIXER_REFS]
    o_ref, z_ref, zmean_ref, zm2_ref, y_even_ref, y_odd_ref = refs[N_MIXER_REFS:]
    i = pl.program_id(0)
    n = pl.program_id(1)

    @pl.when((i == 0) & (n == 0))
    def _():
        y_even_ref[...] = y0_ref[...]

    @pl.when(n == 0)
    def _():
        zmean_ref[...] = jnp.zeros_like(zmean_ref)
        zm2_ref[...] = jnp.zeros_like(zm2_ref)

    def step(cur_ref, nxt_ref):
        half = n & 1
        row0 = pl.multiple_of(lax.shift_right_logical(n, 1) * BLOCK, BLOCK)

        def store(section, col, width, value):
            start = pl.multiple_of(half * HALF_WIDTH + (section + col), HEAD_DIM)
            nxt_ref[pl.ds(row0, BLOCK), pl.ds(start, width)] = value

        mx = _MixerHalfBlock(_next_tile_block(i, n), half, sink_ref, *mixer_refs, store)

        def dot_piece(k):
            ks = slice(k * OUT_K_PIECE, (k + 1) * OUT_K_PIECE)
            return jnp.dot(cur_ref[:, ks], w_ref[ks, :], preferred_element_type=jnp.float32)

        ag0 = DN_ALPHA * g0_ref[...]
        ab0 = DN_ALPHA * b0_ref[...]

        def residual_rows(c):
            rows = slice(c * OUT_ROW_CHUNK, (c + 1) * OUT_ROW_CHUNK)
            z_ref[n, rows, :] = (x_ref[rows, :] - mu_ref[rows, :]) * rstd_ref[rows, :] * ag0 + ab0

        n_row_chunks = OUT_TM // OUT_ROW_CHUNK
        band_scores = mx.scores()
        heads = []
        y = None
        outs = None
        for k in range(OUT_N_PIECES):
            piece = dot_piece(k)
            y = piece if y is None else y + piece
            for head in OUT_HEAD_SLOTS[k]:
                kg, g = divmod(head, GQA_GROUP)
                heads.append(mx.softmax_head(kg, g, band_scores))
            mx.conv_chunk(k)
            if k == OUT_PV_AFTER_PIECE:
                outs = mx.weighted_values(heads)
            if k == OUT_PV_AFTER_PIECE + 1:
                mx.finish(outs, heads)
            if k >= OUT_N_PIECES - 2:
                first = (k - (OUT_N_PIECES - 2)) * (n_row_chunks // 2)
                for c in range(first, first + n_row_chunks // 2):
                    residual_rows(c)

        own_lane = lax.broadcasted_iota(jnp.int32, (OUT_ROW_CHUNK, STAT_LANES), 1) == n
        for c in range(n_row_chunks):
            rows = slice(c * OUT_ROW_CHUNK, (c + 1) * OUT_ROW_CHUNK)
            z = z_ref[n, rows, :] + y[rows, :]
            z_ref[n, rows, :] = z
            zm = jnp.mean(z, axis=-1, keepdims=True)
            zc = z - zm
            zmean_ref[rows, :] = jnp.where(own_lane, zm, zmean_ref[rows, :])
            zm2_ref[rows, :] = jnp.where(own_lane, jnp.sum(zc * zc, axis=-1, keepdims=True),
                                         zm2_ref[rows, :])

    @pl.when((i & 1) == 0)
    def _():
        step(y_even_ref, y_odd_ref)

    @pl.when((i & 1) == 1)
    def _():
        step(y_odd_ref, y_even_ref)

    @pl.when(n == OUT_NT - 1)
    def _():
        used_lane = lax.broadcasted_iota(jnp.int32, (OUT_TM, STAT_LANES), 1) < OUT_NT
        means = zmean_ref[...]
        row_mean = jnp.sum(means, axis=-1, keepdims=True) * (1.0 / OUT_NT)
        dm = jnp.where(used_lane, means - row_mean, 0.0)
        m2 = jnp.sum(zm2_ref[...] + OUT_TN * (dm * dm), axis=-1, keepdims=True)
        zmean_ref[:, 0:1] = row_mean
        zm2_ref[:, 0:1] = lax.rsqrt(m2 * (1.0 / D_MODEL) + LN_EPS)

        def body(c, carry):
            rows = pl.ds(pl.multiple_of(c * OUT_ROW_CHUNK, OUT_ROW_CHUNK), OUT_ROW_CHUNK)
            mean = zmean_ref[rows, 0:1]
            rstd = zm2_ref[rows, 0:1]
            for k in range(OUT_NT):
                cols = slice(k * OUT_TN, (k + 1) * OUT_TN)
                o_ref[rows, cols] = (z_ref[k, rows, :] - mean) * rstd * g1_ref[:, cols] + b1_ref[:, cols]
            return carry

        lax.fori_loop(0, OUT_TM // OUT_ROW_CHUNK, body, 0)


def _out_proj(x2d, mu, rstd, g0, b0, ymix0, w, g1, b1, p, conv_w, sink):
    assert OUT_NT == 2 * (OUT_TM // BLOCK)
    grid = (SEQ // OUT_TM, OUT_NT)
    row_stat = pl.BlockSpec((OUT_TM, 1), lambda i, n: (i, 0))
    col_vec = pl.BlockSpec((1, OUT_TN), lambda i, n: (0, n))
    full_vec = pl.BlockSpec((1, D_MODEL), lambda i, n: (0, 0))
    in_specs = [
        pl.BlockSpec(memory_space=pltpu.SMEM),
        pl.BlockSpec((OUT_TM, D_MODEL), lambda i, n: (0, 0),
                     pipeline_mode=pl.Buffered(1)),
        pl.BlockSpec((OUT_TM, OUT_TN), lambda i, n: (i, n)),
        row_stat, row_stat,
        col_vec, col_vec,
        pl.BlockSpec((D_MODEL, OUT_TN), lambda i, n: (0, n)),
        full_vec, full_vec,
    ] + _mixer_in_specs(_next_tile_block, lambda i, n: n & 1)
    return pl.pallas_call(
        _out_proj_kernel,
        name="out_proj",
        grid=grid,
        in_specs=in_specs,
        out_specs=pl.BlockSpec((OUT_TM, D_MODEL), lambda i, n: (i, 0)),
        out_shape=jax.ShapeDtypeStruct((SEQ, D_MODEL), jnp.float32),
        scratch_shapes=[
            pltpu.VMEM((OUT_NT, OUT_TM, OUT_TN), jnp.float32),
            pltpu.VMEM((OUT_TM, STAT_LANES), jnp.float32),
            pltpu.VMEM((OUT_TM, STAT_LANES), jnp.float32),
            pltpu.VMEM((OUT_TM, D_MODEL), jnp.bfloat16),
            pltpu.VMEM((OUT_TM, D_MODEL), jnp.bfloat16),
        ],
        compiler_params=pltpu.CompilerParams(
            dimension_semantics=("arbitrary", "arbitrary"),
            vmem_limit_bytes=V7X_VMEM_LIMIT_BYTES),
    )(sink, ymix0, x2d, mu, rstd, g0, b0, w, g1, b1, *([p] * (N_MIXER_REFS - 1)), conv_w)


def kernel(x, emb_ln_g, emb_ln_b, w_in, conv_w, sink, w_out, ln_g, ln_b):
    batch, seq, d_model = x.shape
    assert (batch, seq, d_model) == (1, SEQ, D_MODEL)
    assert w_in.shape == (1, D_MODEL, PROJ_WIDTH) and w_out.shape == (1, D_MODEL, D_MODEL)
    x2d = x.reshape(SEQ, D_MODEL)
    g0 = emb_ln_g.reshape(1, D_MODEL)
    b0 = emb_ln_b.reshape(1, D_MODEL)
    g1 = ln_g.reshape(1, D_MODEL)
    b1 = ln_b.reshape(1, D_MODEL)
    h0 = _entry_norm_first_tile(x2d, g0, b0)
    p, mu, rstd, w_out_bf16 = _in_proj(h0, x2d, g0, b0, w_in[0], w_out[0])
    ymix0 = _mixers_first_tile(p, conv_w[0], sink[0])
    out = _out_proj(x2d, mu, rstd, g0, b0, ymix0, w_out_bf16, g1, b1, p, conv_w[0], sink[0])
    return out.reshape(1, SEQ, D_MODEL)
```

```python
import jax
import jax.numpy as jnp
from jax import lax
from jax.experimental import pallas as pl
from jax.experimental.pallas import tpu as pltpu

D_MODEL = 4096
SEQ = 8192
CONV_WIDTH = D_MODEL // 2
ATTN_WIDTH = D_MODEL - CONV_WIDTH
HEAD_DIM = 128
N_Q_HEADS = ATTN_WIDTH // HEAD_DIM
N_KV_HEADS = N_Q_HEADS // 4
GQA_GROUP = N_Q_HEADS // N_KV_HEADS
KV_WIDTH = N_KV_HEADS * HEAD_DIM
WINDOW = 128
BLOCK = 128
N_BLOCKS = SEQ // BLOCK
DN_ALPHA = 2.0 ** 0.25
LN_EPS = 1e-5
NEG_INF = -1e30
LOG2_E = 1.4426950408889634
PROJ_WIDTH = 4 * CONV_WIDTH + ATTN_WIDTH + 2 * KV_WIDTH + ATTN_WIDTH

OFF_CB = 0
OFF_CC = OFF_CB + CONV_WIDTH
OFF_CH = OFF_CC + CONV_WIDTH
OFF_CZ = OFF_CH + CONV_WIDTH
OFF_Q = OFF_CZ + CONV_WIDTH
OFF_K = OFF_Q + ATTN_WIDTH
OFF_V = OFF_K + KV_WIDTH
OFF_AZ = OFF_V + KV_WIDTH

V7X_VMEM_LIMIT_BYTES = 58 * 1024 * 1024

NORM_TM = 256
BF16_ROWS = 16
IN_TM, IN_TN = 1024, 512
IN_LN_STEPS = 16
IN_LN_ROWS = IN_TM // IN_LN_STEPS
WOUT_CAST_ROWS = 32
OUT_TM, OUT_TN = 512, 512
OUT_NT = D_MODEL // OUT_TN
OUT_ROW_CHUNK = 64
OUT_N_PIECES = 8
OUT_K_PIECE = D_MODEL // OUT_N_PIECES
OUT_HEAD_SLOTS = ((0, 1), (2,), (3, 4), (5,), (6,), (7,), (), ())
OUT_PV_AFTER_PIECE = 5
STAT_LANES = 128
assert OUT_NT <= STAT_LANES
HALO_ROWS = 16
CONV_CHUNK = 256
OUT_CONV_SLOTS = ((), (0,), (), (1,), (), (2,), (), (3,))
HALF_WIDTH = CONV_WIDTH // 2
assert HALF_WIDTH == ATTN_WIDTH // 2
P_SECTIONS = ("cb", "cc", "ch", "cz", "q", "az")
P_SLAB_WIDTH = len(P_SECTIONS) * HALF_WIDTH
P_KV_START = 2 * P_SLAB_WIDTH
assert P_KV_START + 2 * KV_WIDTH == PROJ_WIDTH
assert len(OUT_HEAD_SLOTS) == len(OUT_CONV_SLOTS) == OUT_N_PIECES
assert sorted(sum(OUT_HEAD_SLOTS, ())) == list(range(N_Q_HEADS // 2))
assert sorted(sum(OUT_CONV_SLOTS, ())) == list(range(HALF_WIDTH // CONV_CHUNK))


def _silu(z):
    return z / (1.0 + jnp.exp(-z))


def _row_stats(x):
    mu = jnp.mean(x, axis=-1, keepdims=True)
    xc = x - mu
    return mu, lax.rsqrt(jnp.mean(xc * xc, axis=-1, keepdims=True) + LN_EPS), xc


def _entry_norm_kernel(x_ref, g_ref, b_ref, h_ref):
    g = g_ref[...]
    b = b_ref[...]

    def chunk(c, carry):
        for r in range(IN_LN_ROWS // BF16_ROWS):
            rows = pl.ds(pl.multiple_of(c * IN_LN_ROWS, IN_LN_ROWS) + r * BF16_ROWS, BF16_ROWS)
            _, rstd, xc = _row_stats(x_ref[rows, :])
            h_ref[rows, :] = (xc * rstd * g + b).astype(jnp.bfloat16)
        return carry

    lax.fori_loop(0, NORM_TM // IN_LN_ROWS, chunk, 0)


def _entry_norm_first_tile(x2d, g, b):
    vec = pl.BlockSpec((1, D_MODEL), lambda i: (0, 0))
    return pl.pallas_call(
        _entry_norm_kernel,
        name="entry_norm",
        grid=(IN_TM // NORM_TM,),
        in_specs=[pl.BlockSpec((NORM_TM, D_MODEL), lambda i: (i, 0)), vec, vec],
        out_specs=pl.BlockSpec((NORM_TM, D_MODEL), lambda i: (i, 0)),
        out_shape=jax.ShapeDtypeStruct((IN_TM, D_MODEL), jnp.bfloat16),
        compiler_params=pltpu.CompilerParams(
            dimension_semantics=("arbitrary",),
            vmem_limit_bytes=V7X_VMEM_LIMIT_BYTES),
    )(x2d, g, b)


def _in_proj_kernel(h0_ref, x_ref, g_ref, b_ref, w_ref, wo_ref,
                    p_ref, mu_ref, rstd_ref, wob_ref, h_even_ref, h_odd_ref):
    i = pl.program_id(0)
    j = pl.program_id(1)

    @pl.when((i == 0) & (j == 0))
    def _():
        h_even_ref[...] = h0_ref[...]

    def step(cur_ref, nxt_ref):
        p_ref[...] = jnp.dot(cur_ref[...], w_ref[...].astype(jnp.bfloat16),
                             preferred_element_type=jnp.float32).astype(jnp.bfloat16)
        chunk = jnp.minimum(j, IN_LN_STEPS - 1)
        base = pl.multiple_of(chunk * IN_LN_ROWS, IN_LN_ROWS)
        g = g_ref[...]
        b = b_ref[...]
        for r in range(IN_LN_ROWS // BF16_ROWS):
            rows = slice(r * BF16_ROWS, (r + 1) * BF16_ROWS)
            mu, rstd, xc = _row_stats(x_ref[rows, :])
            mu_ref[rows, :] = mu
            rstd_ref[rows, :] = rstd
            nxt_ref[pl.ds(base + r * BF16_ROWS, BF16_ROWS), :] = (
                (xc * rstd * g + b).astype(jnp.bfloat16))
        wob_ref[...] = wo_ref[...].astype(jnp.bfloat16)

    @pl.when((i & 1) == 0)
    def _():
        step(h_even_ref, h_odd_ref)

    @pl.when((i & 1) == 1)
    def _():
        step(h_odd_ref, h_even_ref)


def _in_proj(h0, x2d, g, b, w_in, w_out):
    n_i, n_j = SEQ // IN_TM, PROJ_WIDTH // IN_TN
    n_chunks = SEQ // IN_LN_ROWS
    n_slabs = D_MODEL // WOUT_CAST_ROWS
    assert n_i * n_j >= n_slabs and n_j >= IN_LN_STEPS

    def next_tile_chunk(i, j):
        return (lax.rem(i + 1, n_i) * IN_LN_STEPS + jnp.minimum(j, IN_LN_STEPS - 1), 0)

    def slab(i, j):
        return (jnp.minimum(i * n_j + j, n_slabs - 1), 0)

    def stored_tile(i, j):
        wide_tiles = CONV_WIDTH // IN_TN
        half_tiles = HALF_WIDTH // IN_TN
        assert (wide_tiles, half_tiles) == (4, 2)
        k_tile, v_tile, az_tile = OFF_K // IN_TN, OFF_V // IN_TN, OFF_AZ // IN_TN
        is_az = j >= az_tile
        section = jnp.where(is_az, len(P_SECTIONS) - 1, lax.shift_right_logical(j, 2))
        t = jnp.where(is_az, j - az_tile, j & 3)
        place = (lax.shift_right_logical(t, 1) * (P_SLAB_WIDTH // IN_TN) + section * half_tiles
                 + (t & 1))
        place = jnp.where(j == k_tile, P_KV_START // IN_TN, place)
        place = jnp.where(j == v_tile, P_KV_START // IN_TN + KV_WIDTH // IN_TN, place)
        return (i, place)

    vec = pl.BlockSpec((1, D_MODEL), lambda i, j: (0, 0))
    return pl.pallas_call(
        _in_proj_kernel,
        name="in_proj",
        grid=(n_i, n_j),
        in_specs=[
            pl.BlockSpec((IN_TM, D_MODEL), lambda i, j: (0, 0), pipeline_mode=pl.Buffered(1)),
            pl.BlockSpec((IN_LN_ROWS, D_MODEL), next_tile_chunk),
            vec, vec,
            pl.BlockSpec((D_MODEL, IN_TN), lambda i, j: (0, j)),
            pl.BlockSpec((WOUT_CAST_ROWS, D_MODEL), slab),
        ],
        out_specs=[
            pl.BlockSpec((IN_TM, IN_TN), stored_tile),
            pl.BlockSpec((IN_LN_ROWS, 1), next_tile_chunk),
            pl.BlockSpec((IN_LN_ROWS, 1), next_tile_chunk),
            pl.BlockSpec((WOUT_CAST_ROWS, D_MODEL), slab),
        ],
        out_shape=[
            jax.ShapeDtypeStruct((SEQ, PROJ_WIDTH), jnp.bfloat16),
            jax.ShapeDtypeStruct((SEQ, 1), jnp.float32),
            jax.ShapeDtypeStruct((SEQ, 1), jnp.float32),
            jax.ShapeDtypeStruct((D_MODEL, D_MODEL), jnp.bfloat16),
        ],
        scratch_shapes=[pltpu.VMEM((IN_TM, D_MODEL), jnp.bfloat16),
                        pltpu.VMEM((IN_TM, D_MODEL), jnp.bfloat16)],
        compiler_params=pltpu.CompilerParams(
            dimension_semantics=("arbitrary", "arbitrary"),
            vmem_limit_bytes=V7X_VMEM_LIMIT_BYTES),
    )(h0, x2d, g, b, w_in, w_out)


class _MixerHalfBlock:
    N_CONV_CHUNKS = HALF_WIDTH // CONV_CHUNK
    N_KV_GROUPS = N_KV_HEADS // 2

    def __init__(self, blk, half, sink_ref, main_ref, halo_prev_ref, halo_next_ref,
                 kv_prev_ref, kv_cur_ref, kv_next_ref, cw_ref, store):
        self.blk, self.half, self.sink_ref, self.store = blk, half, sink_ref, store

        def section(ref, s):
            return ref.at[:, s * HALF_WIDTH:(s + 1) * HALF_WIDTH]

        self.cb_ref, self.cc_ref, self.ch_ref, self.cz_ref, self.q_ref, self.az_ref = (
            section(main_ref, s) for s in range(len(P_SECTIONS)))
        self.ccp_ref, self.chp_ref = section(halo_prev_ref, 1), section(halo_prev_ref, 2)
        self.ccn_ref, self.chn_ref = section(halo_next_ref, 1), section(halo_next_ref, 2)
        self.cw_ref = cw_ref
        half_kv = KV_WIDTH // 2
        k0 = pl.multiple_of(half * half_kv, half_kv)
        v0 = pl.multiple_of(KV_WIDTH + half * half_kv, half_kv)
        kv_refs = (kv_prev_ref, kv_cur_ref, kv_next_ref)
        self.k_refs = tuple(r.at[:, pl.ds(k0, half_kv)] for r in kv_refs)
        self.v_refs = tuple(r.at[:, pl.ds(v0, half_kv)] for r in kv_refs)
        self._mask = None

    def conv_chunk(self, c):
        f32 = jnp.float32
        cols = slice(c * CONV_CHUNK, (c + 1) * CONV_CHUNK)
        row = lax.broadcasted_iota(jnp.int32, (BLOCK, CONV_CHUNK), 0)
        u = self.cc_ref[:, cols].astype(f32) * self.ch_ref[:, cols].astype(f32)
        u_prev = jnp.where(self.blk > 0,
                           self.ccp_ref[HALO_ROWS - 1:HALO_ROWS, cols].astype(f32)
                           * self.chp_ref[HALO_ROWS - 1:HALO_ROWS, cols].astype(f32), 0.0)
        u_next = jnp.where(self.blk < N_BLOCKS - 1,
                           self.ccn_ref[0:1, cols].astype(f32)
                           * self.chn_ref[0:1, cols].astype(f32), 0.0)
        um1 = jnp.where(row == 0, u_prev, pltpu.roll(u, 1, axis=0))
        up1 = jnp.where(row == BLOCK - 1, u_next, pltpu.roll(u, BLOCK - 1, axis=0))
        cw_ref = self.cw_ref
        conv = um1 * cw_ref[0:1, cols] + u * cw_ref[1:2, cols] + up1 * cw_ref[2:3, cols]
        y = self.cb_ref[:, cols].astype(f32) * conv * _silu(self.cz_ref[:, cols].astype(f32))
        self.store(0, c * CONV_CHUNK, CONV_CHUNK, y.astype(jnp.bfloat16))

    def _band_mask(self):
        if self._mask is None:
            band = 3 * BLOCK
            qi = lax.broadcasted_iota(jnp.int32, (BLOCK, band), 0)
            kj = lax.broadcasted_iota(jnp.int32, (BLOCK, band), 1) - BLOCK
            dist_i = jnp.abs(qi - kj)
            k_pos = kj + self.blk * BLOCK
            valid = (dist_i <= WINDOW) & (k_pos >= 0) & (k_pos < SEQ)
            self._mask = (valid, dist_i.astype(jnp.float32) * -LOG2_E)
        return self._mask

    @staticmethod
    def _block_diagonal(blocks):
        zero = jnp.zeros_like(blocks[0])
        return jnp.concatenate([jnp.concatenate([blocks[0], zero], axis=1),
                                jnp.concatenate([zero, blocks[1]], axis=1)], axis=0)

    def scores(self):
        assert self.N_KV_GROUPS == 2
        q = jnp.concatenate(
            [jnp.concatenate(
                [self.q_ref[:, (kg * GQA_GROUP + g) * HEAD_DIM:(kg * GQA_GROUP + g + 1) * HEAD_DIM]
                 for kg in range(self.N_KV_GROUPS)], axis=1)
             for g in range(GQA_GROUP)], axis=0)
        keys = self._block_diagonal(
            [jnp.concatenate([r[:, kg * HEAD_DIM:(kg + 1) * HEAD_DIM] for r in self.k_refs], axis=0)
             for kg in range(self.N_KV_GROUPS)])
        return lax.dot_general(q, keys, (((1,), (1,)), ((), ())),
                               preferred_element_type=jnp.float32)

    def softmax_head(self, kg, g, scores):
        valid, neg_dist_log2 = self._band_mask()
        heads_per_half = N_Q_HEADS // 2
        half_slope = jnp.where(self.half == 1, 2.0 ** (-8.0 * heads_per_half / N_Q_HEADS),
                               1.0).astype(jnp.float32)
        local = kg * GQA_GROUP + g
        slope = half_slope * (2.0 ** (-8.0 * (local + 1) / N_Q_HEADS))
        sink = self.sink_ref[self.half * heads_per_half + local] * LOG2_E
        band = 3 * BLOCK
        s = (scores[g * BLOCK:(g + 1) * BLOCK, kg * band:(kg + 1) * band]
             * (HEAD_DIM ** -0.5 * LOG2_E) + neg_dist_log2 * slope)
        s = jnp.where(valid, s, NEG_INF)
        m = jnp.maximum(jnp.max(s, axis=-1, keepdims=True), sink)
        e = jnp.exp2(s - m)
        return e.astype(jnp.bfloat16), jnp.sum(e, axis=-1, keepdims=True) + jnp.exp2(sink - m)

    def weighted_values(self, heads):
        probs = jnp.concatenate(
            [jnp.concatenate([heads[kg * GQA_GROUP + g][0] for kg in range(self.N_KV_GROUPS)], axis=1)
             for g in range(GQA_GROUP)], axis=0)
        values = self._block_diagonal(
            [jnp.concatenate([r[:, kg * HEAD_DIM:(kg + 1) * HEAD_DIM] for r in self.v_refs], axis=0)
             for kg in range(self.N_KV_GROUPS)])
        return jnp.dot(probs, values, preferred_element_type=jnp.float32)

    def finish(self, outs, heads):
        for kg in range(self.N_KV_GROUPS):
            for g in range(GQA_GROUP):
                local = kg * GQA_GROUP + g
                o = (outs[g * BLOCK:(g + 1) * BLOCK, kg * HEAD_DIM:(kg + 1) * HEAD_DIM]
                     / heads[local][1])
                gate = _silu(
                    self.az_ref[:, local * HEAD_DIM:(local + 1) * HEAD_DIM].astype(jnp.float32))
                self.store(CONV_WIDTH, local * HEAD_DIM, HEAD_DIM, (o * gate).astype(jnp.bfloat16))

    def run_all(self):
        for c in range(self.N_CONV_CHUNKS):
            self.conv_chunk(c)
        scores = self.scores()
        heads = [self.softmax_head(kg, g, scores)
                 for kg in range(self.N_KV_GROUPS) for g in range(GQA_GROUP)]
        self.finish(self.weighted_values(heads), heads)


N_MIXER_REFS = 7


def _mixer_in_specs(blk_of, half_of):
    halo_per_block = BLOCK // HALO_ROWS
    n_halo = SEQ // HALO_ROWS

    def slab(rows, row_fn):
        return pl.BlockSpec((rows, P_SLAB_WIDTH),
                            lambda *idx: (row_fn(blk_of(*idx)), half_of(*idx)))

    def kv(row_fn):
        return pl.BlockSpec((BLOCK, 2 * KV_WIDTH),
                            lambda *idx: (row_fn(blk_of(*idx)), P_KV_START // (2 * KV_WIDTH)))

    cur = lambda b: b
    prev_blk = lambda b: jnp.maximum(b - 1, 0)
    next_blk = lambda b: jnp.minimum(b + 1, N_BLOCKS - 1)
    prev_halo = lambda b: jnp.maximum(b * halo_per_block - 1, 0)
    next_halo = lambda b: jnp.minimum((b + 1) * halo_per_block, n_halo - 1)
    return [
        slab(BLOCK, cur),
        slab(HALO_ROWS, prev_halo),
        slab(HALO_ROWS, next_halo),
        kv(prev_blk), kv(cur), kv(next_blk),
        pl.BlockSpec((3, HALF_WIDTH), lambda *idx: (0, half_of(*idx))),
    ]


def _mixers_first_tile_kernel(sink_ref, *refs):
    mixer_refs, y_ref = refs[:N_MIXER_REFS], refs[N_MIXER_REFS]
    blk = pl.program_id(0)
    half = pl.program_id(1)

    def store(section, col, width, value):
        start = pl.multiple_of(half * HALF_WIDTH + (section + col), HEAD_DIM)
        y_ref[:, pl.ds(start, width)] = value

    _MixerHalfBlock(blk, half, sink_ref, *mixer_refs, store).run_all()


def _mixers_first_tile(p, conv_w, sink):
    in_specs = [pl.BlockSpec(memory_space=pltpu.SMEM)] + _mixer_in_specs(
        lambda b, hf: b, lambda b, hf: hf)
    return pl.pallas_call(
        _mixers_first_tile_kernel,
        name="mixers",
        grid=(OUT_TM // BLOCK, 2),
        in_specs=in_specs,
        out_specs=pl.BlockSpec((BLOCK, D_MODEL), lambda b, hf: (b, 0)),
        out_shape=jax.ShapeDtypeStruct((OUT_TM, D_MODEL), jnp.bfloat16),
        compiler_params=pltpu.CompilerParams(
            dimension_semantics=("arbitrary", "arbitrary"),
            vmem_limit_bytes=V7X_VMEM_LIMIT_BYTES),
    )(sink, *([p] * (N_MIXER_REFS - 1)), conv_w)


def _next_tile_block(i, n):
    return jnp.minimum((i + 1) * (OUT_TM // BLOCK) + lax.shift_right_logical(n, 1), N_BLOCKS - 1)


def _out_proj_kernel(sink_ref, y0_ref, x_ref, mu_ref, rstd_ref, g0_ref, b0_ref, w_ref,
                     g1_ref, b1_ref, *refs):
    mixer_refs = refs[:N_MIXER_REFS]
    o_ref, z_ref, zmean_ref, zm2_ref, y_even_ref, y_odd_ref = refs[N_MIXER_REFS:]
    i = pl.program_id(0)
    n = pl.program_id(1)

    @pl.when((i == 0) & (n == 0))
    def _():
        y_even_ref[...] = y0_ref[...]

    @pl.when(n == 0)
    def _():
        zmean_ref[...] = jnp.zeros_like(zmean_ref)
        zm2_ref[...] = jnp.zeros_like(zm2_ref)

    def step(cur_ref, nxt_ref):
        half = n & 1
        row0 = pl.multiple_of(lax.shift_right_logical(n, 1) * BLOCK, BLOCK)

        def store(section, col, width, value):
            start = pl.multiple_of(half * HALF_WIDTH + (section + col), HEAD_DIM)
            nxt_ref[pl.ds(row0, BLOCK), pl.ds(start, width)] = value

        mx = _MixerHalfBlock(_next_tile_block(i, n), half, sink_ref, *mixer_refs, store)

        def dot_piece(k):
            ks = slice(k * OUT_K_PIECE, (k + 1) * OUT_K_PIECE)
            return jnp.dot(cur_ref[:, ks], w_ref[ks, :], preferred_element_type=jnp.float32)

        band_scores = mx.scores()
        heads = []
        y = None
        outs = None
        for k in range(OUT_N_PIECES):
            piece = dot_piece(k)
            y = piece if y is None else y + piece
            for head in OUT_HEAD_SLOTS[k]:
                kg, g = divmod(head, GQA_GROUP)
                heads.append(mx.softmax_head(kg, g, band_scores))
            for c in OUT_CONV_SLOTS[k]:
                mx.conv_chunk(c)
            if k == OUT_PV_AFTER_PIECE:
                outs = mx.weighted_values(heads)
            if k == OUT_PV_AFTER_PIECE + 1:
                mx.finish(outs, heads)

        ag0 = DN_ALPHA * g0_ref[...]
        ab0 = DN_ALPHA * b0_ref[...]
        own_lane = lax.broadcasted_iota(jnp.int32, (OUT_ROW_CHUNK, STAT_LANES), 1) == n
        for c in range(OUT_TM // OUT_ROW_CHUNK):
            rows = slice(c * OUT_ROW_CHUNK, (c + 1) * OUT_ROW_CHUNK)
            z = ((x_ref[rows, :] - mu_ref[rows, :]) * rstd_ref[rows, :] * ag0 + ab0) + y[rows, :]
            z_ref[n, rows, :] = z
            zm = jnp.mean(z, axis=-1, keepdims=True)
            zc = z - zm
            zmean_ref[rows, :] = jnp.where(own_lane, zm, zmean_ref[rows, :])
            zm2_ref[rows, :] = jnp.where(own_lane, jnp.sum(zc * zc, axis=-1, keepdims=True),
                                         zm2_ref[rows, :])

    @pl.when((i & 1) == 0)
    def _():
        step(y_even_ref, y_odd_ref)

    @pl.when((i & 1) == 1)
    def _():
        step(y_odd_ref, y_even_ref)

    @pl.when(n == OUT_NT - 1)
    def _():
        used_lane = lax.broadcasted_iota(jnp.int32, (OUT_TM, STAT_LANES), 1) < OUT_NT
        means = zmean_ref[...]
        row_mean = jnp.sum(means, axis=-1, keepdims=True) * (1.0 / OUT_NT)
        dm = jnp.where(used_lane, means - row_mean, 0.0)
        m2 = jnp.sum(zm2_ref[...] + OUT_TN * (dm * dm), axis=-1, keepdims=True)
        zmean_ref[:, 0:1] = row_mean
        zm2_ref[:, 0:1] = lax.rsqrt(m2 * (1.0 / D_MODEL) + LN_EPS)

        def body(c, carry):
            rows = pl.ds(pl.multiple_of(c * OUT_ROW_CHUNK, OUT_ROW_CHUNK), OUT_ROW_CHUNK)
            mean = zmean_ref[rows, 0:1]
            rstd = zm2_ref[rows, 0:1]
            for k in range(OUT_NT):
                cols = slice(k * OUT_TN, (k + 1) * OUT_TN)
                o_ref[rows, cols] = (z_ref[k, rows, :] - mean) * rstd * g1_ref[:, cols] + b1_ref[:, cols]
            return carry

        lax.fori_loop(0, OUT_TM // OUT_ROW_CHUNK, body, 0)


def _out_proj(x2d, mu, rstd, g0, b0, ymix0, w, g1, b1, p, conv_w, sink):
    assert OUT_NT == 2 * (OUT_TM // BLOCK)
    grid = (SEQ // OUT_TM, OUT_NT)
    row_stat = pl.BlockSpec((OUT_TM, 1), lambda i, n: (i, 0))
    col_vec = pl.BlockSpec((1, OUT_TN), lambda i, n: (0, n))
    full_vec = pl.BlockSpec((1, D_MODEL), lambda i, n: (0, 0))
    in_specs = [
        pl.BlockSpec(memory_space=pltpu.SMEM),
        pl.BlockSpec((OUT_TM, D_MODEL), lambda i, n: (0, 0),
                     pipeline_mode=pl.Buffered(1)),
        pl.BlockSpec((OUT_TM, OUT_TN), lambda i, n: (i, n)),
        row_stat, row_stat,
        col_vec, col_vec,
        pl.BlockSpec((D_MODEL, OUT_TN), lambda i, n: (0, n)),
        full_vec, full_vec,
    ] + _mixer_in_specs(_next_tile_block, lambda i, n: n & 1)
    return pl.pallas_call(
        _out_proj_kernel,
        name="out_proj",
        grid=grid,
        in_specs=in_specs,
        out_specs=pl.BlockSpec((OUT_TM, D_MODEL), lambda i, n: (i, 0)),
        out_shape=jax.ShapeDtypeStruct((SEQ, D_MODEL), jnp.float32),
        scratch_shapes=[
            pltpu.VMEM((OUT_NT, OUT_TM, OUT_TN), jnp.float32),
            pltpu.VMEM((OUT_TM, STAT_LANES), jnp.float32),
            pltpu.VMEM((OUT_TM, STAT_LANES), jnp.float32),
            pltpu.VMEM((OUT_TM, D_MODEL), jnp.bfloat16),
            pltpu.VMEM((OUT_TM, D_MODEL), jnp.bfloat16),
        ],
        compiler_params=pltpu.CompilerParams(
            dimension_semantics=("arbitrary", "arbitrary"),
            vmem_limit_bytes=V7X_VMEM_LIMIT_BYTES),
    )(sink, ymix0, x2d, mu, rstd, g0, b0, w, g1, b1, *([p] * (N_MIXER_REFS - 1)), conv_w)


def kernel(x, emb_ln_g, emb_ln_b, w_in, conv_w, sink, w_out, ln_g, ln_b):
    batch, seq, d_model = x.shape
    assert (batch, seq, d_model) == (1, SEQ, D_MODEL)
    assert w_in.shape == (1, D_MODEL, PROJ_WIDTH) and w_out.shape == (1, D_MODEL, D_MODEL)
    x2d = x.reshape(SEQ, D_MODEL)
    g0 = emb_ln_g.reshape(1, D_MODEL)
    b0 = emb_ln_b.reshape(1, D_MODEL)
    g1 = ln_g.reshape(1, D_MODEL)
    b1 = ln_b.reshape(1, D_MODEL)
    h0 = _entry_norm_first_tile(x2d, g0, b0)
    p, mu, rstd, w_out_bf16 = _in_proj(h0, x2d, g0, b0, w_in[0], w_out[0])
    ymix0 = _mixers_first_tile(p, conv_w[0], sink[0])
    out = _out_proj(x2d, mu, rstd, g0, b0, ymix0, w_out_bf16, g1, b1, p, conv_w[0], sink[0])
    return out.reshape(1, SEQ, D_MODEL)
```

```python
import jax
import jax.numpy as jnp
from jax import lax
from jax.experimental import pallas as pl
from jax.experimental.pallas import tpu as pltpu

D_MODEL = 4096
SEQ = 8192
CONV_WIDTH = D_MODEL // 2
ATTN_WIDTH = D_MODEL - CONV_WIDTH
HEAD_DIM = 128
N_Q_HEADS = ATTN_WIDTH // HEAD_DIM
N_KV_HEADS = N_Q_HEADS // 4
GQA_GROUP = N_Q_HEADS // N_KV_HEADS
KV_WIDTH = N_KV_HEADS * HEAD_DIM
WINDOW = 128
BLOCK = 128
N_BLOCKS = SEQ // BLOCK
DN_ALPHA = 2.0 ** 0.25
LN_EPS = 1e-5
NEG_INF = -1e30
LOG2_E = 1.4426950408889634
PROJ_WIDTH = 4 * CONV_WIDTH + ATTN_WIDTH + 2 * KV_WIDTH + ATTN_WIDTH

OFF_CB = 0
OFF_CC = OFF_CB + CONV_WIDTH
OFF_CH = OFF_CC + CONV_WIDTH
OFF_CZ = OFF_CH + CONV_WIDTH
OFF_Q = OFF_CZ + CONV_WIDTH
OFF_K = OFF_Q + ATTN_WIDTH
OFF_V = OFF_K + KV_WIDTH
OFF_AZ = OFF_V + KV_WIDTH

V7X_VMEM_LIMIT_BYTES = 58 * 1024 * 1024

NORM_TM = 256
BF16_ROWS = 16
IN_TM, IN_TN = 1024, 512
IN_LN_STEPS = 16
IN_LN_ROWS = IN_TM // IN_LN_STEPS
WOUT_CAST_ROWS = 32
OUT_TM, OUT_TN = 512, 512
OUT_NT = D_MODEL // OUT_TN
OUT_ROW_CHUNK = 64
OUT_N_PIECES = 8
OUT_K_PIECE = D_MODEL // OUT_N_PIECES
OUT_SCORES_AFTER_PIECE = 0
OUT_HEAD_SLOTS = ((), (0, 1), (2, 3), (4, 5), (6,), (7,), (), ())
OUT_PV_AFTER_PIECE = 5
STAT_LANES = 128
assert OUT_NT <= STAT_LANES
HALO_ROWS = 16
CONV_CHUNK = 128
OUT_CONV_SLOTS = ((0, 1), (2,), (3,), (4,), (5,), (6,), (7,), ())
HALF_WIDTH = CONV_WIDTH // 2
assert HALF_WIDTH == ATTN_WIDTH // 2
P_SECTIONS = ("cb", "cc", "ch", "cz", "q", "az")
P_SLAB_WIDTH = len(P_SECTIONS) * HALF_WIDTH
P_KV_START = 2 * P_SLAB_WIDTH
assert P_KV_START + 2 * KV_WIDTH == PROJ_WIDTH
assert len(OUT_HEAD_SLOTS) == len(OUT_CONV_SLOTS) == OUT_N_PIECES
assert sorted(sum(OUT_HEAD_SLOTS, ())) == list(range(N_Q_HEADS // 2))
assert sorted(sum(OUT_CONV_SLOTS, ())) == list(range(HALF_WIDTH // CONV_CHUNK))


def _silu(z):
    return z / (1.0 + jnp.exp(-z))


def _row_stats(x):
    mu = jnp.mean(x, axis=-1, keepdims=True)
    xc = x - mu
    return mu, lax.rsqrt(jnp.mean(xc * xc, axis=-1, keepdims=True) + LN_EPS), xc


def _entry_norm_kernel(x_ref, g_ref, b_ref, h_ref):
    g = g_ref[...]
    b = b_ref[...]

    def chunk(c, carry):
        for r in range(IN_LN_ROWS // BF16_ROWS):
            rows = pl.ds(pl.multiple_of(c * IN_LN_ROWS, IN_LN_ROWS) + r * BF16_ROWS, BF16_ROWS)
            _, rstd, xc = _row_stats(x_ref[rows, :])
            h_ref[rows, :] = (xc * rstd * g + b).astype(jnp.bfloat16)
        return carry

    lax.fori_loop(0, NORM_TM // IN_LN_ROWS, chunk, 0)


def _entry_norm_first_tile(x2d, g, b):
    vec = pl.BlockSpec((1, D_MODEL), lambda i: (0, 0))
    return pl.pallas_call(
        _entry_norm_kernel,
        name="entry_norm",
        grid=(IN_TM // NORM_TM,),
        in_specs=[pl.BlockSpec((NORM_TM, D_MODEL), lambda i: (i, 0)), vec, vec],
        out_specs=pl.BlockSpec((NORM_TM, D_MODEL), lambda i: (i, 0)),
        out_shape=jax.ShapeDtypeStruct((IN_TM, D_MODEL), jnp.bfloat16),
        compiler_params=pltpu.CompilerParams(
            dimension_semantics=("arbitrary",),
            vmem_limit_bytes=V7X_VMEM_LIMIT_BYTES),
    )(x2d, g, b)


def _in_proj_kernel(h0_ref, x_ref, g_ref, b_ref, w_ref, wo_ref,
                    p_ref, mu_ref, rstd_ref, wob_ref, h_even_ref, h_odd_ref):
    i = pl.program_id(0)
    j = pl.program_id(1)

    @pl.when((i == 0) & (j == 0))
    def _():
        h_even_ref[...] = h0_ref[...]

    def step(cur_ref, nxt_ref):
        p_ref[...] = jnp.dot(cur_ref[...], w_ref[...].astype(jnp.bfloat16),
                             preferred_element_type=jnp.float32).astype(jnp.bfloat16)
        chunk = jnp.minimum(j, IN_LN_STEPS - 1)
        base = pl.multiple_of(chunk * IN_LN_ROWS, IN_LN_ROWS)
        g = g_ref[...]
        b = b_ref[...]
        for r in range(IN_LN_ROWS // BF16_ROWS):
            rows = slice(r * BF16_ROWS, (r + 1) * BF16_ROWS)
            mu, rstd, xc = _row_stats(x_ref[rows, :])
            mu_ref[rows, :] = mu
            rstd_ref[rows, :] = rstd
            nxt_ref[pl.ds(base + r * BF16_ROWS, BF16_ROWS), :] = (
                (xc * rstd * g + b).astype(jnp.bfloat16))
        wob_ref[...] = wo_ref[...].astype(jnp.bfloat16)

    @pl.when((i & 1) == 0)
    def _():
        step(h_even_ref, h_odd_ref)

    @pl.when((i & 1) == 1)
    def _():
        step(h_odd_ref, h_even_ref)


def _in_proj(h0, x2d, g, b, w_in, w_out):
    n_i, n_j = SEQ // IN_TM, PROJ_WIDTH // IN_TN
    n_chunks = SEQ // IN_LN_ROWS
    n_slabs = D_MODEL // WOUT_CAST_ROWS
    assert n_i * n_j >= n_slabs and n_j >= IN_LN_STEPS

    def next_tile_chunk(i, j):
        return (lax.rem(i + 1, n_i) * IN_LN_STEPS + jnp.minimum(j, IN_LN_STEPS - 1), 0)

    def slab(i, j):
        return (jnp.minimum(i * n_j + j, n_slabs - 1), 0)

    def stored_tile(i, j):
        wide_tiles = CONV_WIDTH // IN_TN
        half_tiles = HALF_WIDTH // IN_TN
        assert (wide_tiles, half_tiles) == (4, 2)
        k_tile, v_tile, az_tile = OFF_K // IN_TN, OFF_V // IN_TN, OFF_AZ // IN_TN
        is_az = j >= az_tile
        section = jnp.where(is_az, len(P_SECTIONS) - 1, lax.shift_right_logical(j, 2))
        t = jnp.where(is_az, j - az_tile, j & 3)
        place = (lax.shift_right_logical(t, 1) * (P_SLAB_WIDTH // IN_TN) + section * half_tiles
                 + (t & 1))
        place = jnp.where(j == k_tile, P_KV_START // IN_TN, place)
        place = jnp.where(j == v_tile, P_KV_START // IN_TN + KV_WIDTH // IN_TN, place)
        return (i, place)

    vec = pl.BlockSpec((1, D_MODEL), lambda i, j: (0, 0))
    return pl.pallas_call(
        _in_proj_kernel,
        name="in_proj",
        grid=(n_i, n_j),
        in_specs=[
            pl.BlockSpec((IN_TM, D_MODEL), lambda i, j: (0, 0), pipeline_mode=pl.Buffered(1)),
            pl.BlockSpec((IN_LN_ROWS, D_MODEL), next_tile_chunk),
            vec, vec,
            pl.BlockSpec((D_MODEL, IN_TN), lambda i, j: (0, j)),
            pl.BlockSpec((WOUT_CAST_ROWS, D_MODEL), slab),
        ],
        out_specs=[
            pl.BlockSpec((IN_TM, IN_TN), stored_tile),
            pl.BlockSpec((IN_LN_ROWS, 1), next_tile_chunk),
            pl.BlockSpec((IN_LN_ROWS, 1), next_tile_chunk),
            pl.BlockSpec((WOUT_CAST_ROWS, D_MODEL), slab),
        ],
        out_shape=[
            jax.ShapeDtypeStruct((SEQ, PROJ_WIDTH), jnp.bfloat16),
            jax.ShapeDtypeStruct((SEQ, 1), jnp.float32),
            jax.ShapeDtypeStruct((SEQ, 1), jnp.float32),
            jax.ShapeDtypeStruct((D_MODEL, D_MODEL), jnp.bfloat16),
        ],
        scratch_shapes=[pltpu.VMEM((IN_TM, D_MODEL), jnp.bfloat16),
                        pltpu.VMEM((IN_TM, D_MODEL), jnp.bfloat16)],
        compiler_params=pltpu.CompilerParams(
            dimension_semantics=("arbitrary", "arbitrary"),
            vmem_limit_bytes=V7X_VMEM_LIMIT_BYTES),
    )(h0, x2d, g, b, w_in, w_out)


class _MixerHalfBlock:
    N_CONV_CHUNKS = HALF_WIDTH // CONV_CHUNK
    N_KV_GROUPS = N_KV_HEADS // 2

    def __init__(self, blk, half, sink_ref, main_ref, halo_prev_ref, halo_next_ref,
                 kv_prev_ref, kv_cur_ref, kv_next_ref, cw_ref, store):
        self.blk, self.half, self.sink_ref, self.store = blk, half, sink_ref, store

        def section(ref, s):
            return ref.at[:, s * HALF_WIDTH:(s + 1) * HALF_WIDTH]

        self.cb_ref, self.cc_ref, self.ch_ref, self.cz_ref, self.q_ref, self.az_ref = (
            section(main_ref, s) for s in range(len(P_SECTIONS)))
        self.ccp_ref, self.chp_ref = section(halo_prev_ref, 1), section(halo_prev_ref, 2)
        self.ccn_ref, self.chn_ref = section(halo_next_ref, 1), section(halo_next_ref, 2)
        self.cw_ref = cw_ref
        half_kv = KV_WIDTH // 2
        k0 = pl.multiple_of(half * half_kv, half_kv)
        v0 = pl.multiple_of(KV_WIDTH + half * half_kv, half_kv)
        kv_refs = (kv_prev_ref, kv_cur_ref, kv_next_ref)
        self.k_refs = tuple(r.at[:, pl.ds(k0, half_kv)] for r in kv_refs)
        self.v_refs = tuple(r.at[:, pl.ds(v0, half_kv)] for r in kv_refs)
        self._mask = None

    def conv_chunk(self, c):
        f32 = jnp.float32
        cols = slice(c * CONV_CHUNK, (c + 1) * CONV_CHUNK)
        row = lax.broadcasted_iota(jnp.int32, (BLOCK, CONV_CHUNK), 0)
        u = self.cc_ref[:, cols].astype(f32) * self.ch_ref[:, cols].astype(f32)
        u_prev = jnp.where(self.blk > 0,
                           self.ccp_ref[HALO_ROWS - 1:HALO_ROWS, cols].astype(f32)
                           * self.chp_ref[HALO_ROWS - 1:HALO_ROWS, cols].astype(f32), 0.0)
        u_next = jnp.where(self.blk < N_BLOCKS - 1,
                           self.ccn_ref[0:1, cols].astype(f32)
                           * self.chn_ref[0:1, cols].astype(f32), 0.0)
        um1 = jnp.where(row == 0, u_prev, pltpu.roll(u, 1, axis=0))
        up1 = jnp.where(row == BLOCK - 1, u_next, pltpu.roll(u, BLOCK - 1, axis=0))
        cw_ref = self.cw_ref
        conv = um1 * cw_ref[0:1, cols] + u * cw_ref[1:2, cols] + up1 * cw_ref[2:3, cols]
        y = self.cb_ref[:, cols].astype(f32) * conv * _silu(self.cz_ref[:, cols].astype(f32))
        self.store(0, c * CONV_CHUNK, CONV_CHUNK, y.astype(jnp.bfloat16))

    def _band_mask(self):
        if self._mask is None:
            band = 3 * BLOCK
            qi = lax.broadcasted_iota(jnp.int32, (BLOCK, band), 0)
            kj = lax.broadcasted_iota(jnp.int32, (BLOCK, band), 1) - BLOCK
            dist_i = jnp.abs(qi - kj)
            k_pos = kj + self.blk * BLOCK
            valid = (dist_i <= WINDOW) & (k_pos >= 0) & (k_pos < SEQ)
            self._mask = (valid, dist_i.astype(jnp.float32) * -LOG2_E)
        return self._mask

    @staticmethod
    def _block_diagonal(blocks):
        zero = jnp.zeros_like(blocks[0])
        return jnp.concatenate([jnp.concatenate([blocks[0], zero], axis=1),
                                jnp.concatenate([zero, blocks[1]], axis=1)], axis=0)

    def scores(self):
        assert self.N_KV_GROUPS == 2
        q = jnp.concatenate(
            [jnp.concatenate(
                [self.q_ref[:, (kg * GQA_GROUP + g) * HEAD_DIM:(kg * GQA_GROUP + g + 1) * HEAD_DIM]
                 for kg in range(self.N_KV_GROUPS)], axis=1)
             for g in range(GQA_GROUP)], axis=0)
        keys = self._block_diagonal(
            [jnp.concatenate([r[:, kg * HEAD_DIM:(kg + 1) * HEAD_DIM] for r in self.k_refs], axis=0)
             for kg in range(self.N_KV_GROUPS)])
        return lax.dot_general(q, keys, (((1,), (1,)), ((), ())),
                               preferred_element_type=jnp.float32)

    def softmax_head(self, kg, g, scores):
        valid, neg_dist_log2 = self._band_mask()
        heads_per_half = N_Q_HEADS // 2
        half_slope = jnp.where(self.half == 1, 2.0 ** (-8.0 * heads_per_half / N_Q_HEADS),
                               1.0).astype(jnp.float32)
        local = kg * GQA_GROUP + g
        slope = half_slope * (2.0 ** (-8.0 * (local + 1) / N_Q_HEADS))
        sink = self.sink_ref[self.half * heads_per_half + local] * LOG2_E
        band = 3 * BLOCK
        s = (scores[g * BLOCK:(g + 1) * BLOCK, kg * band:(kg + 1) * band]
             * (HEAD_DIM ** -0.5 * LOG2_E) + neg_dist_log2 * slope)
        s = jnp.where(valid, s, NEG_INF)
        m = jnp.maximum(jnp.max(s, axis=-1, keepdims=True), sink)
        e = jnp.exp2(s - m)
        return e.astype(jnp.bfloat16), jnp.sum(e, axis=-1, keepdims=True) + jnp.exp2(sink - m)

    def weighted_values(self, heads):
        probs = jnp.concatenate(
            [jnp.concatenate([heads[kg * GQA_GROUP + g][0] for kg in range(self.N_KV_GROUPS)], axis=1)
             for g in range(GQA_GROUP)], axis=0)
        values = self._block_diagonal(
            [jnp.concatenate([r[:, kg * HEAD_DIM:(kg + 1) * HEAD_DIM] for r in self.v_refs], axis=0)
             for kg in range(self.N_KV_GROUPS)])
        return jnp.dot(probs, values, preferred_element_type=jnp.float32)

    def finish(self, outs, heads):
        for kg in range(self.N_KV_GROUPS):
            for g in range(GQA_GROUP):
                local = kg * GQA_GROUP + g
                o = (outs[g * BLOCK:(g + 1) * BLOCK, kg * HEAD_DIM:(kg + 1) * HEAD_DIM]
                     / heads[local][1])
                gate = _silu(
                    self.az_ref[:, local * HEAD_DIM:(local + 1) * HEAD_DIM].astype(jnp.float32))
                self.store(CONV_WIDTH, local * HEAD_DIM, HEAD_DIM, (o * gate).astype(jnp.bfloat16))

    def run_all(self):
        for c in range(self.N_CONV_CHUNKS):
            self.conv_chunk(c)
        scores = self.scores()
        heads = [self.softmax_head(kg, g, scores)
                 for kg in range(self.N_KV_GROUPS) for g in range(GQA_GROUP)]
        self.finish(self.weighted_values(heads), heads)


N_MIXER_REFS = 7


def _mixer_in_specs(blk_of, half_of):
    halo_per_block = BLOCK // HALO_ROWS
    n_halo = SEQ // HALO_ROWS

    def slab(rows, row_fn):
        return pl.BlockSpec((rows, P_SLAB_WIDTH),
                            lambda *idx: (row_fn(blk_of(*idx)), half_of(*idx)))

    def kv(row_fn):
        return pl.BlockSpec((BLOCK, 2 * KV_WIDTH),
                            lambda *idx: (row_fn(blk_of(*idx)), P_KV_START // (2 * KV_WIDTH)))

    cur = lambda b: b
    prev_blk = lambda b: jnp.maximum(b - 1, 0)
    next_blk = lambda b: jnp.minimum(b + 1, N_BLOCKS - 1)
    prev_halo = lambda b: jnp.maximum(b * halo_per_block - 1, 0)
    next_halo = lambda b: jnp.minimum((b + 1) * halo_per_block, n_halo - 1)
    return [
        slab(BLOCK, cur),
        slab(HALO_ROWS, prev_halo),
        slab(HALO_ROWS, next_halo),
        kv(prev_blk), kv(cur), kv(next_blk),
        pl.BlockSpec((3, HALF_WIDTH), lambda *idx: (0, half_of(*idx))),
    ]


def _mixers_first_tile_kernel(sink_ref, *refs):
    mixer_refs, y_ref = refs[:N_MIXER_REFS], refs[N_MIXER_REFS]
    blk = pl.program_id(0)
    half = pl.program_id(1)

    def store(section, col, width, value):
        start = pl.multiple_of(half * HALF_WIDTH + (section + col), HEAD_DIM)
        y_ref[:, pl.ds(start, width)] = value

    _MixerHalfBlock(blk, half, sink_ref, *mixer_refs, store).run_all()


def _mixers_first_tile(p, conv_w, sink):
    in_specs = [pl.BlockSpec(memory_space=pltpu.SMEM)] + _mixer_in_specs(
        lambda b, hf: b, lambda b, hf: hf)
    return pl.pallas_call(
        _mixers_first_tile_kernel,
        name="mixers",
        grid=(OUT_TM // BLOCK, 2),
        in_specs=in_specs,
        out_specs=pl.BlockSpec((BLOCK, D_MODEL), lambda b, hf: (b, 0)),
        out_shape=jax.ShapeDtypeStruct((OUT_TM, D_MODEL), jnp.bfloat16),
        compiler_params=pltpu.CompilerParams(
            dimension_semantics=("arbitrary", "arbitrary"),
            vmem_limit_bytes=V7X_VMEM_LIMIT_BYTES),
    )(sink, *([p] * (N_MIXER_REFS - 1)), conv_w)


def _next_tile_block(i, n):
    return jnp.minimum((i + 1) * (OUT_TM // BLOCK) + lax.shift_right_logical(n, 1), N_BLOCKS - 1)


def _out_proj_kernel(sink_ref, y0_ref, x_ref, mu_ref, rstd_ref, g0_ref, b0_ref, w_ref,
                     g1_ref, b1_ref, *refs):
    mixer_refs = refs[:N_MIXER_REFS]
    o_ref, z_ref, zmean_ref, zm2_ref, y_even_ref, y_odd_ref = refs[N_MIXER_REFS:]
    i = pl.program_id(0)
    n = pl.program_id(1)

    @pl.when((i == 0) & (n == 0))
    def _():
        y_even_ref[...] = y0_ref[...]

    @pl.when(n == 0)
    def _():
        zmean_ref[...] = jnp.zeros_like(zmean_ref)
        zm2_ref[...] = jnp.zeros_like(zm2_ref)

    def step(cur_ref, nxt_ref):
        half = n & 1
        row0 = pl.multiple_of(lax.shift_right_logical(n, 1) * BLOCK, BLOCK)

        def store(section, col, width, value):
            start = pl.multiple_of(half * HALF_WIDTH + (section + col), HEAD_DIM)
            nxt_ref[pl.ds(row0, BLOCK), pl.ds(start, width)] = value

        mx = _MixerHalfBlock(_next_tile_block(i, n), half, sink_ref, *mixer_refs, store)

        def dot_piece(k):
            ks = slice(k * OUT_K_PIECE, (k + 1) * OUT_K_PIECE)
            return jnp.dot(cur_ref[:, ks], w_ref[ks, :], preferred_element_type=jnp.float32)

        band_scores = None
        heads = []
        y = None
        outs = None
        for k in range(OUT_N_PIECES):
            piece = dot_piece(k)
            y = piece if y is None else y + piece
            for head in OUT_HEAD_SLOTS[k]:
                kg, g = divmod(head, GQA_GROUP)
                heads.append(mx.softmax_head(kg, g, band_scores))
            for c in OUT_CONV_SLOTS[k]:
                mx.conv_chunk(c)
            if k == OUT_SCORES_AFTER_PIECE:
                band_scores = mx.scores()
            if k == OUT_PV_AFTER_PIECE:
                outs = mx.weighted_values(heads)
            if k == OUT_PV_AFTER_PIECE + 1:
                mx.finish(outs, heads)

        ag0 = DN_ALPHA * g0_ref[...]
        ab0 = DN_ALPHA * b0_ref[...]
        own_lane = lax.broadcasted_iota(jnp.int32, (OUT_ROW_CHUNK, STAT_LANES), 1) == n
        for c in range(OUT_TM // OUT_ROW_CHUNK):
            rows = slice(c * OUT_ROW_CHUNK, (c + 1) * OUT_ROW_CHUNK)
            z = ((x_ref[rows, :] - mu_ref[rows, :]) * rstd_ref[rows, :] * ag0 + ab0) + y[rows, :]
            z_ref[n, rows, :] = z
            zm = jnp.mean(z, axis=-1, keepdims=True)
            zc = z - zm
            zmean_ref[rows, :] = jnp.where(own_lane, zm, zmean_ref[rows, :])
            zm2_ref[rows, :] = jnp.where(own_lane, jnp.sum(zc * zc, axis=-1, keepdims=True),
                                         zm2_ref[rows, :])

    @pl.when((i & 1) == 0)
    def _():
        step(y_even_ref, y_odd_ref)

    @pl.when((i & 1) == 1)
    def _():
        step(y_odd_ref, y_even_ref)

    @pl.when(n == OUT_NT - 1)
    def _():
        used_lane = lax.broadcasted_iota(jnp.int32, (OUT_TM, STAT_LANES), 1) < OUT_NT
        means = zmean_ref[...]
        row_mean = jnp.sum(means, axis=-1, keepdims=True) * (1.0 / OUT_NT)
        dm = jnp.where(used_lane, means - row_mean, 0.0)
        m2 = jnp.sum(zm2_ref[...] + OUT_TN * (dm * dm), axis=-1, keepdims=True)
        zmean_ref[:, 0:1] = row_mean
        zm2_ref[:, 0:1] = lax.rsqrt(m2 * (1.0 / D_MODEL) + LN_EPS)

        def body(c, carry):
            rows = pl.ds(pl.multiple_of(c * OUT_ROW_CHUNK, OUT_ROW_CHUNK), OUT_ROW_CHUNK)
            mean = zmean_ref[rows, 0:1]
            rstd = zm2_ref[rows, 0:1]
            for k in range(OUT_NT):
                cols = slice(k * OUT_TN, (k + 1) * OUT_TN)
                o_ref[rows, cols] = (z_ref[k, rows, :] - mean) * rstd * g1_ref[:, cols] + b1_ref[:, cols]
            return carry

        lax.fori_loop(0, OUT_TM // OUT_ROW_CHUNK, body, 0)


def _out_proj(x2d, mu, rstd, g0, b0, ymix0, w, g1, b1, p, conv_w, sink):
    assert OUT_NT == 2 * (OUT_TM // BLOCK)
    grid = (SEQ // OUT_TM, OUT_NT)
    row_stat = pl.BlockSpec((OUT_TM, 1), lambda i, n: (i, 0))
    col_vec = pl.BlockSpec((1, OUT_TN), lambda i, n: (0, n))
    full_vec = pl.BlockSpec((1, D_MODEL), lambda i, n: (0, 0))
    in_specs = [
        pl.BlockSpec(memory_space=pltpu.SMEM),
        pl.BlockSpec((OUT_TM, D_MODEL), lambda i, n: (0, 0),
                     pipeline_mode=pl.Buffered(1)),
        pl.BlockSpec((OUT_TM, OUT_TN), lambda i, n: (i, n)),
        row_stat, row_stat,
        col_vec, col_vec,
        pl.BlockSpec((D_MODEL, OUT_TN), lambda i, n: (0, n)),
        full_vec, full_vec,
    ] + _mixer_in_specs(_next_tile_block, lambda i, n: n & 1)
    return pl.pallas_call(
        _out_proj_kernel,
        name="out_proj",
        grid=grid,
        in_specs=in_specs,
        out_specs=pl.BlockSpec((OUT_TM, D_MODEL), lambda i, n: (i, 0)),
        out_shape=jax.ShapeDtypeStruct((SEQ, D_MODEL), jnp.float32),
        scratch_shapes=[
            pltpu.VMEM((OUT_NT, OUT_TM, OUT_TN), jnp.float32),
            pltpu.VMEM((OUT_TM, STAT_LANES), jnp.float32),
            pltpu.VMEM((OUT_TM, STAT_LANES), jnp.float32),
            pltpu.VMEM((OUT_TM, D_MODEL), jnp.bfloat16),
            pltpu.VMEM((OUT_TM, D_MODEL), jnp.bfloat16),
        ],
        compiler_params=pltpu.CompilerParams(
            dimension_semantics=("arbitrary", "arbitrary"),
            vmem_limit_bytes=V7X_VMEM_LIMIT_BYTES),
    )(sink, ymix0, x2d, mu, rstd, g0, b0, w, g1, b1, *([p] * (N_MIXER_REFS - 1)), conv_w)


def kernel(x, emb_ln_g, emb_ln_b, w_in, conv_w, sink, w_out, ln_g, ln_b):
    batch, seq, d_model = x.shape
    assert (batch, seq, d_model) == (1, SEQ, D_MODEL)
    assert w_in.shape == (1, D_MODEL, PROJ_WIDTH) and w_out.shape == (1, D_MODEL, D_MODEL)
    x2d = x.reshape(SEQ, D_MODEL)
    g0 = emb_ln_g.reshape(1, D_MODEL)
    b0 = emb_ln_b.reshape(1, D_MODEL)
    g1 = ln_g.reshape(1, D_MODEL)
    b1 = ln_b.reshape(1, D_MODEL)
    h0 = _entry_norm_first_tile(x2d, g0, b0)
    p, mu, rstd, w_out_bf16 = _in_proj(h0, x2d, g0, b0, w_in[0], w_out[0])
    ymix0 = _mixers_first_tile(p, conv_w[0], sink[0])
    out = _out_proj(x2d, mu, rstd, g0, b0, ymix0, w_out_bf16, g1, b1, p, conv_w[0], sink[0])
    return out.reshape(1, SEQ, D_MODEL)
```

```python
import jax
import jax.numpy as jnp
from jax import lax
from jax.experimental import pallas as pl
from jax.experimental.pallas import tpu as pltpu

D_MODEL = 4096
SEQ = 8192
CONV_WIDTH = D_MODEL // 2
ATTN_WIDTH = D_MODEL - CONV_WIDTH
HEAD_DIM = 128
N_Q_HEADS = ATTN_WIDTH // HEAD_DIM
N_KV_HEADS = N_Q_HEADS // 4
GQA_GROUP = N_Q_HEADS // N_KV_HEADS
KV_WIDTH = N_KV_HEADS * HEAD_DIM
WINDOW = 128
BLOCK = 128
N_BLOCKS = SEQ // BLOCK
DN_ALPHA = 2.0 ** 0.25
LN_EPS = 1e-5
NEG_INF = -1e30
LOG2_E = 1.4426950408889634
PROJ_WIDTH = 4 * CONV_WIDTH + ATTN_WIDTH + 2 * KV_WIDTH + ATTN_WIDTH

OFF_CB = 0
OFF_CC = OFF_CB + CONV_WIDTH
OFF_CH = OFF_CC + CONV_WIDTH
OFF_CZ = OFF_CH + CONV_WIDTH
OFF_Q = OFF_CZ + CONV_WIDTH
OFF_K = OFF_Q + ATTN_WIDTH
OFF_V = OFF_K + KV_WIDTH
OFF_AZ = OFF_V + KV_WIDTH

V7X_VMEM_LIMIT_BYTES = 58 * 1024 * 1024

NORM_TM = 256
BF16_ROWS = 16
IN_TM, IN_TN = 1024, 512
IN_LN_STEPS = 16
IN_LN_ROWS = IN_TM // IN_LN_STEPS
WOUT_CAST_ROWS = 32
OUT_TM, OUT_TN = 512, 512
OUT_NT = D_MODEL // OUT_TN
OUT_ROW_CHUNK = 64
OUT_N_PIECES = 8
OUT_K_PIECE = D_MODEL // OUT_N_PIECES
OUT_SCORES_AFTER_PIECE = 0
OUT_HEAD_SLOTS = ((), (0, 1), (2, 3), (4, 5), (6,), (7,), (), ())
OUT_PV_AFTER_PIECE = 5
assert all(not OUT_HEAD_SLOTS[k] for k in range(OUT_SCORES_AFTER_PIECE + 1))
assert not any(OUT_HEAD_SLOTS[OUT_PV_AFTER_PIECE + 1:]) and OUT_PV_AFTER_PIECE + 1 < OUT_N_PIECES
STAT_LANES = 128
assert OUT_NT <= STAT_LANES
HALO_ROWS = 16
CONV_CHUNK = 128
OUT_CONV_SLOTS = ((0, 1), (2,), (3,), (4,), (5,), (6,), (7,), ())
HALF_WIDTH = CONV_WIDTH // 2
assert HALF_WIDTH == ATTN_WIDTH // 2
P_SECTIONS = ("cb", "cc", "ch", "cz", "q", "az")
P_SLAB_WIDTH = len(P_SECTIONS) * HALF_WIDTH
P_KV_START = 2 * P_SLAB_WIDTH
assert P_KV_START + 2 * KV_WIDTH == PROJ_WIDTH
assert len(OUT_HEAD_SLOTS) == len(OUT_CONV_SLOTS) == OUT_N_PIECES
assert sorted(sum(OUT_HEAD_SLOTS, ())) == list(range(N_Q_HEADS // 2))
assert sorted(sum(OUT_CONV_SLOTS, ())) == list(range(HALF_WIDTH // CONV_CHUNK))


def _silu(z):
    return z / (1.0 + jnp.exp(-z))


def _row_stats(x):
    mu = jnp.mean(x, axis=-1, keepdims=True)
    xc = x - mu
    return mu, lax.rsqrt(jnp.mean(xc * xc, axis=-1, keepdims=True) + LN_EPS), xc


def _entry_norm_kernel(x_ref, g_ref, b_ref, h_ref):
    g = g_ref[...]
    b = b_ref[...]

    def chunk(c, carry):
        for r in range(IN_LN_ROWS // BF16_ROWS):
            rows = pl.ds(pl.multiple_of(c * IN_LN_ROWS, IN_LN_ROWS) + r * BF16_ROWS, BF16_ROWS)
            _, rstd, xc = _row_stats(x_ref[rows, :])
            h_ref[rows, :] = (xc * rstd * g + b).astype(jnp.bfloat16)
        return carry

    lax.fori_loop(0, NORM_TM // IN_LN_ROWS, chunk, 0)


def _entry_norm_first_tile(x2d, g, b):
    vec = pl.BlockSpec((1, D_MODEL), lambda i: (0, 0))
    return pl.pallas_call(
        _entry_norm_kernel,
        name="entry_norm",
        grid=(IN_TM // NORM_TM,),
        in_specs=[pl.BlockSpec((NORM_TM, D_MODEL), lambda i: (i, 0)), vec, vec],
        out_specs=pl.BlockSpec((NORM_TM, D_MODEL), lambda i: (i, 0)),
        out_shape=jax.ShapeDtypeStruct((IN_TM, D_MODEL), jnp.bfloat16),
        compiler_params=pltpu.CompilerParams(
            dimension_semantics=("arbitrary",),
            vmem_limit_bytes=V7X_VMEM_LIMIT_BYTES),
    )(x2d, g, b)


def _in_proj_kernel(h0_ref, x_ref, g_ref, b_ref, w_ref, wo_ref,
                    p_ref, mu_ref, rstd_ref, wob_ref, h_even_ref, h_odd_ref):
    i = pl.program_id(0)
    j = pl.program_id(1)

    @pl.when((i == 0) & (j == 0))
    def _():
        h_even_ref[...] = h0_ref[...]

    def step(cur_ref, nxt_ref):
        p_ref[...] = jnp.dot(cur_ref[...], w_ref[...].astype(jnp.bfloat16),
                             preferred_element_type=jnp.float32).astype(jnp.bfloat16)
        chunk = jnp.minimum(j, IN_LN_STEPS - 1)
        base = pl.multiple_of(chunk * IN_LN_ROWS, IN_LN_ROWS)
        g = g_ref[...]
        b = b_ref[...]
        for r in range(IN_LN_ROWS // BF16_ROWS):
            rows = slice(r * BF16_ROWS, (r + 1) * BF16_ROWS)
            mu, rstd, xc = _row_stats(x_ref[rows, :])
            mu_ref[rows, :] = mu
            rstd_ref[rows, :] = rstd
            nxt_ref[pl.ds(base + r * BF16_ROWS, BF16_ROWS), :] = (
                (xc * rstd * g + b).astype(jnp.bfloat16))
        wob_ref[...] = wo_ref[...].astype(jnp.bfloat16)

    @pl.when((i & 1) == 0)
    def _():
        step(h_even_ref, h_odd_ref)

    @pl.when((i & 1) == 1)
    def _():
        step(h_odd_ref, h_even_ref)


def _in_proj(h0, x2d, g, b, w_in, w_out):
    n_i, n_j = SEQ // IN_TM, PROJ_WIDTH // IN_TN
    n_chunks = SEQ // IN_LN_ROWS
    n_slabs = D_MODEL // WOUT_CAST_ROWS
    assert n_i * n_j >= n_slabs and n_j >= IN_LN_STEPS

    def next_tile_chunk(i, j):
        return (lax.rem(i + 1, n_i) * IN_LN_STEPS + jnp.minimum(j, IN_LN_STEPS - 1), 0)

    def slab(i, j):
        return (jnp.minimum(i * n_j + j, n_slabs - 1), 0)

    def stored_tile(i, j):
        wide_tiles = CONV_WIDTH // IN_TN
        half_tiles = HALF_WIDTH // IN_TN
        assert (wide_tiles, half_tiles) == (4, 2)
        k_tile, v_tile, az_tile = OFF_K // IN_TN, OFF_V // IN_TN, OFF_AZ // IN_TN
        is_az = j >= az_tile
        section = jnp.where(is_az, len(P_SECTIONS) - 1, lax.shift_right_logical(j, 2))
        t = jnp.where(is_az, j - az_tile, j & 3)
        place = (lax.shift_right_logical(t, 1) * (P_SLAB_WIDTH // IN_TN) + section * half_tiles
                 + (t & 1))
        place = jnp.where(j == k_tile, P_KV_START // IN_TN, place)
        place = jnp.where(j == v_tile, P_KV_START // IN_TN + KV_WIDTH // IN_TN, place)
        return (i, place)

    vec = pl.BlockSpec((1, D_MODEL), lambda i, j: (0, 0))
    return pl.pallas_call(
        _in_proj_kernel,
        name="in_proj",
        grid=(n_i, n_j),
        in_specs=[
            pl.BlockSpec((IN_TM, D_MODEL), lambda i, j: (0, 0), pipeline_mode=pl.Buffered(1)),
            pl.BlockSpec((IN_LN_ROWS, D_MODEL), next_tile_chunk),
            vec, vec,
            pl.BlockSpec((D_MODEL, IN_TN), lambda i, j: (0, j)),
            pl.BlockSpec((WOUT_CAST_ROWS, D_MODEL), slab),
        ],
        out_specs=[
            pl.BlockSpec((IN_TM, IN_TN), stored_tile),
            pl.BlockSpec((IN_LN_ROWS, 1), next_tile_chunk),
            pl.BlockSpec((IN_LN_ROWS, 1), next_tile_chunk),
            pl.BlockSpec((WOUT_CAST_ROWS, D_MODEL), slab),
        ],
        out_shape=[
            jax.ShapeDtypeStruct((SEQ, PROJ_WIDTH), jnp.bfloat16),
            jax.ShapeDtypeStruct((SEQ, 1), jnp.float32),
            jax.ShapeDtypeStruct((SEQ, 1), jnp.float32),
            jax.ShapeDtypeStruct((D_MODEL, D_MODEL), jnp.bfloat16),
        ],
        scratch_shapes=[pltpu.VMEM((IN_TM, D_MODEL), jnp.bfloat16),
                        pltpu.VMEM((IN_TM, D_MODEL), jnp.bfloat16)],
        compiler_params=pltpu.CompilerParams(
            dimension_semantics=("arbitrary", "arbitrary"),
            vmem_limit_bytes=V7X_VMEM_LIMIT_BYTES),
    )(h0, x2d, g, b, w_in, w_out)


class _MixerHalfBlock:
    N_CONV_CHUNKS = HALF_WIDTH // CONV_CHUNK
    N_KV_GROUPS = N_KV_HEADS // 2

    def __init__(self, blk, half, sink_ref, main_ref, halo_prev_ref, halo_next_ref,
                 kv_prev_ref, kv_cur_ref, kv_next_ref, cw_ref, store):
        self.blk, self.half, self.sink_ref, self.store = blk, half, sink_ref, store

        def section(ref, s):
            return ref.at[:, s * HALF_WIDTH:(s + 1) * HALF_WIDTH]

        self.cb_ref, self.cc_ref, self.ch_ref, self.cz_ref, self.q_ref, self.az_ref = (
            section(main_ref, s) for s in range(len(P_SECTIONS)))
        self.ccp_ref, self.chp_ref = section(halo_prev_ref, 1), section(halo_prev_ref, 2)
        self.ccn_ref, self.chn_ref = section(halo_next_ref, 1), section(halo_next_ref, 2)
        self.cw_ref = cw_ref
        half_kv = KV_WIDTH // 2
        k0 = pl.multiple_of(half * half_kv, half_kv)
        v0 = pl.multiple_of(KV_WIDTH + half * half_kv, half_kv)
        kv_refs = (kv_prev_ref, kv_cur_ref, kv_next_ref)
        self.k_refs = tuple(r.at[:, pl.ds(k0, half_kv)] for r in kv_refs)
        self.v_refs = tuple(r.at[:, pl.ds(v0, half_kv)] for r in kv_refs)
        self._mask = None

    def conv_chunk(self, c):
        f32 = jnp.float32
        cols = slice(c * CONV_CHUNK, (c + 1) * CONV_CHUNK)
        row = lax.broadcasted_iota(jnp.int32, (BLOCK, CONV_CHUNK), 0)
        u = self.cc_ref[:, cols].astype(f32) * self.ch_ref[:, cols].astype(f32)
        u_prev = jnp.where(self.blk > 0,
                           self.ccp_ref[HALO_ROWS - 1:HALO_ROWS, cols].astype(f32)
                           * self.chp_ref[HALO_ROWS - 1:HALO_ROWS, cols].astype(f32), 0.0)
        u_next = jnp.where(self.blk < N_BLOCKS - 1,
                           self.ccn_ref[0:1, cols].astype(f32)
                           * self.chn_ref[0:1, cols].astype(f32), 0.0)
        um1 = jnp.where(row == 0, u_prev, pltpu.roll(u, 1, axis=0))
        up1 = jnp.where(row == BLOCK - 1, u_next, pltpu.roll(u, BLOCK - 1, axis=0))
        cw_ref = self.cw_ref
        conv = um1 * cw_ref[0:1, cols] + u * cw_ref[1:2, cols] + up1 * cw_ref[2:3, cols]
        y = self.cb_ref[:, cols].astype(f32) * conv * _silu(self.cz_ref[:, cols].astype(f32))
        self.store(0, c * CONV_CHUNK, CONV_CHUNK, y.astype(jnp.bfloat16))

    def _band_mask(self):
        if self._mask is None:
            band = 3 * BLOCK
            qi = lax.broadcasted_iota(jnp.int32, (BLOCK, band), 0)
            kj = lax.broadcasted_iota(jnp.int32, (BLOCK, band), 1) - BLOCK
            dist_i = jnp.abs(qi - kj)
            k_pos = kj + self.blk * BLOCK
            valid = (dist_i <= WINDOW) & (k_pos >= 0) & (k_pos < SEQ)
            self._mask = (valid, dist_i.astype(jnp.float32) * -LOG2_E)
        return self._mask

    @staticmethod
    def _block_diagonal(blocks):
        zero = jnp.zeros_like(blocks[0])
        return jnp.concatenate([jnp.concatenate([blocks[0], zero], axis=1),
                                jnp.concatenate([zero, blocks[1]], axis=1)], axis=0)

    def scores(self):
        assert self.N_KV_GROUPS == 2
        q = jnp.concatenate(
            [jnp.concatenate(
                [self.q_ref[:, (kg * GQA_GROUP + g) * HEAD_DIM:(kg * GQA_GROUP + g + 1) * HEAD_DIM]
                 for kg in range(self.N_KV_GROUPS)], axis=1)
             for g in range(GQA_GROUP)], axis=0)
        keys = self._block_diagonal(
            [jnp.concatenate([r[:, kg * HEAD_DIM:(kg + 1) * HEAD_DIM] for r in self.k_refs], axis=0)
             for kg in range(self.N_KV_GROUPS)])
        return lax.dot_general(q, keys, (((1,), (1,)), ((), ())),
                               preferred_element_type=jnp.float32)

    def softmax_head(self, kg, g, scores):
        valid, neg_dist_log2 = self._band_mask()
        heads_per_half = N_Q_HEADS // 2
        half_slope = jnp.where(self.half == 1, 2.0 ** (-8.0 * heads_per_half / N_Q_HEADS),
                               1.0).astype(jnp.float32)
        local = kg * GQA_GROUP + g
        slope = half_slope * (2.0 ** (-8.0 * (local + 1) / N_Q_HEADS))
        sink = self.sink_ref[self.half * heads_per_half + local] * LOG2_E
        band = 3 * BLOCK
        s = (scores[g * BLOCK:(g + 1) * BLOCK, kg * band:(kg + 1) * band]
             * (HEAD_DIM ** -0.5 * LOG2_E) + neg_dist_log2 * slope)
        s = jnp.where(valid, s, NEG_INF)
        m = jnp.maximum(jnp.max(s, axis=-1, keepdims=True), sink)
        e = jnp.exp2(s - m)
        return e.astype(jnp.bfloat16), jnp.sum(e, axis=-1, keepdims=True) + jnp.exp2(sink - m)

    def weighted_values(self, heads):
        probs = jnp.concatenate(
            [jnp.concatenate([heads[kg * GQA_GROUP + g][0] for kg in range(self.N_KV_GROUPS)], axis=1)
             for g in range(GQA_GROUP)], axis=0)
        values = self._block_diagonal(
            [jnp.concatenate([r[:, kg * HEAD_DIM:(kg + 1) * HEAD_DIM] for r in self.v_refs], axis=0)
             for kg in range(self.N_KV_GROUPS)])
        return jnp.dot(probs, values, preferred_element_type=jnp.float32)

    def finish(self, outs, heads):
        for kg in range(self.N_KV_GROUPS):
            for g in range(GQA_GROUP):
                local = kg * GQA_GROUP + g
                o = (outs[g * BLOCK:(g + 1) * BLOCK, kg * HEAD_DIM:(kg + 1) * HEAD_DIM]
                     / heads[local][1])
                gate = _silu(
                    self.az_ref[:, local * HEAD_DIM:(local + 1) * HEAD_DIM].astype(jnp.float32))
                self.store(CONV_WIDTH, local * HEAD_DIM, HEAD_DIM, (o * gate).astype(jnp.bfloat16))

    def run_all(self):
        for c in range(self.N_CONV_CHUNKS):
            self.conv_chunk(c)
        scores = self.scores()
        heads = [self.softmax_head(kg, g, scores)
                 for kg in range(self.N_KV_GROUPS) for g in range(GQA_GROUP)]
        self.finish(self.weighted_values(heads), heads)


N_MIXER_REFS = 7


def _mixer_in_specs(blk_of, half_of):
    halo_per_block = BLOCK // HALO_ROWS
    n_halo = SEQ // HALO_ROWS

    def slab(rows, row_fn):
        return pl.BlockSpec((rows, P_SLAB_WIDTH),
                            lambda *idx: (row_fn(blk_of(*idx)), half_of(*idx)))

    def kv(row_fn):
        return pl.BlockSpec((BLOCK, 2 * KV_WIDTH),
                            lambda *idx: (row_fn(blk_of(*idx)), P_KV_START // (2 * KV_WIDTH)))

    cur = lambda b: b
    prev_blk = lambda b: jnp.maximum(b - 1, 0)
    next_blk = lambda b: jnp.minimum(b + 1, N_BLOCKS - 1)
    prev_halo = lambda b: jnp.maximum(b * halo_per_block - 1, 0)
    next_halo = lambda b: jnp.minimum((b + 1) * halo_per_block, n_halo - 1)
    return [
        slab(BLOCK, cur),
        slab(HALO_ROWS, prev_halo),
        slab(HALO_ROWS, next_halo),
        kv(prev_blk), kv(cur), kv(next_blk),
        pl.BlockSpec((3, HALF_WIDTH), lambda *idx: (0, half_of(*idx))),
    ]


def _mixers_first_tile_kernel(sink_ref, *refs):
    mixer_refs, y_ref = refs[:N_MIXER_REFS], refs[N_MIXER_REFS]
    blk = pl.program_id(0)
    half = pl.program_id(1)

    def store(section, col, width, value):
        start = pl.multiple_of(half * HALF_WIDTH + (section + col), HEAD_DIM)
        y_ref[:, pl.ds(start, width)] = value

    _MixerHalfBlock(blk, half, sink_ref, *mixer_refs, store).run_all()


def _mixers_first_tile(p, conv_w, sink):
    in_specs = [pl.BlockSpec(memory_space=pltpu.SMEM)] + _mixer_in_specs(
        lambda b, hf: b, lambda b, hf: hf)
    return pl.pallas_call(
        _mixers_first_tile_kernel,
        name="mixers",
        grid=(OUT_TM // BLOCK, 2),
        in_specs=in_specs,
        out_specs=pl.BlockSpec((BLOCK, D_MODEL), lambda b, hf: (b, 0)),
        out_shape=jax.ShapeDtypeStruct((OUT_TM, D_MODEL), jnp.bfloat16),
        compiler_params=pltpu.CompilerParams(
            dimension_semantics=("arbitrary", "arbitrary"),
            vmem_limit_bytes=V7X_VMEM_LIMIT_BYTES),
    )(sink, *([p] * (N_MIXER_REFS - 1)), conv_w)


def _next_tile_block(i, n):
    return jnp.minimum((i + 1) * (OUT_TM // BLOCK) + lax.shift_right_logical(n, 1), N_BLOCKS - 1)


def _out_proj_kernel(sink_ref, y0_ref, x_ref, mu_ref, rstd_ref, g0_ref, b0_ref, w_ref,
                     g1_ref, b1_ref, *refs):
    mixer_refs = refs[:N_MIXER_REFS]
    o_ref, z_ref, zmean_ref, zm2_ref, y_even_ref, y_odd_ref = refs[N_MIXER_REFS:]
    i = pl.program_id(0)
    n = pl.program_id(1)

    @pl.when((i == 0) & (n == 0))
    def _():
        y_even_ref[...] = y0_ref[...]

    @pl.when(n == 0)
    def _():
        zmean_ref[...] = jnp.zeros_like(zmean_ref)
        zm2_ref[...] = jnp.zeros_like(zm2_ref)

    def step(cur_ref, nxt_ref):
        half = n & 1
        row0 = pl.multiple_of(lax.shift_right_logical(n, 1) * BLOCK, BLOCK)

        def store(section, col, width, value):
            start = pl.multiple_of(half * HALF_WIDTH + (section + col), HEAD_DIM)
            nxt_ref[pl.ds(row0, BLOCK), pl.ds(start, width)] = value

        mx = _MixerHalfBlock(_next_tile_block(i, n), half, sink_ref, *mixer_refs, store)

        def dot_piece(k):
            ks = slice(k * OUT_K_PIECE, (k + 1) * OUT_K_PIECE)
            return jnp.dot(cur_ref[:, ks], w_ref[ks, :], preferred_element_type=jnp.float32)

        band_scores = None
        heads = []
        y = None
        outs = None
        for k in range(OUT_N_PIECES):
            piece = dot_piece(k)
            y = piece if y is None else y + piece
            for head in OUT_HEAD_SLOTS[k]:
                kg, g = divmod(head, GQA_GROUP)
                heads.append(mx.softmax_head(kg, g, band_scores))
            for c in OUT_CONV_SLOTS[k]:
                mx.conv_chunk(c)
            if k == OUT_SCORES_AFTER_PIECE:
                band_scores = mx.scores()
            if k == OUT_PV_AFTER_PIECE:
                outs = mx.weighted_values(heads)
            if k == OUT_PV_AFTER_PIECE + 1:
                mx.finish(outs, heads)

        ag0 = DN_ALPHA * g0_ref[...]
        ab0 = DN_ALPHA * b0_ref[...]
        own_lane = lax.broadcasted_iota(jnp.int32, (OUT_ROW_CHUNK, STAT_LANES), 1) == n
        for c in range(OUT_TM // OUT_ROW_CHUNK):
            rows = slice(c * OUT_ROW_CHUNK, (c + 1) * OUT_ROW_CHUNK)
            z = ((x_ref[rows, :] - mu_ref[rows, :]) * rstd_ref[rows, :] * ag0 + ab0) + y[rows, :]
            z_ref[n, rows, :] = z
            zm = jnp.mean(z, axis=-1, keepdims=True)
            zc = z - zm
            zmean_ref[rows, :] = jnp.where(own_lane, zm, zmean_ref[rows, :])
            zm2_ref[rows, :] = jnp.where(own_lane, jnp.sum(zc * zc, axis=-1, keepdims=True),
                                         zm2_ref[rows, :])

    @pl.when((i & 1) == 0)
    def _():
        step(y_even_ref, y_odd_ref)

    @pl.when((i & 1) == 1)
    def _():
        step(y_odd_ref, y_even_ref)

    @pl.when(n == OUT_NT - 1)
    def _():
        used_lane = lax.broadcasted_iota(jnp.int32, (OUT_TM, STAT_LANES), 1) < OUT_NT
        means = zmean_ref[...]
        row_mean = jnp.sum(means, axis=-1, keepdims=True) * (1.0 / OUT_NT)
        dm = jnp.where(used_lane, means - row_mean, 0.0)
        m2 = jnp.sum(zm2_ref[...] + OUT_TN * (dm * dm), axis=-1, keepdims=True)
        zmean_ref[:, 0:1] = row_mean
        zm2_ref[:, 0:1] = lax.rsqrt(m2 * (1.0 / D_MODEL) + LN_EPS)

        def body(c, carry):
            rows = pl.ds(pl.multiple_of(c * OUT_ROW_CHUNK, OUT_ROW_CHUNK), OUT_ROW_CHUNK)
            mean = zmean_ref[rows, 0:1]
            rstd = zm2_ref[rows, 0:1]
            for k in range(OUT_NT):
                cols = slice(k * OUT_TN, (k + 1) * OUT_TN)
                o_ref[rows, cols] = (z_ref[k, rows, :] - mean) * rstd * g1_ref[:, cols] + b1_ref[:, cols]
            return carry

        lax.fori_loop(0, OUT_TM // OUT_ROW_CHUNK, body, 0)


def _out_proj(x2d, mu, rstd, g0, b0, ymix0, w, g1, b1, p, conv_w, sink):
    assert OUT_NT == 2 * (OUT_TM // BLOCK)
    grid = (SEQ // OUT_TM, OUT_NT)
    row_stat = pl.BlockSpec((OUT_TM, 1), lambda i, n: (i, 0))
    col_vec = pl.BlockSpec((1, OUT_TN), lambda i, n: (0, n))
    full_vec = pl.BlockSpec((1, D_MODEL), lambda i, n: (0, 0))
    in_specs = [
        pl.BlockSpec(memory_space=pltpu.SMEM),
        pl.BlockSpec((OUT_TM, D_MODEL), lambda i, n: (0, 0),
                     pipeline_mode=pl.Buffered(1)),
        pl.BlockSpec((OUT_TM, OUT_TN), lambda i, n: (i, n)),
        row_stat, row_stat,
        col_vec, col_vec,
        pl.BlockSpec((D_MODEL, OUT_TN), lambda i, n: (0, n)),
        full_vec, full_vec,
    ] + _mixer_in_specs(_next_tile_block, lambda i, n: n & 1)
    return pl.pallas_call(
        _out_proj_kernel,
        name="out_proj",
        grid=grid,
        in_specs=in_specs,
        out_specs=pl.BlockSpec((OUT_TM, D_MODEL), lambda i, n: (i, 0)),
        out_shape=jax.ShapeDtypeStruct((SEQ, D_MODEL), jnp.float32),
        scratch_shapes=[
            pltpu.VMEM((OUT_NT, OUT_TM, OUT_TN), jnp.float32),
            pltpu.VMEM((OUT_TM, STAT_LANES), jnp.float32),
            pltpu.VMEM((OUT_TM, STAT_LANES), jnp.float32),
            pltpu.VMEM((OUT_TM, D_MODEL), jnp.bfloat16),
            pltpu.VMEM((OUT_TM, D_MODEL), jnp.bfloat16),
        ],
        compiler_params=pltpu.CompilerParams(
            dimension_semantics=("arbitrary", "arbitrary"),
            vmem_limit_bytes=V7X_VMEM_LIMIT_BYTES),
    )(sink, ymix0, x2d, mu, rstd, g0, b0, w, g1, b1, *([p] * (N_MIXER_REFS - 1)), conv_w)


def kernel(x, emb_ln_g, emb_ln_b, w_in, conv_w, sink, w_out, ln_g, ln_b):
    batch, seq, d_model = x.shape
    assert (batch, seq, d_model) == (1, SEQ, D_MODEL)
    assert w_in.shape == (1, D_MODEL, PROJ_WIDTH) and w_out.shape == (1, D_MODEL, D_MODEL)
    x2d = x.reshape(SEQ, D_MODEL)
    g0 = emb_ln_g.reshape(1, D_MODEL)
    b0 = emb_ln_b.reshape(1, D_MODEL)
    g1 = ln_g.reshape(1, D_MODEL)
    b1 = ln_b.reshape(1, D_MODEL)
    h0 = _entry_norm_first_tile(x2d, g0, b0)
    p, mu, rstd, w_out_bf16 = _in_proj(h0, x2d, g0, b0, w_in[0], w_out[0])
    ymix0 = _mixers_first_tile(p, conv_w[0], sink[0])
    out = _out_proj(x2d, mu, rstd, g0, b0, ymix0, w_out_bf16, g1, b1, p, conv_w[0], sink[0])
    return out.reshape(1, SEQ, D_MODEL)
```

```python
import jax
import jax.numpy as jnp
from jax import lax
from jax.experimental import pallas as pl
from jax.experimental.pallas import tpu as pltpu

D_MODEL = 4096
SEQ = 8192
CONV_WIDTH = D_MODEL // 2
ATTN_WIDTH = D_MODEL - CONV_WIDTH
HEAD_DIM = 128
N_Q_HEADS = ATTN_WIDTH // HEAD_DIM
N_KV_HEADS = N_Q_HEADS // 4
GQA_GROUP = N_Q_HEADS // N_KV_HEADS
KV_WIDTH = N_KV_HEADS * HEAD_DIM
WINDOW = 128
BLOCK = 128
N_BLOCKS = SEQ // BLOCK
DN_ALPHA = 2.0 ** 0.25
LN_EPS = 1e-5
NEG_INF = -1e30
LOG2_E = 1.4426950408889634
PROJ_WIDTH = 4 * CONV_WIDTH + ATTN_WIDTH + 2 * KV_WIDTH + ATTN_WIDTH

OFF_CB = 0
OFF_CC = OFF_CB + CONV_WIDTH
OFF_CH = OFF_CC + CONV_WIDTH
OFF_CZ = OFF_CH + CONV_WIDTH
OFF_Q = OFF_CZ + CONV_WIDTH
OFF_K = OFF_Q + ATTN_WIDTH
OFF_V = OFF_K + KV_WIDTH
OFF_AZ = OFF_V + KV_WIDTH

V7X_VMEM_LIMIT_BYTES = 58 * 1024 * 1024

NORM_TM = 256
BF16_ROWS = 16
IN_TM, IN_TN = 1024, 512
IN_M_PIECE = IN_TM // 2
IN_LN_STEPS = 16
IN_LN_ROWS = IN_TM // IN_LN_STEPS
WOUT_CAST_ROWS = 32
OUT_TM, OUT_TN = 512, 512
OUT_NT = D_MODEL // OUT_TN
OUT_ROW_CHUNK = 64
OUT_N_PIECES = 8
OUT_K_PIECE = D_MODEL // OUT_N_PIECES
OUT_SCORES_AFTER_PIECE = 0
OUT_HEAD_SLOTS = ((), (0, 1), (2, 3), (4, 5), (6,), (7,), (), ())
OUT_PV_AFTER_PIECE = 5
STAT_LANES = 128
assert OUT_NT <= STAT_LANES
HALO_ROWS = 16
CONV_CHUNK = 128
OUT_CONV_SLOTS = ((0, 1), (2,), (3,), (4,), (5,), (6,), (7,), ())
HALF_WIDTH = CONV_WIDTH // 2
assert HALF_WIDTH == ATTN_WIDTH // 2
P_SECTIONS = ("cb", "cc", "ch", "cz", "q", "az")
P_SLAB_WIDTH = len(P_SECTIONS) * HALF_WIDTH
P_KV_START = 2 * P_SLAB_WIDTH
assert P_KV_START + 2 * KV_WIDTH == PROJ_WIDTH
assert len(OUT_HEAD_SLOTS) == len(OUT_CONV_SLOTS) == OUT_N_PIECES
assert sorted(sum(OUT_HEAD_SLOTS, ())) == list(range(N_Q_HEADS // 2))
assert sorted(sum(OUT_CONV_SLOTS, ())) == list(range(HALF_WIDTH // CONV_CHUNK))


def _silu(z):
    return z / (1.0 + jnp.exp(-z))


def _row_stats(x):
    mu = jnp.mean(x, axis=-1, keepdims=True)
    xc = x - mu
    return mu, lax.rsqrt(jnp.mean(xc * xc, axis=-1, keepdims=True) + LN_EPS), xc


def _entry_norm_kernel(x_ref, g_ref, b_ref, h_ref):
    g = g_ref[...]
    b = b_ref[...]

    def chunk(c, carry):
        for r in range(IN_LN_ROWS // BF16_ROWS):
            rows = pl.ds(pl.multiple_of(c * IN_LN_ROWS, IN_LN_ROWS) + r * BF16_ROWS, BF16_ROWS)
            _, rstd, xc = _row_stats(x_ref[rows, :])
            h_ref[rows, :] = (xc * rstd * g + b).astype(jnp.bfloat16)
        return carry

    lax.fori_loop(0, NORM_TM // IN_LN_ROWS, chunk, 0)


def _entry_norm_first_tile(x2d, g, b):
    vec = pl.BlockSpec((1, D_MODEL), lambda i: (0, 0))
    return pl.pallas_call(
        _entry_norm_kernel,
        name="entry_norm",
        grid=(IN_TM // NORM_TM,),
        in_specs=[pl.BlockSpec((NORM_TM, D_MODEL), lambda i: (i, 0)), vec, vec],
        out_specs=pl.BlockSpec((NORM_TM, D_MODEL), lambda i: (i, 0)),
        out_shape=jax.ShapeDtypeStruct((IN_TM, D_MODEL), jnp.bfloat16),
        compiler_params=pltpu.CompilerParams(
            dimension_semantics=("arbitrary",),
            vmem_limit_bytes=V7X_VMEM_LIMIT_BYTES),
    )(x2d, g, b)


def _in_proj_kernel(h0_ref, x_ref, g_ref, b_ref, w_ref, wo_ref,
                    p_ref, mu_ref, rstd_ref, wob_ref, h_ref):
    i = pl.program_id(0)
    j = pl.program_id(1)

    @pl.when((i == 0) & (j == 0))
    def _():
        h_ref[0] = h0_ref[...]

    slot = i & 1

    w_bf16 = w_ref[...].astype(jnp.bfloat16)

    def project_rows(m):
        rows = slice(m * IN_M_PIECE, (m + 1) * IN_M_PIECE)
        p_ref[rows, :] = jnp.dot(h_ref[slot, rows, :], w_bf16,
                                 preferred_element_type=jnp.float32).astype(jnp.bfloat16)

    project_rows(0)
    chunk = jnp.minimum(j, IN_LN_STEPS - 1)
    base = pl.multiple_of(chunk * IN_LN_ROWS, IN_LN_ROWS)
    g = g_ref[...]
    b = b_ref[...]
    for r in range(IN_LN_ROWS // BF16_ROWS):
        rows = slice(r * BF16_ROWS, (r + 1) * BF16_ROWS)
        mu, rstd, xc = _row_stats(x_ref[rows, :])
        mu_ref[rows, :] = mu
        rstd_ref[rows, :] = rstd
        h_ref[1 - slot, pl.ds(base + r * BF16_ROWS, BF16_ROWS), :] = (
            (xc * rstd * g + b).astype(jnp.bfloat16))
    wob_ref[...] = wo_ref[...].astype(jnp.bfloat16)
    for m in range(1, IN_TM // IN_M_PIECE):
        project_rows(m)


def _in_proj(h0, x2d, g, b, w_in, w_out):
    n_i, n_j = SEQ // IN_TM, PROJ_WIDTH // IN_TN
    n_chunks = SEQ // IN_LN_ROWS
    n_slabs = D_MODEL // WOUT_CAST_ROWS
    assert n_i * n_j >= n_slabs and n_j >= IN_LN_STEPS

    def next_tile_chunk(i, j):
        return (lax.rem(i + 1, n_i) * IN_LN_STEPS + jnp.minimum(j, IN_LN_STEPS - 1), 0)

    def slab(i, j):
        return (jnp.minimum(i * n_j + j, n_slabs - 1), 0)

    def stored_tile(i, j):
        wide_tiles = CONV_WIDTH // IN_TN
        half_tiles = HALF_WIDTH // IN_TN
        assert (wide_tiles, half_tiles) == (4, 2)
        k_tile, v_tile, az_tile = OFF_K // IN_TN, OFF_V // IN_TN, OFF_AZ // IN_TN
        is_az = j >= az_tile
        section = jnp.where(is_az, len(P_SECTIONS) - 1, lax.shift_right_logical(j, 2))
        t = jnp.where(is_az, j - az_tile, j & 3)
        place = (lax.shift_right_logical(t, 1) * (P_SLAB_WIDTH // IN_TN) + section * half_tiles
                 + (t & 1))
        place = jnp.where(j == k_tile, P_KV_START // IN_TN, place)
        place = jnp.where(j == v_tile, P_KV_START // IN_TN + KV_WIDTH // IN_TN, place)
        return (i, place)

    vec = pl.BlockSpec((1, D_MODEL), lambda i, j: (0, 0))
    return pl.pallas_call(
        _in_proj_kernel,
        name="in_proj",
        grid=(n_i, n_j),
        in_specs=[
            pl.BlockSpec((IN_TM, D_MODEL), lambda i, j: (0, 0), pipeline_mode=pl.Buffered(1)),
            pl.BlockSpec((IN_LN_ROWS, D_MODEL), next_tile_chunk),
            vec, vec,
            pl.BlockSpec((D_MODEL, IN_TN), lambda i, j: (0, j)),
            pl.BlockSpec((WOUT_CAST_ROWS, D_MODEL), slab),
        ],
        out_specs=[
            pl.BlockSpec((IN_TM, IN_TN), stored_tile),
            pl.BlockSpec((IN_LN_ROWS, 1), next_tile_chunk),
            pl.BlockSpec((IN_LN_ROWS, 1), next_tile_chunk),
            pl.BlockSpec((WOUT_CAST_ROWS, D_MODEL), slab),
        ],
        out_shape=[
            jax.ShapeDtypeStruct((SEQ, PROJ_WIDTH), jnp.bfloat16),
            jax.ShapeDtypeStruct((SEQ, 1), jnp.float32),
            jax.ShapeDtypeStruct((SEQ, 1), jnp.float32),
            jax.ShapeDtypeStruct((D_MODEL, D_MODEL), jnp.bfloat16),
        ],
        scratch_shapes=[pltpu.VMEM((2, IN_TM, D_MODEL), jnp.bfloat16)],
        compiler_params=pltpu.CompilerParams(
            dimension_semantics=("arbitrary", "arbitrary"),
            vmem_limit_bytes=V7X_VMEM_LIMIT_BYTES),
    )(h0, x2d, g, b, w_in, w_out)


class _MixerHalfBlock:
    N_CONV_CHUNKS = HALF_WIDTH // CONV_CHUNK
    N_KV_GROUPS = N_KV_HEADS // 2

    def __init__(self, blk, half, sink_ref, main_ref, halo_prev_ref, halo_next_ref,
                 kv_prev_ref, kv_cur_ref, kv_next_ref, cw_ref, store):
        self.blk, self.half, self.sink_ref, self.store = blk, half, sink_ref, store

        def section(ref, s):
            return ref.at[:, s * HALF_WIDTH:(s + 1) * HALF_WIDTH]

        self.cb_ref, self.cc_ref, self.ch_ref, self.cz_ref, self.q_ref, self.az_ref = (
            section(main_ref, s) for s in range(len(P_SECTIONS)))
        self.ccp_ref, self.chp_ref = section(halo_prev_ref, 1), section(halo_prev_ref, 2)
        self.ccn_ref, self.chn_ref = section(halo_next_ref, 1), section(halo_next_ref, 2)
        self.cw_ref = cw_ref
        half_kv = KV_WIDTH // 2
        k0 = pl.multiple_of(half * half_kv, half_kv)
        v0 = pl.multiple_of(KV_WIDTH + half * half_kv, half_kv)
        kv_refs = (kv_prev_ref, kv_cur_ref, kv_next_ref)
        self.k_refs = tuple(r.at[:, pl.ds(k0, half_kv)] for r in kv_refs)
        self.v_refs = tuple(r.at[:, pl.ds(v0, half_kv)] for r in kv_refs)
        self._mask = None

    def conv_chunk(self, c):
        f32 = jnp.float32
        cols = slice(c * CONV_CHUNK, (c + 1) * CONV_CHUNK)
        row = lax.broadcasted_iota(jnp.int32, (BLOCK, CONV_CHUNK), 0)
        u = self.cc_ref[:, cols].astype(f32) * self.ch_ref[:, cols].astype(f32)
        u_prev = jnp.where(self.blk > 0,
                           self.ccp_ref[HALO_ROWS - 1:HALO_ROWS, cols].astype(f32)
                           * self.chp_ref[HALO_ROWS - 1:HALO_ROWS, cols].astype(f32), 0.0)
        u_next = jnp.where(self.blk < N_BLOCKS - 1,
                           self.ccn_ref[0:1, cols].astype(f32)
                           * self.chn_ref[0:1, cols].astype(f32), 0.0)
        um1 = jnp.where(row == 0, u_prev, pltpu.roll(u, 1, axis=0))
        up1 = jnp.where(row == BLOCK - 1, u_next, pltpu.roll(u, BLOCK - 1, axis=0))
        cw_ref = self.cw_ref
        conv = um1 * cw_ref[0:1, cols] + u * cw_ref[1:2, cols] + up1 * cw_ref[2:3, cols]
        y = self.cb_ref[:, cols].astype(f32) * conv * _silu(self.cz_ref[:, cols].astype(f32))
        self.store(0, c * CONV_CHUNK, CONV_CHUNK, y.astype(jnp.bfloat16))

    def _band_mask(self):
        if self._mask is None:
            band = 3 * BLOCK
            qi = lax.broadcasted_iota(jnp.int32, (BLOCK, band), 0)
            kj = lax.broadcasted_iota(jnp.int32, (BLOCK, band), 1) - BLOCK
            dist_i = jnp.abs(qi - kj)
            k_pos = kj + self.blk * BLOCK
            valid = (dist_i <= WINDOW) & (k_pos >= 0) & (k_pos < SEQ)
            self._mask = (valid, dist_i.astype(jnp.float32) * -LOG2_E)
        return self._mask

    @staticmethod
    def _block_diagonal(blocks):
        zero = jnp.zeros_like(blocks[0])
        return jnp.concatenate([jnp.concatenate([blocks[0], zero], axis=1),
                                jnp.concatenate([zero, blocks[1]], axis=1)], axis=0)

    def scores(self):
        assert self.N_KV_GROUPS == 2
        q = jnp.concatenate(
            [jnp.concatenate(
                [self.q_ref[:, (kg * GQA_GROUP + g) * HEAD_DIM:(kg * GQA_GROUP + g + 1) * HEAD_DIM]
                 for kg in range(self.N_KV_GROUPS)], axis=1)
             for g in range(GQA_GROUP)], axis=0)
        keys = self._block_diagonal(
            [jnp.concatenate([r[:, kg * HEAD_DIM:(kg + 1) * HEAD_DIM] for r in self.k_refs], axis=0)
             for kg in range(self.N_KV_GROUPS)])
        return lax.dot_general(q, keys, (((1,), (1,)), ((), ())),
                               preferred_element_type=jnp.float32)

    def softmax_head(self, kg, g, scores):
        valid, neg_dist_log2 = self._band_mask()
        heads_per_half = N_Q_HEADS // 2
        half_slope = jnp.where(self.half == 1, 2.0 ** (-8.0 * heads_per_half / N_Q_HEADS),
                               1.0).astype(jnp.float32)
        local = kg * GQA_GROUP + g
        slope = half_slope * (2.0 ** (-8.0 * (local + 1) / N_Q_HEADS))
        sink = self.sink_ref[self.half * heads_per_half + local] * LOG2_E
        band = 3 * BLOCK
        s = (scores[g * BLOCK:(g + 1) * BLOCK, kg * band:(kg + 1) * band]
             * (HEAD_DIM ** -0.5 * LOG2_E) + neg_dist_log2 * slope)
        s = jnp.where(valid, s, NEG_INF)
        m = jnp.maximum(jnp.max(s, axis=-1, keepdims=True), sink)
        e = jnp.exp2(s - m)
        return e.astype(jnp.bfloat16), jnp.sum(e, axis=-1, keepdims=True) + jnp.exp2(sink - m)

    def weighted_values(self, heads):
        probs = jnp.concatenate(
            [jnp.concatenate([heads[kg * GQA_GROUP + g][0] for kg in range(self.N_KV_GROUPS)], axis=1)
             for g in range(GQA_GROUP)], axis=0)
        values = self._block_diagonal(
            [jnp.concatenate([r[:, kg * HEAD_DIM:(kg + 1) * HEAD_DIM] for r in self.v_refs], axis=0)
             for kg in range(self.N_KV_GROUPS)])
        return jnp.dot(probs, values, preferred_element_type=jnp.float32)

    def finish(self, outs, heads):
        for kg in range(self.N_KV_GROUPS):
            for g in range(GQA_GROUP):
                local = kg * GQA_GROUP + g
                o = (outs[g * BLOCK:(g + 1) * BLOCK, kg * HEAD_DIM:(kg + 1) * HEAD_DIM]
                     / heads[local][1])
                gate = _silu(
                    self.az_ref[:, local * HEAD_DIM:(local + 1) * HEAD_DIM].astype(jnp.float32))
                self.store(CONV_WIDTH, local * HEAD_DIM, HEAD_DIM, (o * gate).astype(jnp.bfloat16))

    def run_all(self):
        for c in range(self.N_CONV_CHUNKS):
            self.conv_chunk(c)
        scores = self.scores()
        heads = [self.softmax_head(kg, g, scores)
                 for kg in range(self.N_KV_GROUPS) for g in range(GQA_GROUP)]
        self.finish(self.weighted_values(heads), heads)


N_MIXER_REFS = 7


def _mixer_in_specs(blk_of, half_of):
    halo_per_block = BLOCK // HALO_ROWS
    n_halo = SEQ // HALO_ROWS

    def slab(rows, row_fn):
        return pl.BlockSpec((rows, P_SLAB_WIDTH),
                            lambda *idx: (row_fn(blk_of(*idx)), half_of(*idx)))

    def kv(row_fn):
        return pl.BlockSpec((BLOCK, 2 * KV_WIDTH),
                            lambda *idx: (row_fn(blk_of(*idx)), P_KV_START // (2 * KV_WIDTH)))

    cur = lambda b: b
    prev_blk = lambda b: jnp.maximum(b - 1, 0)
    next_blk = lambda b: jnp.minimum(b + 1, N_BLOCKS - 1)
    prev_halo = lambda b: jnp.maximum(b * halo_per_block - 1, 0)
    next_halo = lambda b: jnp.minimum((b + 1) * halo_per_block, n_halo - 1)
    return [
        slab(BLOCK, cur),
        slab(HALO_ROWS, prev_halo),
        slab(HALO_ROWS, next_halo),
        kv(prev_blk), kv(cur), kv(next_blk),
        pl.BlockSpec((3, HALF_WIDTH), lambda *idx: (0, half_of(*idx))),
    ]


def _mixers_first_tile_kernel(sink_ref, *refs):
    mixer_refs, y_ref = refs[:N_MIXER_REFS], refs[N_MIXER_REFS]
    blk = pl.program_id(0)
    half = pl.program_id(1)

    def store(section, col, width, value):
        start = pl.multiple_of(half * HALF_WIDTH + (section + col), HEAD_DIM)
        y_ref[:, pl.ds(start, width)] = value

    _MixerHalfBlock(blk, half, sink_ref, *mixer_refs, store).run_all()


def _mixers_first_tile(p, conv_w, sink):
    in_specs = [pl.BlockSpec(memory_space=pltpu.SMEM)] + _mixer_in_specs(
        lambda b, hf: b, lambda b, hf: hf)
    return pl.pallas_call(
        _mixers_first_tile_kernel,
        name="mixers",
        grid=(OUT_TM // BLOCK, 2),
        in_specs=in_specs,
        out_specs=pl.BlockSpec((BLOCK, D_MODEL), lambda b, hf: (b, 0)),
        out_shape=jax.ShapeDtypeStruct((OUT_TM, D_MODEL), jnp.bfloat16),
        compiler_params=pltpu.CompilerParams(
            dimension_semantics=("arbitrary", "arbitrary"),
            vmem_limit_bytes=V7X_VMEM_LIMIT_BYTES),
    )(sink, *([p] * (N_MIXER_REFS - 1)), conv_w)


def _next_tile_block(i, n):
    return jnp.minimum((i + 1) * (OUT_TM // BLOCK) + lax.shift_right_logical(n, 1), N_BLOCKS - 1)


def _out_proj_kernel(sink_ref, y0_ref, x_ref, mu_ref, rstd_ref, g0_ref, b0_ref, w_ref,
                     g1_ref, b1_ref, *refs):
    mixer_refs = refs[:N_MIXER_REFS]
    o_ref, z_ref, zmean_ref, zm2_ref, y_even_ref, y_odd_ref = refs[N_MIXER_REFS:]
    i = pl.program_id(0)
    n = pl.program_id(1)

    @pl.when((i == 0) & (n == 0))
    def _():
        y_even_ref[...] = y0_ref[...]

    @pl.when(n == 0)
    def _():
        zmean_ref[...] = jnp.zeros_like(zmean_ref)
        zm2_ref[...] = jnp.zeros_like(zm2_ref)

    def step(cur_ref, nxt_ref):
        half = n & 1
        row0 = pl.multiple_of(lax.shift_right_logical(n, 1) * BLOCK, BLOCK)

        def store(section, col, width, value):
            start = pl.multiple_of(half * HALF_WIDTH + (section + col), HEAD_DIM)
            nxt_ref[pl.ds(row0, BLOCK), pl.ds(start, width)] = value

        mx = _MixerHalfBlock(_next_tile_block(i, n), half, sink_ref, *mixer_refs, store)

        def dot_piece(k):
            ks = slice(k * OUT_K_PIECE, (k + 1) * OUT_K_PIECE)
            return jnp.dot(cur_ref[:, ks], w_ref[ks, :], preferred_element_type=jnp.float32)

        band_scores = None
        heads = []
        y = None
        outs = None
        for k in range(OUT_N_PIECES):
            piece = dot_piece(k)
            y = piece if y is None else y + piece
            for head in OUT_HEAD_SLOTS[k]:
                kg, g = divmod(head, GQA_GROUP)
                heads.append(mx.softmax_head(kg, g, band_scores))
            for c in OUT_CONV_SLOTS[k]:
                mx.conv_chunk(c)
            if k == OUT_SCORES_AFTER_PIECE:
                band_scores = mx.scores()
            if k == OUT_PV_AFTER_PIECE:
                outs = mx.weighted_values(heads)
            if k == OUT_PV_AFTER_PIECE + 1:
                mx.finish(outs, heads)

        ag0 = DN_ALPHA * g0_ref[...]
        ab0 = DN_ALPHA * b0_ref[...]
        own_lane = lax.broadcasted_iota(jnp.int32, (OUT_ROW_CHUNK, STAT_LANES), 1) == n
        for c in range(OUT_TM // OUT_ROW_CHUNK):
            rows = slice(c * OUT_ROW_CHUNK, (c + 1) * OUT_ROW_CHUNK)
            z = ((x_ref[rows, :] - mu_ref[rows, :]) * rstd_ref[rows, :] * ag0 + ab0) + y[rows, :]
            z_ref[n, rows, :] = z
            zm = jnp.mean(z, axis=-1, keepdims=True)
            zc = z - zm
            zmean_ref[rows, :] = jnp.where(own_lane, zm, zmean_ref[rows, :])
            zm2_ref[rows, :] = jnp.where(own_lane, jnp.sum(zc * zc, axis=-1, keepdims=True),
                                         zm2_ref[rows, :])

    @pl.when((i & 1) == 0)
    def _():
        step(y_even_ref, y_odd_ref)

    @pl.when((i & 1) == 1)
    def _():
        step(y_odd_ref, y_even_ref)

    @pl.when(n == OUT_NT - 1)
    def _():
        used_lane = lax.broadcasted_iota(jnp.int32, (OUT_TM, STAT_LANES), 1) < OUT_NT
        means = zmean_ref[...]
        row_mean = jnp.sum(means, axis=-1, keepdims=True) * (1.0 / OUT_NT)
        dm = jnp.where(used_lane, means - row_mean, 0.0)
        m2 = jnp.sum(zm2_ref[...] + OUT_TN * (dm * dm), axis=-1, keepdims=True)
        zmean_ref[:, 0:1] = row_mean
        zm2_ref[:, 0:1] = lax.rsqrt(m2 * (1.0 / D_MODEL) + LN_EPS)

        def body(c, carry):
            rows = pl.ds(pl.multiple_of(c * OUT_ROW_CHUNK, OUT_ROW_CHUNK), OUT_ROW_CHUNK)
            mean = zmean_ref[rows, 0:1]
            rstd = zm2_ref[rows, 0:1]
            for k in range(OUT_NT):
                cols = slice(k * OUT_TN, (k + 1) * OUT_TN)
                o_ref[rows, cols] = (z_ref[k, rows, :] - mean) * rstd * g1_ref[:, cols] + b1_ref[:, cols]
            return carry

        lax.fori_loop(0, OUT_TM // OUT_ROW_CHUNK, body, 0)


def _out_proj(x2d, mu, rstd, g0, b0, ymix0, w, g1, b1, p, conv_w, sink):
    assert OUT_NT == 2 * (OUT_TM // BLOCK)
    grid = (SEQ // OUT_TM, OUT_NT)
    row_stat = pl.BlockSpec((OUT_TM, 1), lambda i, n: (i, 0))
    col_vec = pl.BlockSpec((1, OUT_TN), lambda i, n: (0, n))
    full_vec = pl.BlockSpec((1, D_MODEL), lambda i, n: (0, 0))
    in_specs = [
        pl.BlockSpec(memory_space=pltpu.SMEM),
        pl.BlockSpec((OUT_TM, D_MODEL), lambda i, n: (0, 0),
                     pipeline_mode=pl.Buffered(1)),
        pl.BlockSpec((OUT_TM, OUT_TN), lambda i, n: (i, n)),
        row_stat, row_stat,
        col_vec, col_vec,
        pl.BlockSpec((D_MODEL, OUT_TN), lambda i, n: (0, n)),
        full_vec, full_vec,
    ] + _mixer_in_specs(_next_tile_block, lambda i, n: n & 1)
    return pl.pallas_call(
        _out_proj_kernel,
        name="out_proj",
        grid=grid,
        in_specs=in_specs,
        out_specs=pl.BlockSpec((OUT_TM, D_MODEL), lambda i, n: (i, 0)),
        out_shape=jax.ShapeDtypeStruct((SEQ, D_MODEL), jnp.float32),
        scratch_shapes=[
            pltpu.VMEM((OUT_NT, OUT_TM, OUT_TN), jnp.float32),
            pltpu.VMEM((OUT_TM, STAT_LANES), jnp.float32),
            pltpu.VMEM((OUT_TM, STAT_LANES), jnp.float32),
            pltpu.VMEM((OUT_TM, D_MODEL), jnp.bfloat16),
            pltpu.VMEM((OUT_TM, D_MODEL), jnp.bfloat16),
        ],
        compiler_params=pltpu.CompilerParams(
            dimension_semantics=("arbitrary", "arbitrary"),
            vmem_limit_bytes=V7X_VMEM_LIMIT_BYTES),
    )(sink, ymix0, x2d, mu, rstd, g0, b0, w, g1, b1, *([p] * (N_MIXER_REFS - 1)), conv_w)


def kernel(x, emb_ln_g, emb_ln_b, w_in, conv_w, sink, w_out, ln_g, ln_b):
    batch, seq, d_model = x.shape
    assert (batch, seq, d_model) == (1, SEQ, D_MODEL)
    assert w_in.shape == (1, D_MODEL, PROJ_WIDTH) and w_out.shape == (1, D_MODEL, D_MODEL)
    x2d = x.reshape(SEQ, D_MODEL)
    g0 = emb_ln_g.reshape(1, D_MODEL)
    b0 = emb_ln_b.reshape(1, D_MODEL)
    g1 = ln_g.reshape(1, D_MODEL)
    b1 = ln_b.reshape(1, D_MODEL)
    h0 = _entry_norm_first_tile(x2d, g0, b0)
    p, mu, rstd, w_out_bf16 = _in_proj(h0, x2d, g0, b0, w_in[0], w_out[0])
    ymix0 = _mixers_first_tile(p, conv_w[0], sink[0])
    out = _out_proj(x2d, mu, rstd, g0, b0, ymix0, w_out_bf16, g1, b1, p, conv_w[0], sink[0])
    return out.reshape(1, SEQ, D_MODEL)
```
